```python
import math
import jax, jax.numpy as jnp
from jax import lax
import numpy as np

D_MODEL = 2048
BATCH = 4
SEQ = 8192
DEPTH = 1

D_MIX = D_MODEL
ATT_WIDTH = D_MIX // 2
ATT_HEADS = 16
ATT_HEAD_DIM = ATT_WIDTH // ATT_HEADS
DILATED_PATTERNS = ((128, 1), (512, 4), (2048, 16))
GLA_WIDTH = D_MIX - ATT_WIDTH
GLA_HEADS = 4
GLA_KEY_WIDTH = GLA_WIDTH // 2
GLA_DK = GLA_KEY_WIDTH // GLA_HEADS
GLA_DV = GLA_WIDTH // GLA_HEADS
GLA_GATE_RANK = 16
GLA_GATE_NORM = 16.0
GLA_CHUNK = 64
REL_BUCKETS = 32
REL_MAX_DIST = 1024
EPS = 1e-6
NEG_INF = -1e30

PROJ_SIZES = (ATT_WIDTH, ATT_WIDTH, ATT_WIDTH, ATT_WIDTH,
              GLA_KEY_WIDTH, GLA_KEY_WIDTH, GLA_WIDTH, GLA_WIDTH,
              GLA_GATE_RANK, GLA_GATE_RANK)
PROJ_COLS = int(sum(PROJ_SIZES))
PROJ_SPLITS = [int(s) for s in np.cumsum(PROJ_SIZES)[:-1]]

kernel_name = "hybrid_dilated_attn_gla_block"


def rms_norm(x):
    xf = x.astype(jnp.float32)
    return (xf * lax.rsqrt(jnp.mean(xf * xf, axis=-1, keepdims=True) + EPS)).astype(x.dtype)


def t5_bucket_np(rel):
    nb = REL_BUCKETS // 2
    max_exact = nb // 2
    n = np.abs(rel)
    large = max_exact + (np.log(np.maximum(n, 1) / max_exact)
                         / np.log(REL_MAX_DIST / max_exact) * (nb - max_exact)).astype(np.int32)
    large = np.minimum(large, nb - 1)
    return (np.where(rel > 0, nb, 0) + np.where(n < max_exact, n, large)).astype(np.int32)


def dilated_window_attention(q, k, v, rel_bias, window, dilation):
    B, S, H, E = q.shape
    w = window // (2 * dilation)
    L = S // dilation
    nb = -(-L // w)
    Lp = nb * w

    def residue_layout(t, pad_lo, pad_hi):
        t = t.reshape(B, L, dilation, H, E)
        return jnp.pad(t, ((0, 0), (pad_lo, pad_hi), (0, 0), (0, 0), (0, 0)))

    qb = residue_layout(q, 0, Lp - L).reshape(B, nb, w, dilation, H, E)

    def key_windows(t):
        tb = residue_layout(t, w, Lp - L + w).reshape(B, nb + 2, w, dilation, H, E)
        return jnp.concatenate([tb[:, :-2], tb[:, 1:-1], tb[:, 2:]], axis=2)

    kw = key_windows(k)
    vw = key_windows(v)
    s = jnp.einsum('bnqrhe,bnkrhe->bhrnqk', qb, kw).astype(jnp.float32) * (E ** -0.5)

    qi = np.arange(w)[:, None]
    kj = np.arange(3 * w)[None, :]
    step = kj - w - qi
    in_window = np.abs(step) <= w
    bucket = t5_bucket_np(step * dilation)
    kpos = np.arange(nb)[:, None] * w + np.arange(3 * w)[None, :] - w
    key_ok = (kpos >= 0) & (kpos < L)
    mask = in_window[None] & key_ok[:, None, :]

    bias = jnp.transpose(rel_bias[bucket], (2, 0, 1)).astype(jnp.float32)
    s = s + bias[None, :, None, None]
    s = jnp.where(mask, s, NEG_INF)
    m = jnp.max(s, axis=-1, keepdims=True)
    p = jnp.exp(s - m)
    den = jnp.sum(p, axis=-1, keepdims=True)
    o = jnp.einsum('bhrnqk,bnkrhe->bhrnqe', (p / den).astype(v.dtype), vw)
    lse = (m + jnp.log(den))[..., 0]
    o = o.transpose(0, 3, 4, 2, 1, 5).reshape(B, Lp, dilation, H, E)[:, :L].reshape(B, S, H, E)
    lse = lse.transpose(0, 3, 4, 2, 1).reshape(B, Lp, dilation, H)[:, :L].reshape(B, S, H)
    return o, lse


def gla_chunked(q, k, v, log_g):
    B, S, H, dk = q.shape
    dv = v.shape[-1]
    C = GLA_CHUNK
    N = S // C

    def chunks(t):
        return t.reshape(B, N, C, H, t.shape[-1]).transpose(0, 3, 1, 2, 4)

    q, k, v, log_g = chunks(q), chunks(k), chunks(v), chunks(log_g)
    b = jnp.cumsum(log_g.astype(jnp.float32), axis=3)
    b_last = b[:, :, :, -1:]
    qf = q.astype(jnp.float32) * jnp.exp(b) * (dk ** -0.5)
    kf = k.astype(jnp.float32)
    vf = v.astype(jnp.float32)
    causal = np.tril(np.ones((C, C), dtype=bool))
    att = jnp.einsum('bhnid,bhnjd->bhnij', qf, kf * jnp.exp(-b))
    att = jnp.where(causal, att, 0.0)
    o_intra = jnp.einsum('bhnij,bhnjv->bhniv', att, vf)
    kv = jnp.einsum('bhnjd,bhnjv->bhndv', kf * jnp.exp(b_last - b), vf)
    decay = jnp.exp(b_last[:, :, :, 0])

    def step(state, inp):
        kv_n, g_n = inp
        return g_n[..., None] * state + kv_n, state

    init = jnp.zeros((B, H, dk, dv), jnp.float32)
    _, states = lax.scan(step, init, (jnp.moveaxis(kv, 2, 0), jnp.moveaxis(decay, 2, 0)))
    states = jnp.moveaxis(states, 0, 2)
    o = o_intra + jnp.einsum('bhnid,bhndv->bhniv', qf, states)
    return o.transpose(0, 2, 3, 1, 4).reshape(B, S, H, dv).astype(v.dtype)


def setup_inputs(seed: int = 0) -> dict:
    key = jax.random.key(seed)
    ks = jax.random.split(key, 14)
    f32 = jnp.float32
    x = jax.random.normal(ks[0], (BATCH, SEQ, D_MODEL), f32)
    c = jax.random.normal(ks[1], (BATCH, D_MODEL), f32)
    w_cond = jax.random.normal(ks[2], (DEPTH, D_MODEL, 3 * D_MODEL), f32) * (0.5 * D_MODEL ** -0.5)
    b_cond = jax.random.normal(ks[3], (DEPTH, 3 * D_MODEL), f32) * 0.02
    w_in = jax.random.normal(ks[4], (DEPTH, D_MODEL, PROJ_COLS), f32) * (D_MODEL ** -0.5)
    gla_gate_up_fwd = jax.random.normal(ks[5], (DEPTH, GLA_GATE_RANK, GLA_KEY_WIDTH), f32) * (GLA_GATE_RANK ** -0.5)
    gla_gate_bias_fwd = jax.random.normal(ks[6], (DEPTH, GLA_KEY_WIDTH), f32) * 0.01
    gla_gate_up_bwd = jax.random.normal(ks[7], (DEPTH, GLA_GATE_RANK, GLA_KEY_WIDTH), f32) * (GLA_GATE_RANK ** -0.5)
    gla_gate_bias_bwd = jax.random.normal(ks[8], (DEPTH, GLA_KEY_WIDTH), f32) * 0.01
    gla_norm_gain = 1.0 + 0.02 * jax.random.normal(ks[9], (DEPTH, GLA_WIDTH), f32)
    rel_bias = jax.random.normal(ks[10], (REL_BUCKETS, ATT_HEADS), f32) * 0.5
    w_out = jax.random.normal(ks[11], (DEPTH, D_MIX, D_MODEL), f32) * (D_MIX ** -0.5)
    final_gain = 1.0 + 0.02 * jax.random.normal(ks[12], (D_MODEL,), f32)
    return {"x": x, "c": c, "w_cond": w_cond, "b_cond": b_cond, "w_in": w_in,
            "gla_gate_up_fwd": gla_gate_up_fwd, "gla_gate_bias_fwd": gla_gate_bias_fwd,
            "gla_gate_up_bwd": gla_gate_up_bwd, "gla_gate_bias_bwd": gla_gate_bias_bwd,
            "gla_norm_gain": gla_norm_gain, "rel_bias": rel_bias, "w_out": w_out,
            "final_gain": final_gain}


def reference(x, c, w_cond, b_cond, w_in, gla_gate_up_fwd, gla_gate_bias_fwd,
              gla_gate_up_bwd, gla_gate_bias_bwd, gla_norm_gain, rel_bias, w_out, final_gain):
    B, S, _ = x.shape
    for layer in range(DEPTH):
        mod = jax.nn.silu(c) @ w_cond[layer] + b_cond[layer]
        shift, scale, gate = jnp.split(mod, 3, axis=-1)
        h = rms_norm(x) * (1.0 + scale[:, None]) + shift[:, None]

        proj = h @ w_in[layer]
        aq, ak, av, ag, gq, gk, gv, gg, lr_f, lr_b = jnp.split(proj, PROJ_SPLITS, axis=-1)

        hs = (B, S, ATT_HEADS, ATT_HEAD_DIM)
        aq, ak, av = aq.reshape(hs), ak.reshape(hs), av.reshape(hs)
        outs, lses = [], []
        for window, dilation in DILATED_PATTERNS:
            o_p, lse_p = dilated_window_attention(aq, ak, av, rel_bias, window, dilation)
            outs.append(o_p)
            lses.append(lse_p)
        mix_w = jax.nn.softmax(jnp.stack(lses, axis=0), axis=0)
        att = jnp.einsum('pbsh,pbshe->bshe', mix_w.astype(av.dtype), jnp.stack(outs, axis=0))
        a_out = att.reshape(B, S, ATT_WIDTH) * jax.nn.silu(ag)

        ks_ = (B, S, GLA_HEADS, GLA_DK)
        gq, gk = gq.reshape(ks_), gk.reshape(ks_)
        gv = gv.reshape(B, S, GLA_HEADS, GLA_DV)
        log_g_f = (jax.nn.log_sigmoid((lr_f @ gla_gate_up_fwd[layer] + gla_gate_bias_fwd[layer]).astype(jnp.float32))
                   / GLA_GATE_NORM).reshape(ks_)
        log_g_b = (jax.nn.log_sigmoid((lr_b @ gla_gate_up_bwd[layer] + gla_gate_bias_bwd[layer]).astype(jnp.float32))
                   / GLA_GATE_NORM).reshape(ks_)
        o_fwd = gla_chunked(gq, gk, gv, log_g_f)
        o_bwd = jnp.flip(gla_chunked(jnp.flip(gq, 1), jnp.flip(gk, 1), jnp.flip(gv, 1),
                                     jnp.flip(log_g_b, 1)), 1)
        g_o = rms_norm(o_fwd + o_bwd).reshape(B, S, GLA_WIDTH) * gla_norm_gain[layer]
        g_out = g_o * jax.nn.silu(gg)

        y = jnp.concatenate([a_out, g_out], axis=-1) @ w_out[layer]
        x = x + gate[:, None] * y
    return rms_norm(x) * final_gain
```

```python
import functools
import math

import jax
import jax.numpy as jnp
import numpy as np
from jax import lax
from jax.experimental import pallas as pl
from jax.experimental.pallas import tpu as pltpu

D_MODEL = 2048
ATT_WIDTH = 1024
ATT_HEADS = 16
ATT_HEAD_DIM = 64
DILATED_PATTERNS = ((128, 1), (512, 4), (2048, 16))
ATT_STEPS = 64
GLA_WIDTH = 1024
GLA_HEADS = 4
GLA_KEY_WIDTH = 512
GLA_DK = 128
GLA_DV = 256
GLA_GATE_RANK = 16
GLA_GATE_NORM = 16.0
GLA_CHUNK = 64
REL_BUCKETS = 32
REL_MAX_DIST = 1024
EPS = 1e-6
NEG_INF = -1e30

PROJ_WIDTH = 4 * ATT_WIDTH + 2 * GLA_KEY_WIDTH + 2 * GLA_WIDTH
COL_AQ, COL_AK, COL_AV, COL_AG = 0, 1024, 2048, 3072
COL_GQ, COL_GK, COL_GV, COL_GG = 4096, 4608, 5120, 6144
LR_PAD = 128

VMEM_LIMIT = 56 * 1024 * 1024

BF16 = jnp.bfloat16
F32 = jnp.float32


def _params(sem):
    return pltpu.CompilerParams(dimension_semantics=sem, vmem_limit_bytes=VMEM_LIMIT)


def _dot(a, b):
    return jnp.dot(a, b, preferred_element_type=F32)


def _dot_nt(a, b):
    return lax.dot_general(a, b, (((1,), (1,)), ((), ())), preferred_element_type=F32)


def _dot_tn(a, b):
    return lax.dot_general(a, b, (((0,), (0,)), ((), ())), preferred_element_type=F32)


def _split_bf16(x):
    hi = x.astype(BF16)
    lo = (x - hi.astype(F32)).astype(BF16)
    return hi, lo


def _silu(x):
    return x / (1.0 + jnp.exp(-x))


def _mod_kernel(c_ref, w_ref, b_ref, o_ref):
    s = _silu(c_ref[...])
    o_ref[...] = jnp.dot(s, w_ref[...], preferred_element_type=F32,
                         precision=lax.Precision.HIGHEST) + b_ref[...]


def _mod(c, w_cond, b_cond):
    B, D = c.shape
    N = w_cond.shape[1]
    tn = 768
    cp = jnp.pad(c, ((0, 8 - B), (0, 0)))
    out = pl.pallas_call(
        _mod_kernel,
        grid=(N // tn,),
        in_specs=[pl.BlockSpec((8, D), lambda j: (0, 0)),
                  pl.BlockSpec((D, tn), lambda j: (0, j)),
                  pl.BlockSpec((1, tn), lambda j: (0, j))],
        out_specs=pl.BlockSpec((8, tn), lambda j: (0, j)),
        out_shape=jax.ShapeDtypeStruct((8, N), F32),
        compiler_params=_params(("arbitrary",)),
        name="mod",
    )(cp, w_cond, b_cond.reshape(1, N))
    return out[:B]


def _inproj_kernel(x_ref, scale_ref, shift_ref, w_ref, wlr_ref, p_ref, lr_ref, h_scr):
    @pl.when(pl.program_id(1) == 0)
    def _():
        x = x_ref[...]
        ms = jnp.mean(x * x, axis=-1, keepdims=True)
        h = x * lax.rsqrt(ms + EPS) * (1.0 + scale_ref[...]) + shift_ref[...]
        hb = h.astype(BF16)
        h_scr[...] = hb
        lr_ref[...] = _dot(hb, wlr_ref[...])

    p_ref[...] = _dot(h_scr[...], w_ref[...]).astype(BF16)


def _inproj(x2, scale, shift, w_main, w_lr, seq):
    M, D = x2.shape
    tm, tn = 512, 1024
    bpt = seq // tm
    return pl.pallas_call(
        _inproj_kernel,
        grid=(M // tm, PROJ_WIDTH // tn),
        in_specs=[pl.BlockSpec((tm, D), lambda i, j: (i, 0)),
                  pl.BlockSpec((None, 1, D), lambda i, j: (i // bpt, 0, 0)),
                  pl.BlockSpec((None, 1, D), lambda i, j: (i // bpt, 0, 0)),
                  pl.BlockSpec((D, tn), lambda i, j: (0, j)),
                  pl.BlockSpec((D, LR_PAD), lambda i, j: (0, 0))],
        out_specs=[pl.BlockSpec((tm, tn), lambda i, j: (i, j)),
                   pl.BlockSpec((tm, LR_PAD), lambda i, j: (i, 0))],
        out_shape=[jax.ShapeDtypeStruct((M, PROJ_WIDTH), BF16),
                   jax.ShapeDtypeStruct((M, LR_PAD), F32)],
        scratch_shapes=[pltpu.VMEM((tm, D), BF16)],
        compiler_params=_params(("arbitrary", "arbitrary")),
        name="inproj",
    )(x2, scale, shift, w_main, w_lr)


def _t5_bucket_np(rel):
    nb = REL_BUCKETS // 2
    max_exact = nb // 2
    n = np.abs(rel)
    large = max_exact + (np.log(np.maximum(n, 1) / max_exact)
                         / np.log(REL_MAX_DIST / max_exact) * (nb - max_exact)).astype(np.int32)
    large = np.minimum(large, nb - 1)
    return (np.where(rel > 0, nb, 0) + np.where(n < max_exact, n, large)).astype(np.int32)


def _bias_kernel(rbt_ref, bucket_ref, mask_ref, o_ref):
    rbt = rbt_ref[...]
    bucket = bucket_ref[...]
    tbl = jnp.zeros((ATT_HEADS, bucket.shape[1]), F32)
    for b in range(REL_BUCKETS):
        tbl = jnp.where(bucket == b, rbt[:, b:b + 1], tbl)
    for v in range(3):
        o_ref[v] = jnp.where(mask_ref[v] > 0.5, tbl, NEG_INF)


def _bias_tiles(rel_bias, dilation):
    w = ATT_STEPS
    qi = np.arange(w)[:, None]
    kj = np.arange(3 * w)[None, :]
    step = kj - w - qi
    band = np.abs(step) <= w
    bucket = _t5_bucket_np(step * dilation).reshape(1, w * 3 * w)
    masks = np.stack([band & (kj >= w), band, band & (kj < 2 * w)]).astype(np.float32)
    masks = masks.reshape(3, 1, w * 3 * w)
    out = pl.pallas_call(
        _bias_kernel,
        out_shape=jax.ShapeDtypeStruct((3, ATT_HEADS, w * 3 * w), F32),
        name=f"bias_d{dilation}",
    )(rel_bias.T, jnp.asarray(bucket), jnp.asarray(masks))
    return out.reshape(3, ATT_HEADS, w, 3 * w)


def _attn_kernel(q_ref, kp_ref, km_ref, kn_ref, vp_ref, vm_ref, vn_ref, bias_ref,
                 o_ref, lse_ref, kbuf, vbuf, *, tq):
    w = ATT_STEPS
    nsub = tq // w
    i = pl.program_id(2)
    first = i == 0
    last = i == pl.num_programs(2) - 1

    kbuf[0:w] = kp_ref[...]
    kbuf[w:w + tq] = km_ref[...]
    kbuf[w + tq:] = kn_ref[...]
    vbuf[0:w] = vp_ref[...]
    vbuf[w:w + tq] = vm_ref[...]
    vbuf[w + tq:] = vn_ref[...]

    lane = lax.broadcasted_iota(jnp.int32, (w, 128), 1)

    def sub(j, carry):
        qs = pl.multiple_of(j * w, w)
        var = jnp.where(jnp.logical_and(first, j == 0), 0,
                        jnp.where(jnp.logical_and(last, j == nsub - 1), 2, 1))
        lse_tile = jnp.zeros((w, 128), F32)
        for h in range(ATT_HEADS):
            cs = slice(h * ATT_HEAD_DIM, (h + 1) * ATT_HEAD_DIM)
            q = q_ref[pl.ds(qs, w), cs]
            k = kbuf[pl.ds(qs, 3 * w), cs]
            v = vbuf[pl.ds(qs, 3 * w), cs]
            s = _dot_nt(q, k) * (ATT_HEAD_DIM ** -0.5) + bias_ref[var, h]
            m = jnp.max(s, axis=-1, keepdims=True)
            p = jnp.exp(s - m)
            den = jnp.sum(p, axis=-1, keepdims=True)
            o = _dot(p.astype(BF16), v) / den
            o_ref[pl.ds(qs, w), cs] = o.astype(BF16)
            lse_tile = jnp.where(lane == h, m + jnp.log(den), lse_tile)
        lse_ref[pl.ds(qs, w), :] = lse_tile
        return carry

    lax.fori_loop(0, nsub, sub, 0)


def _attn_pattern(proj, bias, batch, seq, dilation):
    w = ATT_STEPS
    L = seq // dilation
    tq = min(512, L)
    nblk = L // tq
    hb = tq // w
    nhalo = L // w
    cpb = PROJ_WIDTH // ATT_WIDTH
    pv = proj.reshape(batch, L, dilation * PROJ_WIDTH)

    def main(col):
        return pl.BlockSpec((None, tq, ATT_WIDTH), lambda b, r, i: (b, i, r * cpb + col))

    def prev(col):
        return pl.BlockSpec((None, w, ATT_WIDTH),
                            lambda b, r, i: (b, jnp.maximum(i * hb - 1, 0), r * cpb + col))

    def nxt(col):
        return pl.BlockSpec((None, w, ATT_WIDTH),
                            lambda b, r, i: (b, jnp.minimum((i + 1) * hb, nhalo - 1), r * cpb + col))

    o, lse = pl.pallas_call(
        functools.partial(_attn_kernel, tq=tq),
        grid=(batch, dilation, nblk),
        in_specs=[main(0), prev(1), main(1), nxt(1), prev(2), main(2), nxt(2),
                  pl.BlockSpec((3, ATT_HEADS, w, 3 * w), lambda b, r, i: (0, 0, 0, 0))],
        out_specs=[pl.BlockSpec((None, tq, ATT_WIDTH), lambda b, r, i: (b, i, r)),
                   pl.BlockSpec((None, tq, 128), lambda b, r, i: (b, i, r))],
        out_shape=[jax.ShapeDtypeStruct((batch, L, dilation * ATT_WIDTH), BF16),
                   jax.ShapeDtypeStruct((batch, L, dilation * 128), F32)],
        scratch_shapes=[pltpu.VMEM((tq + 2 * w, ATT_WIDTH), BF16),
                        pltpu.VMEM((tq + 2 * w, ATT_WIDTH), BF16)],
        compiler_params=_params(("arbitrary", "arbitrary", "arbitrary")),
        name=f"attn_d{dilation}",
    )(pv, pv, pv, pv, pv, pv, pv, bias)
    return o.reshape(batch * seq, ATT_WIDTH), lse.reshape(batch * seq, 128)


def _merge_kernel(o1_ref, o2_ref, o3_ref, l1_ref, l2_ref, l3_ref, ag_ref, e_ref, out_ref):
    l1, l2, l3 = l1_ref[...], l2_ref[...], l3_ref[...]
    mx = jnp.maximum(jnp.maximum(l1, l2), l3)
    e1, e2, e3 = jnp.exp(l1 - mx), jnp.exp(l2 - mx), jnp.exp(l3 - mx)
    inv = 1.0 / (e1 + e2 + e3)
    e = e_ref[...]

    def expand(wgt):
        hi, lo = _split_bf16(wgt)
        return _dot(hi, e) + _dot(lo, e)

    att = (expand(e1 * inv) * o1_ref[...].astype(F32)
           + expand(e2 * inv) * o2_ref[...].astype(F32)
           + expand(e3 * inv) * o3_ref[...].astype(F32))
    out_ref[...] = (att * _silu(ag_ref[...].astype(F32))).astype(BF16)


def _merge(outs, lses, proj):
    M = proj.shape[0]
    tm = 512
    expand = np.zeros((128, ATT_WIDTH), np.float32)
    for h in range(ATT_HEADS):
        expand[h, h * ATT_HEAD_DIM:(h + 1) * ATT_HEAD_DIM] = 1.0
    row = lambda width: pl.BlockSpec((tm, width), lambda i: (i, 0))
    return pl.pallas_call(
        _merge_kernel,
        grid=(M // tm,),
        in_specs=[row(ATT_WIDTH)] * 3 + [row(128)] * 3
                 + [pl.BlockSpec((tm, ATT_WIDTH), lambda i: (i, COL_AG // ATT_WIDTH)),
                    pl.BlockSpec((128, ATT_WIDTH), lambda i: (0, 0))],
        out_specs=row(ATT_WIDTH),
        out_shape=jax.ShapeDtypeStruct((M, ATT_WIDTH), BF16),
        compiler_params=_params(("arbitrary",)),
        name="merge",
    )(*outs, *lses, proj, jnp.asarray(expand, BF16))


def _gla_kernel(*refs, ts, reverse):
    if reverse:
        (q_ref, k_ref, v_ref, lr_ref, up_ref, gb_ref, tri_ref, of_ref, gg_ref, gain_ref,
         out_ref, state) = refs
    else:
        q_ref, k_ref, v_ref, lr_ref, up_ref, gb_ref, tri_ref, out_ref, state = refs
    C = GLA_CHUNK
    nchunk = ts // C

    @pl.when(pl.program_id(2) == 0)
    def _():
        state[...] = jnp.zeros_like(state)

    lr_hi, lr_lo = _split_bf16(lr_ref[...])
    up_hi, up_lo = _split_bf16(up_ref[...])
    z = _dot(lr_hi, up_hi) + _dot(lr_hi, up_lo) + _dot(lr_lo, up_hi) + gb_ref[...]
    log_g = (jnp.minimum(z, 0.0) - jnp.log(1.0 + jnp.exp(-jnp.abs(z)))) * (1.0 / GLA_GATE_NORM)

    tri = tri_ref[...]
    ri = lax.broadcasted_iota(jnp.int32, (C, C), 0)
    ci = lax.broadcasted_iota(jnp.int32, (C, C), 1)
    keep = (ci >= ri) if reverse else (ci <= ri)

    order = range(nchunk - 1, -1, -1) if reverse else range(nchunk)
    for c in order:
        rows = slice(c * C, (c + 1) * C)
        g_hi, g_lo = _split_bf16(log_g[rows])
        b = _dot(tri, g_hi) + _dot(tri, g_lo)
        b_edge = b[0:1] if reverse else b[C - 1:C]
        q = q_ref[rows, :].astype(F32)
        k = k_ref[rows, :].astype(F32)
        v = v_ref[rows, :]
        qf = (q * jnp.exp(b) * (GLA_DK ** -0.5)).astype(BF16)
        kd = (k * jnp.exp(-b)).astype(BF16)
        ks = (k * jnp.exp(b_edge - b)).astype(BF16)
        att = jnp.where(keep, _dot_nt(qf, kd), 0.0)
        st = state[...]
        o = _dot(att.astype(BF16), v) + _dot_nt(qf, st.astype(BF16))
        state[...] = st * jnp.exp(b_edge) + _dot_tn(v, ks)
        if reverse:
            tot = o + of_ref[rows, :]
            ms = jnp.mean(tot * tot, axis=-1, keepdims=True)
            g_o = tot * lax.rsqrt(ms + EPS) * gain_ref[...]
            out_ref[rows, :] = (g_o * _silu(gg_ref[rows, :].astype(F32))).astype(BF16)
        else:
            out_ref[rows, :] = o


def _gla_direction(proj, lr, up_pad, gate_bias, batch, seq, reverse, o_fwd=None, gain=None):
    ts = 512
    nstep = seq // ts
    C = GLA_CHUNK
    p3 = proj.reshape(batch, seq, PROJ_WIDTH)
    lr3 = lr.reshape(batch, seq, LR_PAD)
    tri = np.triu(np.ones((C, C), np.float32)) if reverse else np.tril(np.ones((C, C), np.float32))

    def step(i):
        return nstep - 1 - i if reverse else i

    def seq_block(width, col0):
        return pl.BlockSpec((None, ts, width), lambda b, h, i: (b, step(i), col0 // width + h))

    in_specs = [seq_block(GLA_DK, COL_GQ), seq_block(GLA_DK, COL_GK), seq_block(GLA_DV, COL_GV),
                pl.BlockSpec((None, ts, LR_PAD), lambda b, h, i: (b, step(i), 0)),
                pl.BlockSpec((LR_PAD, GLA_DK), lambda b, h, i: (0, h)),
                pl.BlockSpec((1, GLA_DK), lambda b, h, i: (0, h)),
                pl.BlockSpec((C, C), lambda b, h, i: (0, 0))]
    args = [p3, p3, p3, lr3, up_pad, gate_bias.reshape(1, GLA_KEY_WIDTH), jnp.asarray(tri, BF16)]
    if reverse:
        in_specs += [pl.BlockSpec((None, ts, GLA_DV), lambda b, h, i: (b, step(i), h)),
                     seq_block(GLA_DV, COL_GG),
                     pl.BlockSpec((1, GLA_DV), lambda b, h, i: (0, h))]
        args += [o_fwd, p3, gain.reshape(1, GLA_WIDTH)]
        out_dtype = BF16
    else:
        out_dtype = F32
    return pl.pallas_call(
        functools.partial(_gla_kernel, ts=ts, reverse=reverse),
        grid=(batch, GLA_HEADS, nstep),
        in_specs=in_specs,
        out_specs=pl.BlockSpec((None, ts, GLA_DV), lambda b, h, i: (b, step(i), h)),
        out_shape=jax.ShapeDtypeStruct((batch, seq, GLA_WIDTH), out_dtype),
        scratch_shapes=[pltpu.VMEM((GLA_DV, GLA_DK), F32)],
        compiler_params=_params(("arbitrary", "arbitrary", "arbitrary")),
        name="gla_bwd" if reverse else "gla_fwd",
    )(*args)


def _outproj_kernel(a_ref, g_ref, wa_ref, wg_ref, x_ref, gate_ref, fg_ref, o_ref, *, final):
    y = _dot(a_ref[...], wa_ref[...]) + _dot(g_ref[...], wg_ref[...])
    xn = x_ref[...] + gate_ref[...] * y
    if final:
        ms = jnp.mean(xn * xn, axis=-1, keepdims=True)
        xn = xn * lax.rsqrt(ms + EPS) * fg_ref[...]
    o_ref[...] = xn


def _outproj(a_out, g_out, w_out_bf16, x2, gate, final_gain, seq, final):
    M, D = x2.shape
    tm = 256
    bpt = seq // tm
    return pl.pallas_call(
        functools.partial(_outproj_kernel, final=final),
        grid=(M // tm,),
        in_specs=[pl.BlockSpec((tm, ATT_WIDTH), lambda i: (i, 0)),
                  pl.BlockSpec((tm, GLA_WIDTH), lambda i: (i, 0)),
                  pl.BlockSpec((ATT_WIDTH, D), lambda i: (0, 0)),
                  pl.BlockSpec((GLA_WIDTH, D), lambda i: (1, 0)),
                  pl.BlockSpec((tm, D), lambda i: (i, 0)),
                  pl.BlockSpec((None, 1, D), lambda i: (i // bpt, 0, 0)),
                  pl.BlockSpec((1, D), lambda i: (0, 0))],
        out_specs=pl.BlockSpec((tm, D), lambda i: (i, 0)),
        out_shape=jax.ShapeDtypeStruct((M, D), F32),
        compiler_params=_params(("arbitrary",)),
        name="outproj",
    )(a_out, g_out, w_out_bf16, w_out_bf16, x2, gate, final_gain.reshape(1, D))


def kernel(x, c, w_cond, b_cond, w_in, gla_gate_up_fwd, gla_gate_bias_fwd, gla_gate_up_bwd,
           gla_gate_bias_bwd, gla_norm_gain, rel_bias, w_out, final_gain):
    B, S, D = x.shape
    depth = w_cond.shape[0]
    R = GLA_GATE_RANK
    xs = x.reshape(B * S, D)
    for layer in range(depth):
        mod = _mod(c, w_cond[layer], b_cond[layer])
        shift, scale, gate = [m.reshape(B, 1, D) for m in jnp.split(mod, 3, axis=-1)]

        w_main = w_in[layer, :, :PROJ_WIDTH].astype(BF16)
        w_lr = jnp.pad(w_in[layer, :, PROJ_WIDTH:], ((0, 0), (0, LR_PAD - 2 * R))).astype(BF16)
        proj, lr = _inproj(xs, scale, shift, w_main, w_lr, S)

        outs, lses = [], []
        for _, dilation in DILATED_PATTERNS:
            o_p, lse_p = _attn_pattern(proj, _bias_tiles(rel_bias, dilation), B, S, dilation)
            outs.append(o_p)
            lses.append(lse_p)
        a_out = _merge(outs, lses, proj)

        up_f = jnp.pad(gla_gate_up_fwd[layer], ((0, LR_PAD - R), (0, 0)))
        up_b = jnp.pad(gla_gate_up_bwd[layer], ((R, LR_PAD - 2 * R), (0, 0)))
        o_fwd = _gla_direction(proj, lr, up_f, gla_gate_bias_fwd[layer], B, S, reverse=False)
        g_out = _gla_direction(proj, lr, up_b, gla_gate_bias_bwd[layer], B, S, reverse=True,
                               o_fwd=o_fwd, gain=gla_norm_gain[layer])

        xs = _outproj(a_out, g_out.reshape(B * S, GLA_WIDTH), w_out[layer].astype(BF16),
                      xs, gate, final_gain, S, final=layer == depth - 1)
    return xs.reshape(B, S, D)
```

```python
import functools
import math

import jax
import jax.numpy as jnp
import numpy as np
from jax import lax
from jax.experimental import pallas as pl
from jax.experimental.pallas import tpu as pltpu

D_MODEL = 2048
ATT_WIDTH = 1024
ATT_HEADS = 16
ATT_HEAD_DIM = 64
DILATED_PATTERNS = ((128, 1), (512, 4), (2048, 16))
ATT_STEPS = 64
GLA_WIDTH = 1024
GLA_HEADS = 4
GLA_KEY_WIDTH = 512
GLA_DK = 128
GLA_DV = 256
GLA_GATE_RANK = 16
GLA_GATE_NORM = 16.0
GLA_CHUNK = 64
REL_BUCKETS = 32
REL_MAX_DIST = 1024
EPS = 1e-6
NEG_INF = -1e30

PROJ_WIDTH = 4 * ATT_WIDTH + 2 * GLA_KEY_WIDTH + 2 * GLA_WIDTH
COL_AQ, COL_AK, COL_AV, COL_AG = 0, 1024, 2048, 3072
COL_GQ, COL_GK, COL_GV, COL_GG = 4096, 4608, 5120, 6144
LR_PAD = 128
ATT_QKV_TILES = 3
RESIDUE_DILATIONS = tuple(d for _, d in DILATED_PATTERNS if d > 1)
ATT_TQ = 128
ATT_TK = ATT_TQ + 2 * ATT_STEPS

VMEM_LIMIT = 56 * 1024 * 1024

BF16 = jnp.bfloat16
F32 = jnp.float32


def _params(sem):
    return pltpu.CompilerParams(dimension_semantics=sem, vmem_limit_bytes=VMEM_LIMIT)


def _dot(a, b):
    return jnp.dot(a, b, preferred_element_type=F32)


def _dot_nt(a, b):
    return lax.dot_general(a, b, (((1,), (1,)), ((), ())), preferred_element_type=F32)


def _dot_tn(a, b):
    return lax.dot_general(a, b, (((0,), (0,)), ((), ())), preferred_element_type=F32)


def _split_bf16(x):
    hi = x.astype(BF16)
    lo = (x - hi.astype(F32)).astype(BF16)
    return hi, lo


def _silu(x):
    return x / (1.0 + jnp.exp(-x))


def _mod_kernel(c_ref, w_ref, b_ref, o_ref):
    s = _silu(c_ref[...])
    o_ref[...] = jnp.dot(s, w_ref[...], preferred_element_type=F32,
                         precision=lax.Precision.HIGHEST) + b_ref[...]


def _mod(c, w_cond, b_cond):
    B, D = c.shape
    N = w_cond.shape[1]
    tn = 768
    cp = jnp.pad(c, ((0, 8 - B), (0, 0)))
    out = pl.pallas_call(
        _mod_kernel,
        grid=(N // tn,),
        in_specs=[pl.BlockSpec((8, D), lambda j: (0, 0)),
                  pl.BlockSpec((D, tn), lambda j: (0, j)),
                  pl.BlockSpec((1, tn), lambda j: (0, j))],
        out_specs=pl.BlockSpec((8, tn), lambda j: (0, j)),
        out_shape=jax.ShapeDtypeStruct((8, N), F32),
        compiler_params=_params(("arbitrary",)),
        name="mod",
    )(cp, w_cond, b_cond.reshape(1, N))
    return out[:B]


def _inproj_kernel(x_ref, scale_ref, shift_ref, w_ref, wlr_ref, p_ref, lr_ref, *rest, tm, tn):
    res_refs, (h_scr, acc_scr) = rest[:-2], rest[-2:]
    j = pl.program_id(1)

    @pl.when(j == 0)
    def _():
        x = x_ref[...]
        ms = jnp.mean(x * x, axis=-1, keepdims=True)
        h = x * lax.rsqrt(ms + EPS) * (1.0 + scale_ref[...]) + shift_ref[...]
        hb = h.astype(BF16)
        h_scr[...] = hb
        lr_ref[...] = _dot(hb, wlr_ref[...])

    acc = _dot(h_scr[...], w_ref[...])
    p_ref[...] = acc.astype(BF16)

    @pl.when(j < ATT_QKV_TILES)
    def _():
        for c in range(tn // 128):
            acc_scr[c] = acc[:, c * 128:(c + 1) * 128]
        for ref, d in zip(res_refs, RESIDUE_DILATIONS):
            for r in range(d):
                for c in range(tn // 128):
                    rows = acc_scr[c, pl.ds(r, tm // d, stride=d), :]
                    ref[:, r * tn + c * 128:r * tn + (c + 1) * 128] = rows.astype(BF16)


def _inproj(x2, scale, shift, w_main, w_lr, seq):
    M, D = x2.shape
    tm, tn = 512, ATT_WIDTH
    bpt = seq // tm
    last_qkv = ATT_QKV_TILES - 1
    res_specs = [pl.BlockSpec((tm // d, d * tn), lambda i, j: (i, jnp.minimum(j, last_qkv)))
                 for d in RESIDUE_DILATIONS]
    res_shapes = [jax.ShapeDtypeStruct((M // d, ATT_QKV_TILES * d * tn), BF16)
                  for d in RESIDUE_DILATIONS]
    return pl.pallas_call(
        functools.partial(_inproj_kernel, tm=tm, tn=tn),
        grid=(M // tm, PROJ_WIDTH // tn),
        in_specs=[pl.BlockSpec((tm, D), lambda i, j: (i, 0)),
                  pl.BlockSpec((None, 1, D), lambda i, j: (i // bpt, 0, 0)),
                  pl.BlockSpec((None, 1, D), lambda i, j: (i // bpt, 0, 0)),
                  pl.BlockSpec((D, tn), lambda i, j: (0, j)),
                  pl.BlockSpec((D, LR_PAD), lambda i, j: (0, 0))],
        out_specs=[pl.BlockSpec((tm, tn), lambda i, j: (i, j)),
                   pl.BlockSpec((tm, LR_PAD), lambda i, j: (i, 0))] + res_specs,
        out_shape=[jax.ShapeDtypeStruct((M, PROJ_WIDTH), BF16),
                   jax.ShapeDtypeStruct((M, LR_PAD), F32)] + res_shapes,
        scratch_shapes=[pltpu.VMEM((tm, D), BF16), pltpu.VMEM((tn // 128, tm, 128), F32)],
        compiler_params=_params(("arbitrary", "arbitrary")),
        name="inproj",
    )(x2, scale, shift, w_main, w_lr)


def _t5_bucket_np(rel):
    nb = REL_BUCKETS // 2
    max_exact = nb // 2
    n = np.abs(rel)
    large = max_exact + (np.log(np.maximum(n, 1) / max_exact)
                         / np.log(REL_MAX_DIST / max_exact) * (nb - max_exact)).astype(np.int32)
    large = np.minimum(large, nb - 1)
    return (np.where(rel > 0, nb, 0) + np.where(n < max_exact, n, large)).astype(np.int32)


def _bias_kernel(rbt_ref, bucket_ref, mask_ref, o_ref):
    rbt = rbt_ref[...]
    bucket = bucket_ref[...]
    tbl = jnp.zeros((ATT_HEADS, bucket.shape[1]), F32)
    for b in range(REL_BUCKETS):
        tbl = jnp.where(bucket == b, rbt[:, b:b + 1], tbl)
    for v in range(3):
        o_ref[v] = jnp.where(mask_ref[v] > 0.5, tbl, NEG_INF)


def _bias_tiles(rel_bias, dilation):
    w, tq, tk = ATT_STEPS, ATT_TQ, ATT_TK
    qi = np.arange(tq)[:, None]
    kj = np.arange(tk)[None, :]
    step = kj - w - qi
    band = np.abs(step) <= w
    bucket = _t5_bucket_np(step * dilation).reshape(1, tq * tk)
    masks = np.stack([band & (kj >= w), band, band & (kj < tk - w)]).astype(np.float32)
    masks = masks.reshape(3, 1, tq * tk)
    out = pl.pallas_call(
        _bias_kernel,
        out_shape=jax.ShapeDtypeStruct((3, ATT_HEADS, tq * tk), F32),
        compiler_params=pltpu.CompilerParams(vmem_limit_bytes=VMEM_LIMIT),
        name=f"bias_d{dilation}",
    )(rel_bias.T, jnp.asarray(bucket), jnp.asarray(masks))
    return out.reshape(3, ATT_HEADS, tq, tk)


def _attn_kernel(q_ref, kp_ref, km_ref, kn_ref, vp_ref, vm_ref, vn_ref, bias_ref,
                 o_ref, lse_ref, k_lo, k_hi, v_lo, v_hi, s_scr, p_scr, *, tb):
    w, tq, tk = ATT_STEPS, ATT_TQ, ATT_TK
    nsub = tb // tq
    npair = ATT_HEADS // 2
    i = pl.program_id(2)
    first = i == 0
    last = i == pl.num_programs(2) - 1

    lane = lax.broadcasted_iota(jnp.int32, (1, ATT_WIDTH), 1)
    lower = (lane % 128) < ATT_HEAD_DIM
    for lo, hi, parts in ((k_lo, k_hi, (kp_ref, km_ref, kn_ref)), (v_lo, v_hi, (vp_ref, vm_ref, vn_ref))):
        row = 0
        for part in parts:
            x = part[...]
            n = x.shape[0]
            lo[row:row + n] = jnp.where(lower, x, jnp.zeros_like(x))
            hi[row:row + n] = jnp.where(lower, jnp.zeros_like(x), x)
            row += n

    head_lane = lax.broadcasted_iota(jnp.int32, (tq, 128), 1)

    def sub(j, carry):
        qs = pl.multiple_of(j * tq, tq)
        var = jnp.where(jnp.logical_and(first, j == 0), 0,
                        jnp.where(jnp.logical_and(last, j == nsub - 1), 2, 1))
        for hp in range(npair):
            cp = slice(hp * 128, (hp + 1) * 128)
            q = q_ref[pl.ds(qs, tq), cp]
            s_scr[2 * hp] = _dot_nt(q, k_lo[pl.ds(qs, tk), cp])
            s_scr[2 * hp + 1] = _dot_nt(q, k_hi[pl.ds(qs, tk), cp])
        lse_tile = jnp.zeros((tq, 128), F32)
        for h in range(ATT_HEADS):
            s = s_scr[h] + bias_ref[var, h]
            m = jnp.max(s, axis=-1, keepdims=True)
            p = jnp.exp(s - m)
            den = jnp.sum(p, axis=-1, keepdims=True)
            p_scr[h] = (p * (1.0 / den)).astype(BF16)
            lse_tile = jnp.where(head_lane == h, m + jnp.log(den), lse_tile)
        lse_ref[pl.ds(qs, tq), :] = lse_tile
        for hp in range(npair):
            cp = slice(hp * 128, (hp + 1) * 128)
            o = (_dot(p_scr[2 * hp], v_lo[pl.ds(qs, tk), cp])
                 + _dot(p_scr[2 * hp + 1], v_hi[pl.ds(qs, tk), cp]))
            o_ref[pl.ds(qs, tq), cp] = o.astype(BF16)
        return carry

    lax.fori_loop(0, nsub, sub, 0)


def _attn_pattern(qkv, bias, batch, seq, dilation):
    w = ATT_STEPS
    L = seq // dilation
    tb = min(512, L)
    nblk = L // tb
    hb = tb // w
    nhalo = L // w

    def main(j):
        return pl.BlockSpec((None, tb, ATT_WIDTH), lambda b, r, i: (b, i, j * dilation + r))

    def prev(j):
        return pl.BlockSpec((None, w, ATT_WIDTH),
                            lambda b, r, i: (b, jnp.maximum(i * hb - 1, 0), j * dilation + r))

    def nxt(j):
        return pl.BlockSpec((None, w, ATT_WIDTH),
                            lambda b, r, i: (b, jnp.minimum((i + 1) * hb, nhalo - 1), j * dilation + r))

    halo_buf = pltpu.VMEM((tb + 2 * w, ATT_WIDTH), BF16)
    return pl.pallas_call(
        functools.partial(_attn_kernel, tb=tb),
        grid=(batch, dilation, nblk),
        in_specs=[main(0), prev(1), main(1), nxt(1), prev(2), main(2), nxt(2),
                  pl.BlockSpec((3, ATT_HEADS, ATT_TQ, ATT_TK), lambda b, r, i: (0, 0, 0, 0))],
        out_specs=[pl.BlockSpec((None, tb, ATT_WIDTH), lambda b, r, i: (b, i, r)),
                   pl.BlockSpec((None, tb, 128), lambda b, r, i: (b, i, r))],
        out_shape=[jax.ShapeDtypeStruct((batch, L, dilation * ATT_WIDTH), BF16),
                   jax.ShapeDtypeStruct((batch, L, dilation * 128), F32)],
        scratch_shapes=[halo_buf, halo_buf, halo_buf, halo_buf,
                        pltpu.VMEM((ATT_HEADS, ATT_TQ, ATT_TK), F32),
                        pltpu.VMEM((ATT_HEADS, ATT_TQ, ATT_TK), BF16)],
        compiler_params=_params(("arbitrary", "arbitrary", "arbitrary")),
        name=f"attn_d{dilation}",
    )(qkv, qkv, qkv, qkv, qkv, qkv, qkv, bias)


def _merge_kernel(*refs, tm):
    nres = len(RESIDUE_DILATIONS)
    o1_ref, l1_ref = refs[0], refs[1]
    res_o, res_l = refs[2:2 + nres], refs[2 + nres:2 + 2 * nres]
    ag_ref, e_ref, out_ref = refs[2 + 2 * nres:5 + 2 * nres]
    scr_o, scr_l = refs[5 + 2 * nres:5 + 3 * nres], refs[5 + 3 * nres:]

    ncol = ATT_WIDTH // 128
    for d, o_ref, l_ref, so, sl in zip(RESIDUE_DILATIONS, res_o, res_l, scr_o, scr_l):
        for r in range(d):
            for c in range(ncol):
                col = r * ATT_WIDTH + c * 128
                so[c, pl.ds(r, tm // d, stride=d), :] = o_ref[:, col:col + 128].astype(F32)
            sl[pl.ds(r, tm // d, stride=d), :] = l_ref[:, r * 128:(r + 1) * 128]

    lses = [l1_ref[...]] + [sl[...] for sl in scr_l]
    mx = functools.reduce(jnp.maximum, lses)
    es = [jnp.exp(l - mx) for l in lses]
    inv = 1.0 / functools.reduce(jnp.add, es)
    e = e_ref[...]

    def expand(wgt):
        hi, lo = _split_bf16(wgt)
        return _dot(hi, e) + _dot(lo, e)

    outs = [o1_ref[...].astype(F32)] + [jnp.concatenate([so[c] for c in range(ncol)], axis=1)
                                        for so in scr_o]
    att = functools.reduce(jnp.add, [expand(ei * inv) * o for ei, o in zip(es, outs)])
    out_ref[...] = (att * _silu(ag_ref[...].astype(F32))).astype(BF16)


def _merge(o1, l1, res_outs, res_lses, proj):
    M = proj.shape[0]
    tm = 512
    expand = np.zeros((128, ATT_WIDTH), np.float32)
    for h in range(ATT_HEADS):
        expand[h, h * ATT_HEAD_DIM:(h + 1) * ATT_HEAD_DIM] = 1.0
    row = lambda rows, width: pl.BlockSpec((rows, width), lambda i: (i, 0))
    in_specs = ([row(tm, ATT_WIDTH), row(tm, 128)]
                + [row(tm // d, d * ATT_WIDTH) for d in RESIDUE_DILATIONS]
                + [row(tm // d, d * 128) for d in RESIDUE_DILATIONS]
                + [pl.BlockSpec((tm, ATT_WIDTH), lambda i: (i, COL_AG // ATT_WIDTH)),
                   pl.BlockSpec((128, ATT_WIDTH), lambda i: (0, 0))])
    return pl.pallas_call(
        functools.partial(_merge_kernel, tm=tm),
        grid=(M // tm,),
        in_specs=in_specs,
        out_specs=row(tm, ATT_WIDTH),
        out_shape=jax.ShapeDtypeStruct((M, ATT_WIDTH), BF16),
        scratch_shapes=([pltpu.VMEM((ATT_WIDTH // 128, tm, 128), F32) for _ in RESIDUE_DILATIONS]
                        + [pltpu.VMEM((tm, 128), F32) for _ in RESIDUE_DILATIONS]),
        compiler_params=_params(("arbitrary",)),
        name="merge",
    )(o1, l1, *res_outs, *res_lses, proj, jnp.asarray(expand, BF16))


def _gla_kernel(*refs, ts, reverse):
    if reverse:
        (q_ref, k_ref, v_ref, lr_ref, up_ref, gb_ref, tri_ref, of_ref, gg_ref, gain_ref,
         out_ref, state) = refs
    else:
        q_ref, k_ref, v_ref, lr_ref, up_ref, gb_ref, tri_ref, out_ref, state = refs
    C = GLA_CHUNK
    nchunk = ts // C

    @pl.when(pl.program_id(2) == 0)
    def _():
        state[...] = jnp.zeros_like(state)

    lr_hi, lr_lo = _split_bf16(lr_ref[...])
    up_hi, up_lo = _split_bf16(up_ref[...])
    z = _dot(lr_hi, up_hi) + _dot(lr_hi, up_lo) + _dot(lr_lo, up_hi) + gb_ref[...]
    log_g = (jnp.minimum(z, 0.0) - jnp.log(1.0 + jnp.exp(-jnp.abs(z)))) * (1.0 / GLA_GATE_NORM)

    tri = tri_ref[...]
    ri = lax.broadcasted_iota(jnp.int32, (C, C), 0)
    ci = lax.broadcasted_iota(jnp.int32, (C, C), 1)
    keep = (ci >= ri) if reverse else (ci <= ri)

    order = range(nchunk - 1, -1, -1) if reverse else range(nchunk)
    for c in order:
        rows = slice(c * C, (c + 1) * C)
        g_hi, g_lo = _split_bf16(log_g[rows])
        b = _dot(tri, g_hi) + _dot(tri, g_lo)
        b_edge = b[0:1] if reverse else b[C - 1:C]
        q = q_ref[rows, :].astype(F32)
        k = k_ref[rows, :].astype(F32)
        v = v_ref[rows, :]
        qf = (q * jnp.exp(b) * (GLA_DK ** -0.5)).astype(BF16)
        kd = (k * jnp.exp(-b)).astype(BF16)
        ks = (k * jnp.exp(b_edge - b)).astype(BF16)
        att = jnp.where(keep, _dot_nt(qf, kd), 0.0)
        st = state[...]
        o = _dot(att.astype(BF16), v) + _dot_nt(qf, st.astype(BF16))
        state[...] = st * jnp.exp(b_edge) + _dot_tn(v, ks)
        if reverse:
            tot = o + of_ref[rows, :]
            ms = jnp.mean(tot * tot, axis=-1, keepdims=True)
            g_o = tot * lax.rsqrt(ms + EPS) * gain_ref[...]
            out_ref[rows, :] = (g_o * _silu(gg_ref[rows, :].astype(F32))).astype(BF16)
        else:
            out_ref[rows, :] = o


def _gla_direction(proj, lr, up_pad, gate_bias, batch, seq, reverse, o_fwd=None, gain=None):
    ts = 512
    nstep = seq // ts
    C = GLA_CHUNK
    p3 = proj.reshape(batch, seq, PROJ_WIDTH)
    lr3 = lr.reshape(batch, seq, LR_PAD)
    tri = np.triu(np.ones((C, C), np.float32)) if reverse else np.tril(np.ones((C, C), np.float32))

    def step(i):
        return nstep - 1 - i if reverse else i

    def seq_block(width, col0):
        return pl.BlockSpec((None, ts, width), lambda b, h, i: (b, step(i), col0 // width + h))

    in_specs = [seq_block(GLA_DK, COL_GQ), seq_block(GLA_DK, COL_GK), seq_block(GLA_DV, COL_GV),
                pl.BlockSpec((None, ts, LR_PAD), lambda b, h, i: (b, step(i), 0)),
                pl.BlockSpec((LR_PAD, GLA_DK), lambda b, h, i: (0, h)),
                pl.BlockSpec((1, GLA_DK), lambda b, h, i: (0, h)),
                pl.BlockSpec((C, C), lambda b, h, i: (0, 0))]
    args = [p3, p3, p3, lr3, up_pad, gate_bias.reshape(1, GLA_KEY_WIDTH), jnp.asarray(tri, BF16)]
    if reverse:
        in_specs += [pl.BlockSpec((None, ts, GLA_DV), lambda b, h, i: (b, step(i), h)),
                     seq_block(GLA_DV, COL_GG),
                     pl.BlockSpec((1, GLA_DV), lambda b, h, i: (0, h))]
        args += [o_fwd, p3, gain.reshape(1, GLA_WIDTH)]
        out_dtype = BF16
    else:
        out_dtype = F32
    return pl.pallas_call(
        functools.partial(_gla_kernel, ts=ts, reverse=reverse),
        grid=(batch, GLA_HEADS, nstep),
        in_specs=in_specs,
        out_specs=pl.BlockSpec((None, ts, GLA_DV), lambda b, h, i: (b, step(i), h)),
        out_shape=jax.ShapeDtypeStruct((batch, seq, GLA_WIDTH), out_dtype),
        scratch_shapes=[pltpu.VMEM((GLA_DV, GLA_DK), F32)],
        compiler_params=_params(("arbitrary", "arbitrary", "arbitrary")),
        name="gla_bwd" if reverse else "gla_fwd",
    )(*args)


def _outproj_kernel(a_ref, g_ref, wa_ref, wg_ref, x_ref, gate_ref, fg_ref, o_ref, *, final):
    y = _dot(a_ref[...], wa_ref[...]) + _dot(g_ref[...], wg_ref[...])
    xn = x_ref[...] + gate_ref[...] * y
    if final:
        ms = jnp.mean(xn * xn, axis=-1, keepdims=True)
        xn = xn * lax.rsqrt(ms + EPS) * fg_ref[...]
    o_ref[...] = xn


def _outproj(a_out, g_out, w_out_bf16, x2, gate, final_gain, seq, final):
    M, D = x2.shape
    tm = 256
    bpt = seq // tm
    return pl.pallas_call(
        functools.partial(_outproj_kernel, final=final),
        grid=(M // tm,),
        in_specs=[pl.BlockSpec((tm, ATT_WIDTH), lambda i: (i, 0)),
                  pl.BlockSpec((tm, GLA_WIDTH), lambda i: (i, 0)),
                  pl.BlockSpec((ATT_WIDTH, D), lambda i: (0, 0)),
                  pl.BlockSpec((GLA_WIDTH, D), lambda i: (1, 0)),
                  pl.BlockSpec((tm, D), lambda i: (i, 0)),
                  pl.BlockSpec((None, 1, D), lambda i: (i // bpt, 0, 0)),
                  pl.BlockSpec((1, D), lambda i: (0, 0))],
        out_specs=pl.BlockSpec((tm, D), lambda i: (i, 0)),
        out_shape=jax.ShapeDtypeStruct((M, D), F32),
        compiler_params=_params(("arbitrary",)),
        name="outproj",
    )(a_out, g_out, w_out_bf16, w_out_bf16, x2, gate, final_gain.reshape(1, D))


def kernel(x, c, w_cond, b_cond, w_in, gla_gate_up_fwd, gla_gate_bias_fwd, gla_gate_up_bwd,
           gla_gate_bias_bwd, gla_norm_gain, rel_bias, w_out, final_gain):
    B, S, D = x.shape
    depth = w_cond.shape[0]
    R = GLA_GATE_RANK
    xs = x.reshape(B * S, D)
    for layer in range(depth):
        mod = _mod(c, w_cond[layer], b_cond[layer])
        shift, scale, gate = [m.reshape(B, 1, D) for m in jnp.split(mod, 3, axis=-1)]

        col_scale = jnp.where(jnp.arange(PROJ_WIDTH) < ATT_WIDTH, ATT_HEAD_DIM ** -0.5, 1.0)
        w_main = (w_in[layer, :, :PROJ_WIDTH] * col_scale).astype(BF16)
        w_lr = jnp.pad(w_in[layer, :, PROJ_WIDTH:], ((0, 0), (0, LR_PAD - 2 * R))).astype(BF16)
        proj, lr, *res_qkv = _inproj(xs, scale, shift, w_main, w_lr, S)

        o1, l1 = _attn_pattern(proj.reshape(B, S, PROJ_WIDTH), _bias_tiles(rel_bias, 1), B, S, 1)
        res_outs, res_lses = [], []
        for d, qkv in zip(RESIDUE_DILATIONS, res_qkv):
            o_p, lse_p = _attn_pattern(qkv.reshape(B, S // d, -1), _bias_tiles(rel_bias, d), B, S, d)
            res_outs.append(o_p.reshape(B * S // d, d * ATT_WIDTH))
            res_lses.append(lse_p.reshape(B * S // d, d * 128))
        a_out = _merge(o1.reshape(B * S, ATT_WIDTH), l1.reshape(B * S, 128), res_outs, res_lses, proj)

        up_f = jnp.pad(gla_gate_up_fwd[layer], ((0, LR_PAD - R), (0, 0)))
        up_b = jnp.pad(gla_gate_up_bwd[layer], ((R, LR_PAD - 2 * R), (0, 0)))
        o_fwd = _gla_direction(proj, lr, up_f, gla_gate_bias_fwd[layer], B, S, reverse=False)
        g_out = _gla_direction(proj, lr, up_b, gla_gate_bias_bwd[layer], B, S, reverse=True,
                               o_fwd=o_fwd, gain=gla_norm_gain[layer])

        xs = _outproj(a_out, g_out.reshape(B * S, GLA_WIDTH), w_out[layer].astype(BF16),
                      xs, gate, final_gain, S, final=layer == depth - 1)
    return xs.reshape(B, S, D)
```

```python
import functools
import math

import jax
import jax.numpy as jnp
import numpy as np
from jax import lax
from jax.experimental import pallas as pl
from jax.experimental.pallas import tpu as pltpu

D_MODEL = 2048
ATT_WIDTH = 1024
ATT_HEADS = 16
ATT_HEAD_DIM = 64
DILATED_PATTERNS = ((128, 1), (512, 4), (2048, 16))
ATT_STEPS = 64
GLA_WIDTH = 1024
GLA_HEADS = 4
GLA_KEY_WIDTH = 512
GLA_DK = 128
GLA_DV = 256
GLA_GATE_RANK = 16
GLA_GATE_NORM = 16.0
GLA_CHUNK = 64
REL_BUCKETS = 32
REL_MAX_DIST = 1024
EPS = 1e-6
NEG_INF = -1e30

PROJ_WIDTH = 4 * ATT_WIDTH + 2 * GLA_KEY_WIDTH + 2 * GLA_WIDTH
COL_AQ, COL_AK, COL_AV, COL_AG = 0, 1024, 2048, 3072
COL_GQ, COL_GK, COL_GV, COL_GG = 4096, 4608, 5120, 6144
LR_PAD = 128
ATT_QKV_TILES = 3
RESIDUE_DILATIONS = tuple(d for _, d in DILATED_PATTERNS if d > 1)
ATT_TQ = 128
ATT_TK = ATT_TQ + 2 * ATT_STEPS

VMEM_LIMIT = 56 * 1024 * 1024

BF16 = jnp.bfloat16
F32 = jnp.float32


def _params(sem):
    return pltpu.CompilerParams(dimension_semantics=sem, vmem_limit_bytes=VMEM_LIMIT)


def _dot(a, b):
    return jnp.dot(a, b, preferred_element_type=F32)


def _dot_nt(a, b):
    return lax.dot_general(a, b, (((1,), (1,)), ((), ())), preferred_element_type=F32)


def _dot_tn(a, b):
    return lax.dot_general(a, b, (((0,), (0,)), ((), ())), preferred_element_type=F32)


def _split_bf16(x):
    hi = x.astype(BF16)
    lo = (x - hi.astype(F32)).astype(BF16)
    return hi, lo


def _silu(x):
    return x / (1.0 + jnp.exp(-x))


def _mod_kernel(c_ref, w_ref, b_ref, o_ref):
    s = _silu(c_ref[...])
    o_ref[...] = jnp.dot(s, w_ref[...], preferred_element_type=F32,
                         precision=lax.Precision.HIGHEST) + b_ref[...]


def _mod(c, w_cond, b_cond):
    B, D = c.shape
    N = w_cond.shape[1]
    tn = 768
    cp = jnp.pad(c, ((0, 8 - B), (0, 0)))
    out = pl.pallas_call(
        _mod_kernel,
        grid=(N // tn,),
        in_specs=[pl.BlockSpec((8, D), lambda j: (0, 0)),
                  pl.BlockSpec((D, tn), lambda j: (0, j)),
                  pl.BlockSpec((1, tn), lambda j: (0, j))],
        out_specs=pl.BlockSpec((8, tn), lambda j: (0, j)),
        out_shape=jax.ShapeDtypeStruct((8, N), F32),
        compiler_params=_params(("arbitrary",)),
        name="mod",
    )(cp, w_cond, b_cond.reshape(1, N))
    return out[:B]


def _inproj_kernel(x_ref, scale_ref, shift_ref, w_ref, wlr_ref, p_ref, lr_ref, *rest, tm, tn):
    nres = len(RESIDUE_DILATIONS)
    res_refs, h_scr, acc_scr = rest[:nres], rest[nres], rest[nres + 1:]
    j = pl.program_id(1)

    @pl.when(j == 0)
    def _():
        x = x_ref[...]
        ms = jnp.mean(x * x, axis=-1, keepdims=True)
        h = x * lax.rsqrt(ms + EPS) * (1.0 + scale_ref[...]) + shift_ref[...]
        hb = h.astype(BF16)
        h_scr[...] = hb
        lr_ref[...] = _dot(hb, wlr_ref[...])

    @pl.when(j >= ATT_QKV_TILES)
    def _():
        p_ref[...] = _dot(h_scr[...], w_ref[...]).astype(BF16)

    @pl.when(j < ATT_QKV_TILES)
    def _():
        h = h_scr[...]
        chunk = 256
        for c0 in range(0, tn, chunk):
            acc = _dot(h, w_ref[:, c0:c0 + chunk])
            p_ref[:, c0:c0 + chunk] = acc.astype(BF16)
            for c in range(c0 // 128, (c0 + chunk) // 128):
                lanes = slice(c * 128 - c0, (c + 1) * 128 - c0)
                src, prev_d = acc_scr[0], 1
                src[c] = acc[:, lanes]
                for lvl, (ref, d) in enumerate(zip(res_refs, RESIDUE_DILATIONS)):
                    ratio, n = d // prev_d, tm // d
                    dst = acc_scr[lvl + 1] if lvl + 1 < len(RESIDUE_DILATIONS) else None
                    for rp in range(prev_d):
                        for a in range(ratio):
                            r = rp + prev_d * a
                            rows = src[c, pl.ds(rp * (tm // prev_d) + a, n, stride=ratio), :]
                            ref[:, r * tn + c * 128:r * tn + (c + 1) * 128] = rows.astype(BF16)
                            if dst is not None:
                                dst[c, r * n:(r + 1) * n, :] = rows
                    src, prev_d = dst, d


def _inproj(x2, scale, shift, w_main, w_lr, seq):
    M, D = x2.shape
    tm, tn = 512, ATT_WIDTH
    bpt = seq // tm
    last_qkv = ATT_QKV_TILES - 1
    res_specs = [pl.BlockSpec((tm // d, d * tn), lambda i, j: (i, jnp.minimum(j, last_qkv)))
                 for d in RESIDUE_DILATIONS]
    res_shapes = [jax.ShapeDtypeStruct((M // d, ATT_QKV_TILES * d * tn), BF16)
                  for d in RESIDUE_DILATIONS]
    return pl.pallas_call(
        functools.partial(_inproj_kernel, tm=tm, tn=tn),
        grid=(M // tm, PROJ_WIDTH // tn),
        in_specs=[pl.BlockSpec((tm, D), lambda i, j: (i, 0)),
                  pl.BlockSpec((None, 1, D), lambda i, j: (i // bpt, 0, 0)),
                  pl.BlockSpec((None, 1, D), lambda i, j: (i // bpt, 0, 0)),
                  pl.BlockSpec((D, tn), lambda i, j: (0, j)),
                  pl.BlockSpec((D, LR_PAD), lambda i, j: (0, 0))],
        out_specs=[pl.BlockSpec((tm, tn), lambda i, j: (i, j)),
                   pl.BlockSpec((tm, LR_PAD), lambda i, j: (i, 0))] + res_specs,
        out_shape=[jax.ShapeDtypeStruct((M, PROJ_WIDTH), BF16),
                   jax.ShapeDtypeStruct((M, LR_PAD), F32)] + res_shapes,
        scratch_shapes=[pltpu.VMEM((tm, D), BF16)]
                       + [pltpu.VMEM((tn // 128, tm, 128), F32) for _ in RESIDUE_DILATIONS],
        compiler_params=_params(("arbitrary", "arbitrary")),
        name="inproj",
    )(x2, scale, shift, w_main, w_lr)


def _t5_bucket_np(rel):
    nb = REL_BUCKETS // 2
    max_exact = nb // 2
    n = np.abs(rel)
    large = max_exact + (np.log(np.maximum(n, 1) / max_exact)
                         / np.log(REL_MAX_DIST / max_exact) * (nb - max_exact)).astype(np.int32)
    large = np.minimum(large, nb - 1)
    return (np.where(rel > 0, nb, 0) + np.where(n < max_exact, n, large)).astype(np.int32)


def _bias_kernel(rbt_ref, bucket_ref, mask_ref, o_ref):
    rbt = rbt_ref[...]
    bucket = bucket_ref[...]
    tbl = jnp.zeros((ATT_HEADS, bucket.shape[1]), F32)
    for b in range(REL_BUCKETS):
        tbl = jnp.where(bucket == b, rbt[:, b:b + 1], tbl)
    for v in range(3):
        o_ref[v] = jnp.where(mask_ref[v] > 0.5, tbl, NEG_INF)


def _bias_tiles(rel_bias, dilation):
    w, tq, tk = ATT_STEPS, ATT_TQ, ATT_TK
    qi = np.arange(tq)[:, None]
    kj = np.arange(tk)[None, :]
    step = kj - w - qi
    band = np.abs(step) <= w
    bucket = _t5_bucket_np(step * dilation).reshape(1, tq * tk)
    masks = np.stack([band & (kj >= w), band, band & (kj < tk - w)]).astype(np.float32)
    masks = masks.reshape(3, 1, tq * tk)
    out = pl.pallas_call(
        _bias_kernel,
        out_shape=jax.ShapeDtypeStruct((3, ATT_HEADS, tq * tk), F32),
        compiler_params=pltpu.CompilerParams(vmem_limit_bytes=VMEM_LIMIT),
        name=f"bias_d{dilation}",
    )(rel_bias.T, jnp.asarray(bucket), jnp.asarray(masks))
    return out.reshape(3, ATT_HEADS, tq, tk)


def _attn_kernel(q_ref, kp_ref, km_ref, kn_ref, vp_ref, vm_ref, vn_ref, bias_ref,
                 o_ref, lse_ref, k_lo, k_hi, v_lo, v_hi, s_scr, p_scr, *, tb):
    w, tq, tk = ATT_STEPS, ATT_TQ, ATT_TK
    nsub = tb // tq
    npair = ATT_HEADS // 2
    i = pl.program_id(2)
    first = i == 0
    last = i == pl.num_programs(2) - 1

    lane = lax.broadcasted_iota(jnp.int32, (1, ATT_WIDTH), 1)
    lower = (lane % 128) < ATT_HEAD_DIM
    for lo, hi, parts in ((k_lo, k_hi, (kp_ref, km_ref, kn_ref)), (v_lo, v_hi, (vp_ref, vm_ref, vn_ref))):
        row = 0
        for part in parts:
            x = part[...]
            n = x.shape[0]
            lo[row:row + n] = jnp.where(lower, x, jnp.zeros_like(x))
            hi[row:row + n] = jnp.where(lower, jnp.zeros_like(x), x)
            row += n

    head_lane = lax.broadcasted_iota(jnp.int32, (tq, 128), 1)

    def sub(j, carry):
        qs = pl.multiple_of(j * tq, tq)
        var = jnp.where(jnp.logical_and(first, j == 0), 0,
                        jnp.where(jnp.logical_and(last, j == nsub - 1), 2, 1))
        for hp in range(npair):
            cp = slice(hp * 128, (hp + 1) * 128)
            q = q_ref[pl.ds(qs, tq), cp]
            s_scr[2 * hp] = _dot_nt(q, k_lo[pl.ds(qs, tk), cp])
            s_scr[2 * hp + 1] = _dot_nt(q, k_hi[pl.ds(qs, tk), cp])
        lse_tile = jnp.zeros((tq, 128), F32)
        for h in range(ATT_HEADS):
            s = s_scr[h] + bias_ref[var, h]
            m = jnp.max(s, axis=-1, keepdims=True)
            p = jnp.exp(s - m)
            den = jnp.sum(p, axis=-1, keepdims=True)
            p_scr[h] = (p * (1.0 / den)).astype(BF16)
            lse_tile = jnp.where(head_lane == h, m + jnp.log(den), lse_tile)
        lse_ref[pl.ds(qs, tq), :] = lse_tile
        for hp in range(npair):
            cp = slice(hp * 128, (hp + 1) * 128)
            o = (_dot(p_scr[2 * hp], v_lo[pl.ds(qs, tk), cp])
                 + _dot(p_scr[2 * hp + 1], v_hi[pl.ds(qs, tk), cp]))
            o_ref[pl.ds(qs, tq), cp] = o.astype(BF16)
        return carry

    lax.fori_loop(0, nsub, sub, 0)


def _attn_pattern(qkv, bias, batch, seq, dilation):
    w = ATT_STEPS
    L = seq // dilation
    tb = min(512, L)
    nblk = L // tb
    hb = tb // w
    nhalo = L // w

    def main(j):
        return pl.BlockSpec((None, tb, ATT_WIDTH), lambda b, r, i: (b, i, j * dilation + r))

    def prev(j):
        return pl.BlockSpec((None, w, ATT_WIDTH),
                            lambda b, r, i: (b, jnp.maximum(i * hb - 1, 0), j * dilation + r))

    def nxt(j):
        return pl.BlockSpec((None, w, ATT_WIDTH),
                            lambda b, r, i: (b, jnp.minimum((i + 1) * hb, nhalo - 1), j * dilation + r))

    halo_buf = pltpu.VMEM((tb + 2 * w, ATT_WIDTH), BF16)
    return pl.pallas_call(
        functools.partial(_attn_kernel, tb=tb),
        grid=(batch, dilation, nblk),
        in_specs=[main(0), prev(1), main(1), nxt(1), prev(2), main(2), nxt(2),
                  pl.BlockSpec((3, ATT_HEADS, ATT_TQ, ATT_TK), lambda b, r, i: (0, 0, 0, 0))],
        out_specs=[pl.BlockSpec((None, tb, ATT_WIDTH), lambda b, r, i: (b, i, r)),
                   pl.BlockSpec((None, tb, 128), lambda b, r, i: (b, i, r))],
        out_shape=[jax.ShapeDtypeStruct((batch, L, dilation * ATT_WIDTH), BF16),
                   jax.ShapeDtypeStruct((batch, L, dilation * 128), F32)],
        scratch_shapes=[halo_buf, halo_buf, halo_buf, halo_buf,
                        pltpu.VMEM((ATT_HEADS, ATT_TQ, ATT_TK), F32),
                        pltpu.VMEM((ATT_HEADS, ATT_TQ, ATT_TK), BF16)],
        compiler_params=_params(("arbitrary", "arbitrary", "arbitrary")),
        name=f"attn_d{dilation}",
    )(qkv, qkv, qkv, qkv, qkv, qkv, qkv, bias)


def _merge_kernel(*refs, tm):
    nres = len(RESIDUE_DILATIONS)
    o1_ref, l1_ref = refs[0], refs[1]
    res_o, res_l = refs[2:2 + nres], refs[2 + nres:2 + 2 * nres]
    ag_ref, e_ref, out_ref = refs[2 + 2 * nres:5 + 2 * nres]
    scr_o, scr_l = refs[5 + 2 * nres:5 + 3 * nres], refs[5 + 3 * nres:]

    ncol = ATT_WIDTH // 128
    for d, o_ref, l_ref, so, sl in zip(RESIDUE_DILATIONS, res_o, res_l, scr_o, scr_l):
        for r in range(d):
            for c in range(ncol):
                col = r * ATT_WIDTH + c * 128
                so[c, pl.ds(r, tm // d, stride=d), :] = o_ref[:, col:col + 128].astype(F32)
            sl[pl.ds(r, tm // d, stride=d), :] = l_ref[:, r * 128:(r + 1) * 128]

    lses = [l1_ref[...]] + [sl[...] for sl in scr_l]
    mx = functools.reduce(jnp.maximum, lses)
    es = [jnp.exp(l - mx) for l in lses]
    inv = 1.0 / functools.reduce(jnp.add, es)
    e = e_ref[...]

    def expand(wgt):
        hi, lo = _split_bf16(wgt)
        return _dot(hi, e) + _dot(lo, e)

    outs = [o1_ref[...].astype(F32)] + [jnp.concatenate([so[c] for c in range(ncol)], axis=1)
                                        for so in scr_o]
    att = functools.reduce(jnp.add, [expand(ei * inv) * o for ei, o in zip(es, outs)])
    out_ref[...] = (att * _silu(ag_ref[...].astype(F32))).astype(BF16)


def _merge(o1, l1, res_outs, res_lses, proj):
    M = proj.shape[0]
    tm = 512
    expand = np.zeros((128, ATT_WIDTH), np.float32)
    for h in range(ATT_HEADS):
        expand[h, h * ATT_HEAD_DIM:(h + 1) * ATT_HEAD_DIM] = 1.0
    row = lambda rows, width: pl.BlockSpec((rows, width), lambda i: (i, 0))
    in_specs = ([row(tm, ATT_WIDTH), row(tm, 128)]
                + [row(tm // d, d * ATT_WIDTH) for d in RESIDUE_DILATIONS]
                + [row(tm // d, d * 128) for d in RESIDUE_DILATIONS]
                + [pl.BlockSpec((tm, ATT_WIDTH), lambda i: (i, COL_AG // ATT_WIDTH)),
                   pl.BlockSpec((128, ATT_WIDTH), lambda i: (0, 0))])
    return pl.pallas_call(
        functools.partial(_merge_kernel, tm=tm),
        grid=(M // tm,),
        in_specs=in_specs,
        out_specs=row(tm, ATT_WIDTH),
        out_shape=jax.ShapeDtypeStruct((M, ATT_WIDTH), BF16),
        scratch_shapes=([pltpu.VMEM((ATT_WIDTH // 128, tm, 128), F32) for _ in RESIDUE_DILATIONS]
                        + [pltpu.VMEM((tm, 128), F32) for _ in RESIDUE_DILATIONS]),
        compiler_params=_params(("arbitrary",)),
        name="merge",
    )(o1, l1, *res_outs, *res_lses, proj, jnp.asarray(expand, BF16))


def _gla_kernel(*refs, ts, reverse):
    if reverse:
        (q_ref, k_ref, v_ref, lr_ref, up_ref, gb_ref, of_ref, gg_ref, gain_ref,
         out_ref, state, qf_scr, kd_scr, ks_scr, oin_scr, st_scr) = refs
    else:
        (q_ref, k_ref, v_ref, lr_ref, up_ref, gb_ref,
         out_ref, state, qf_scr, kd_scr, ks_scr, oin_scr, st_scr) = refs
    C = GLA_CHUNK
    nchunk = ts // C
    chunks = [slice(c * C, (c + 1) * C) for c in range(nchunk)]

    @pl.when(pl.program_id(2) == 0)
    def _():
        state[...] = jnp.zeros_like(state)

    lr_hi, lr_lo = _split_bf16(lr_ref[...])
    up_hi, up_lo = _split_bf16(up_ref[...])
    z = _dot(lr_hi, up_hi) + _dot(lr_hi, up_lo) + _dot(lr_lo, up_hi) + gb_ref[...]
    log_g = (jnp.minimum(z, 0.0) - jnp.log(1.0 + jnp.exp(-jnp.abs(z)))) * (1.0 / GLA_GATE_NORM)


    row = lax.broadcasted_iota(jnp.int32, (C, GLA_DK), 0)
    dec_cols = []
    for rows in chunks:
        b = log_g[rows]
        s = 1
        while s < C:
            if reverse:
                b = b + jnp.where(row < C - s, pltpu.roll(b, C - s, 0), 0.0)
            else:
                b = b + jnp.where(row >= s, pltpu.roll(b, s, 0), 0.0)
            s *= 2
        b_edge = b[0:1] if reverse else b[C - 1:C]
        q = q_ref[rows, :].astype(F32)
        k = k_ref[rows, :].astype(F32)
        qf_scr[rows, :] = (q * jnp.exp(b) * (GLA_DK ** -0.5)).astype(BF16)
        kd_scr[rows, :] = (k * jnp.exp(-b)).astype(BF16)
        ks_scr[rows, :] = (k * jnp.exp(b_edge - b)).astype(BF16)
        dec = jnp.broadcast_to(jnp.exp(b_edge), (GLA_DK, GLA_DK)).T
        dec_cols.append(jnp.concatenate([dec, dec], axis=1))

    ri = lax.broadcasted_iota(jnp.int32, (C, C), 0)
    ci = lax.broadcasted_iota(jnp.int32, (C, C), 1)
    keep = (ci >= ri) if reverse else (ci <= ri)
    atts = [jnp.where(keep, _dot_nt(qf_scr[rows, :], kd_scr[rows, :]), 0.0).astype(BF16)
            for rows in chunks]

    kvs = []
    for rows, att in zip(chunks, atts):
        v = v_ref[rows, :]
        oin_scr[rows, :] = _dot(att, v)
        kvs.append(_dot_tn(ks_scr[rows, :], v))

    st = state[...]
    order = list(range(nchunk - 1, -1, -1) if reverse else range(nchunk))
    for c in order:
        st_scr[c] = st.astype(BF16)
        st = st * dec_cols[c] + kvs[c]
    state[...] = st

    for c in order:
        rows = chunks[c]
        o = oin_scr[rows, :] + _dot(qf_scr[rows, :], st_scr[c])
        if reverse:
            tot = o + of_ref[rows, :]
            ms = jnp.mean(tot * tot, axis=-1, keepdims=True)
            g_o = tot * lax.rsqrt(ms + EPS) * gain_ref[...]
            out_ref[rows, :] = (g_o * _silu(gg_ref[rows, :].astype(F32))).astype(BF16)
        else:
            out_ref[rows, :] = o


def _gla_direction(proj, lr, up_pad, gate_bias, batch, seq, reverse, o_fwd=None, gain=None):
    ts = 512
    nstep = seq // ts
    C = GLA_CHUNK
    p3 = proj.reshape(batch, seq, PROJ_WIDTH)
    lr3 = lr.reshape(batch, seq, LR_PAD)

    def step(i):
        return nstep - 1 - i if reverse else i

    def seq_block(width, col0):
        return pl.BlockSpec((None, ts, width), lambda b, h, i: (b, step(i), col0 // width + h))

    in_specs = [seq_block(GLA_DK, COL_GQ), seq_block(GLA_DK, COL_GK), seq_block(GLA_DV, COL_GV),
                pl.BlockSpec((None, ts, LR_PAD), lambda b, h, i: (b, step(i), 0)),
                pl.BlockSpec((LR_PAD, GLA_DK), lambda b, h, i: (0, h)),
                pl.BlockSpec((1, GLA_DK), lambda b, h, i: (0, h))]
    args = [p3, p3, p3, lr3, up_pad, gate_bias.reshape(1, GLA_KEY_WIDTH)]
    if reverse:
        in_specs += [pl.BlockSpec((None, ts, GLA_DV), lambda b, h, i: (b, step(i), h)),
                     seq_block(GLA_DV, COL_GG),
                     pl.BlockSpec((1, GLA_DV), lambda b, h, i: (0, h))]
        args += [o_fwd, p3, gain.reshape(1, GLA_WIDTH)]
        out_dtype = BF16
    else:
        out_dtype = F32
    return pl.pallas_call(
        functools.partial(_gla_kernel, ts=ts, reverse=reverse),
        grid=(batch, GLA_HEADS, nstep),
        in_specs=in_specs,
        out_specs=pl.BlockSpec((None, ts, GLA_DV), lambda b, h, i: (b, step(i), h)),
        out_shape=jax.ShapeDtypeStruct((batch, seq, GLA_WIDTH), out_dtype),
        scratch_shapes=[pltpu.VMEM((GLA_DK, GLA_DV), F32),
                        pltpu.VMEM((ts, GLA_DK), BF16),
                        pltpu.VMEM((ts, GLA_DK), BF16),
                        pltpu.VMEM((ts, GLA_DK), BF16),
                        pltpu.VMEM((ts, GLA_DV), F32),
                        pltpu.VMEM((ts // C, GLA_DK, GLA_DV), BF16)],
        compiler_params=_params(("arbitrary", "arbitrary", "arbitrary")),
        name="gla_bwd" if reverse else "gla_fwd",
    )(*args)


def _outproj_kernel(a_ref, g_ref, wa_ref, wg_ref, x_ref, gate_ref, fg_ref, o_ref, *, final):
    y = _dot(a_ref[...], wa_ref[...]) + _dot(g_ref[...], wg_ref[...])
    xn = x_ref[...] + gate_ref[...] * y
    if final:
        ms = jnp.mean(xn * xn, axis=-1, keepdims=True)
        xn = xn * lax.rsqrt(ms + EPS) * fg_ref[...]
    o_ref[...] = xn


def _outproj(a_out, g_out, w_out_bf16, x2, gate, final_gain, seq, final):
    M, D = x2.shape
    tm = 256
    bpt = seq // tm
    return pl.pallas_call(
        functools.partial(_outproj_kernel, final=final),
        grid=(M // tm,),
        in_specs=[pl.BlockSpec((tm, ATT_WIDTH), lambda i: (i, 0)),
                  pl.BlockSpec((tm, GLA_WIDTH), lambda i: (i, 0)),
                  pl.BlockSpec((ATT_WIDTH, D), lambda i: (0, 0)),
                  pl.BlockSpec((GLA_WIDTH, D), lambda i: (1, 0)),
                  pl.BlockSpec((tm, D), lambda i: (i, 0)),
                  pl.BlockSpec((None, 1, D), lambda i: (i // bpt, 0, 0)),
                  pl.BlockSpec((1, D), lambda i: (0, 0))],
        out_specs=pl.BlockSpec((tm, D), lambda i: (i, 0)),
        out_shape=jax.ShapeDtypeStruct((M, D), F32),
        compiler_params=_params(("arbitrary",)),
        name="outproj",
    )(a_out, g_out, w_out_bf16, w_out_bf16, x2, gate, final_gain.reshape(1, D))


def kernel(x, c, w_cond, b_cond, w_in, gla_gate_up_fwd, gla_gate_bias_fwd, gla_gate_up_bwd,
           gla_gate_bias_bwd, gla_norm_gain, rel_bias, w_out, final_gain):
    B, S, D = x.shape
    depth = w_cond.shape[0]
    R = GLA_GATE_RANK
    xs = x.reshape(B * S, D)
    for layer in range(depth):
        mod = _mod(c, w_cond[layer], b_cond[layer])
        shift, scale, gate = [m.reshape(B, 1, D) for m in jnp.split(mod, 3, axis=-1)]

        col_scale = jnp.where(jnp.arange(PROJ_WIDTH) < ATT_WIDTH, ATT_HEAD_DIM ** -0.5, 1.0)
        w_main = (w_in[layer, :, :PROJ_WIDTH] * col_scale).astype(BF16)
        w_lr = jnp.pad(w_in[layer, :, PROJ_WIDTH:], ((0, 0), (0, LR_PAD - 2 * R))).astype(BF16)
        proj, lr, *res_qkv = _inproj(xs, scale, shift, w_main, w_lr, S)

        o1, l1 = _attn_pattern(proj.reshape(B, S, PROJ_WIDTH), _bias_tiles(rel_bias, 1), B, S, 1)
        res_outs, res_lses = [], []
        for d, qkv in zip(RESIDUE_DILATIONS, res_qkv):
            o_p, lse_p = _attn_pattern(qkv.reshape(B, S // d, -1), _bias_tiles(rel_bias, d), B, S, d)
            res_outs.append(o_p.reshape(B * S // d, d * ATT_WIDTH))
            res_lses.append(lse_p.reshape(B * S // d, d * 128))
        a_out = _merge(o1.reshape(B * S, ATT_WIDTH), l1.reshape(B * S, 128), res_outs, res_lses, proj)

        up_f = jnp.pad(gla_gate_up_fwd[layer], ((0, LR_PAD - R), (0, 0)))
        up_b = jnp.pad(gla_gate_up_bwd[layer], ((R, LR_PAD - 2 * R), (0, 0)))
        o_fwd = _gla_direction(proj, lr, up_f, gla_gate_bias_fwd[layer], B, S, reverse=False)
        g_out = _gla_direction(proj, lr, up_b, gla_gate_bias_bwd[layer], B, S, reverse=True,
                               o_fwd=o_fwd, gain=gla_norm_gain[layer])

        xs = _outproj(a_out, g_out.reshape(B * S, GLA_WIDTH), w_out[layer].astype(BF16),
                      xs, gate, final_gain, S, final=layer == depth - 1)
    return xs.reshape(B, S, D)
```

```python
import functools
import math

import jax
import jax.numpy as jnp
import numpy as np
from jax import lax
from jax.experimental import pallas as pl
from jax.experimental.pallas import tpu as pltpu

D_MODEL = 2048
ATT_WIDTH = 1024
ATT_HEADS = 16
ATT_HEAD_DIM = 64
DILATED_PATTERNS = ((128, 1), (512, 4), (2048, 16))
ATT_STEPS = 64
GLA_WIDTH = 1024
GLA_HEADS = 4
GLA_KEY_WIDTH = 512
GLA_DK = 128
GLA_DV = 256
GLA_GATE_RANK = 16
GLA_GATE_NORM = 16.0
GLA_CHUNK = 64
REL_BUCKETS = 32
REL_MAX_DIST = 1024
EPS = 1e-6
NEG_INF = -1e30

PROJ_WIDTH = 4 * ATT_WIDTH + 2 * GLA_KEY_WIDTH + 2 * GLA_WIDTH
COL_AQ, COL_AK, COL_AV, COL_AG = 0, 1024, 2048, 3072
COL_GQ, COL_GK, COL_GV, COL_GG = 4096, 4608, 5120, 6144
LR_PAD = 128
ATT_QKV_TILES = 3
RESIDUE_DILATIONS = tuple(d for _, d in DILATED_PATTERNS if d > 1)
ATT_TQ = 128
ATT_TK = ATT_TQ + 2 * ATT_STEPS
ATT_GROUP_HEADS = 4

VMEM_LIMIT = 56 * 1024 * 1024

BF16 = jnp.bfloat16
F32 = jnp.float32


def _params(sem):
    return pltpu.CompilerParams(dimension_semantics=sem, vmem_limit_bytes=VMEM_LIMIT)


def _dot(a, b):
    return jnp.dot(a, b, preferred_element_type=F32)


def _dot_nt(a, b):
    return lax.dot_general(a, b, (((1,), (1,)), ((), ())), preferred_element_type=F32)


def _dot_tn(a, b):
    return lax.dot_general(a, b, (((0,), (0,)), ((), ())), preferred_element_type=F32)


def _split_bf16(x):
    hi = x.astype(BF16)
    lo = (x - hi.astype(F32)).astype(BF16)
    return hi, lo


def _silu(x):
    return x / (1.0 + jnp.exp(-x))


def _mod_kernel(c_ref, w_ref, b_ref, o_ref):
    s = _silu(c_ref[...])
    o_ref[...] = jnp.dot(s, w_ref[...], preferred_element_type=F32,
                         precision=lax.Precision.HIGHEST) + b_ref[...]


def _mod(c, w_cond, b_cond):
    B, D = c.shape
    N = w_cond.shape[1]
    tn = 768
    cp = jnp.pad(c, ((0, 8 - B), (0, 0)))
    out = pl.pallas_call(
        _mod_kernel,
        grid=(N // tn,),
        in_specs=[pl.BlockSpec((8, D), lambda j: (0, 0)),
                  pl.BlockSpec((D, tn), lambda j: (0, j)),
                  pl.BlockSpec((1, tn), lambda j: (0, j))],
        out_specs=pl.BlockSpec((8, tn), lambda j: (0, j)),
        out_shape=jax.ShapeDtypeStruct((8, N), F32),
        compiler_params=_params(("arbitrary",)),
        name="mod",
    )(cp, w_cond, b_cond.reshape(1, N))
    return out[:B]


def _inproj_kernel(x_ref, scale_ref, shift_ref, w_ref, wlr_ref, p_ref, lr_ref, *rest, tm, tn):
    nres = len(RESIDUE_DILATIONS)
    res_refs, h_scr, acc_scr = rest[:nres], rest[nres], rest[nres + 1:]
    j = pl.program_id(1)

    @pl.when(j == 0)
    def _():
        x = x_ref[...]
        ms = jnp.mean(x * x, axis=-1, keepdims=True)
        h = x * lax.rsqrt(ms + EPS) * (1.0 + scale_ref[...]) + shift_ref[...]
        hb = h.astype(BF16)
        h_scr[...] = hb
        lr_ref[...] = _dot(hb, wlr_ref[...])

    @pl.when(j >= ATT_QKV_TILES)
    def _():
        p_ref[...] = _dot(h_scr[...], w_ref[j]).astype(BF16)

    @pl.when(j < ATT_QKV_TILES)
    def _():
        h = h_scr[...]
        chunk = 256
        for c0 in range(0, tn, chunk):
            acc = _dot(h, w_ref[j, :, c0:c0 + chunk])
            p_ref[:, c0:c0 + chunk] = acc.astype(BF16)
            for c in range(c0 // 128, (c0 + chunk) // 128):
                lanes = slice(c * 128 - c0, (c + 1) * 128 - c0)
                src, prev_d = acc_scr[0], 1
                src[c] = acc[:, lanes]
                for lvl, (ref, d) in enumerate(zip(res_refs, RESIDUE_DILATIONS)):
                    ratio, n = d // prev_d, tm // d
                    dst = acc_scr[lvl + 1] if lvl + 1 < len(RESIDUE_DILATIONS) else None
                    for rp in range(prev_d):
                        for a in range(ratio):
                            r = rp + prev_d * a
                            rows = src[c, pl.ds(rp * (tm // prev_d) + a, n, stride=ratio), :]
                            ref[:, r * tn + c * 128:r * tn + (c + 1) * 128] = rows.astype(BF16)
                            if dst is not None:
                                dst[c, r * n:(r + 1) * n, :] = rows
                    src, prev_d = dst, d


def _inproj(x2, scale, shift, w_main, w_lr, seq):
    M, D = x2.shape
    tm, tn = 512, ATT_WIDTH
    bpt = seq // tm
    last_qkv = ATT_QKV_TILES - 1
    res_specs = [pl.BlockSpec((tm // d, d * tn), lambda i, j: (i, jnp.minimum(j, last_qkv)))
                 for d in RESIDUE_DILATIONS]
    res_shapes = [jax.ShapeDtypeStruct((M // d, ATT_QKV_TILES * d * tn), BF16)
                  for d in RESIDUE_DILATIONS]
    return pl.pallas_call(
        functools.partial(_inproj_kernel, tm=tm, tn=tn),
        grid=(M // tm, PROJ_WIDTH // tn),
        in_specs=[pl.BlockSpec((tm, D), lambda i, j: (i, 0)),
                  pl.BlockSpec((None, 1, D), lambda i, j: (i // bpt, 0, 0)),
                  pl.BlockSpec((None, 1, D), lambda i, j: (i // bpt, 0, 0)),
                  pl.BlockSpec((PROJ_WIDTH // tn, D, tn), lambda i, j: (0, 0, 0),
                               pipeline_mode=pl.Buffered(1)),
                  pl.BlockSpec((D, LR_PAD), lambda i, j: (0, 0))],
        out_specs=[pl.BlockSpec((tm, tn), lambda i, j: (i, j)),
                   pl.BlockSpec((tm, LR_PAD), lambda i, j: (i, 0))] + res_specs,
        out_shape=[jax.ShapeDtypeStruct((M, PROJ_WIDTH), BF16),
                   jax.ShapeDtypeStruct((M, LR_PAD), F32)] + res_shapes,
        scratch_shapes=[pltpu.VMEM((tm, D), BF16)]
                       + [pltpu.VMEM((tn // 128, tm, 128), F32) for _ in RESIDUE_DILATIONS],
        compiler_params=_params(("arbitrary", "arbitrary")),
        name="inproj",
    )(x2, scale, shift, w_main, w_lr)


def _t5_bucket_np(rel):
    nb = REL_BUCKETS // 2
    max_exact = nb // 2
    n = np.abs(rel)
    large = max_exact + (np.log(np.maximum(n, 1) / max_exact)
                         / np.log(REL_MAX_DIST / max_exact) * (nb - max_exact)).astype(np.int32)
    large = np.minimum(large, nb - 1)
    return (np.where(rel > 0, nb, 0) + np.where(n < max_exact, n, large)).astype(np.int32)


def _bias_kernel(rbt_ref, bucket_ref, mask_ref, o_ref):
    rbt = rbt_ref[...]
    bucket = bucket_ref[...]
    tbl = jnp.zeros((ATT_HEADS, bucket.shape[1]), F32)
    for b in range(REL_BUCKETS):
        tbl = jnp.where(bucket == b, rbt[:, b:b + 1], tbl)
    for v in range(3):
        o_ref[v] = jnp.where(mask_ref[v] > 0.5, tbl, NEG_INF)


def _bias_tiles(rel_bias, dilation):
    w, tq, tk = ATT_STEPS, ATT_TQ, ATT_TK
    qi = np.arange(tq)[:, None]
    kj = np.arange(tk)[None, :]
    step = kj - w - qi
    band = np.abs(step) <= w
    bucket = _t5_bucket_np(step * dilation).reshape(1, tq * tk)
    masks = np.stack([band & (kj >= w), band, band & (kj < tk - w)]).astype(np.float32)
    masks = masks.reshape(3, 1, tq * tk)
    out = pl.pallas_call(
        _bias_kernel,
        out_shape=jax.ShapeDtypeStruct((3, ATT_HEADS, tq * tk), F32),
        compiler_params=pltpu.CompilerParams(vmem_limit_bytes=VMEM_LIMIT),
        name=f"bias_d{dilation}",
    )(rel_bias.T, jnp.asarray(bucket), jnp.asarray(masks))
    return out.reshape(3, ATT_HEADS, tq, tk)


def _attn_kernel(q_ref, kp_ref, km_ref, kn_ref, vp_ref, vm_ref, vn_ref, bias_ref,
                 o_ref, m_ref, den_ref, k_lo, k_hi, v_slot, s_scr, p_scr, inv_scr, *, tb):
    w, tq, tk = ATT_STEPS, ATT_TQ, ATT_TK
    nsub = tb // tq
    npair = ATT_HEADS // 2
    gh = ATT_GROUP_HEADS
    gw = gh * ATT_HEAD_DIM
    ngroup = ATT_HEADS // gh
    nk = tb + 2 * w
    i = pl.program_id(2)
    first = i == 0
    last = i == pl.num_programs(2) - 1

    lane = lax.broadcasted_iota(jnp.int32, (1, ATT_WIDTH), 1)
    lower = (lane % 128) < ATT_HEAD_DIM
    slot = (lane % gw) // ATT_HEAD_DIM
    row = 0
    for kpart, vpart in ((kp_ref, vp_ref), (km_ref, vm_ref), (kn_ref, vn_ref)):
        rows = slice(row, row + kpart.shape[0])
        x = kpart[...]
        k_lo[rows] = jnp.where(lower, x, jnp.zeros_like(x))
        k_hi[rows] = jnp.where(lower, jnp.zeros_like(x), x)
        x = vpart[...]
        for s in range(gh):
            v_slot[s, rows] = jnp.where(slot == s, x, jnp.zeros_like(x))
        row += kpart.shape[0]

    half = tq // 2
    lower_half = lax.broadcasted_iota(jnp.int32, (half, 128), 1) < ATT_HEAD_DIM

    def sub(j, carry):
        qs = pl.multiple_of(j * tq, tq)
        var = jnp.where(jnp.logical_and(first, j == 0), 0,
                        jnp.where(jnp.logical_and(last, j == nsub - 1), 2, 1))
        for hp in range(npair):
            cp = slice(hp * 128, (hp + 1) * 128)
            q = q_ref[pl.ds(qs, tq), cp]
            s_scr[2 * hp] = _dot_nt(q, k_lo[pl.ds(qs, tk), cp])
            s_scr[2 * hp + 1] = _dot_nt(q, k_hi[pl.ds(qs, tk), cp])
        m_ref[pl.ds(qs, tq), :] = jnp.zeros((tq, 128), F32)
        den_ref[pl.ds(qs, tq), :] = jnp.ones((tq, 128), F32)
        for hp in range(npair):
            for r0 in (0, half):
                dens = []
                for h in (2 * hp, 2 * hp + 1):
                    s = s_scr[h, r0:r0 + half, :] + bias_ref[var, h, r0:r0 + half, :]
                    m = jnp.max(s, axis=-1, keepdims=True)
                    p = jnp.exp(s - m)
                    den = jnp.sum(p, axis=-1, keepdims=True)
                    p_scr[h // gh, r0:r0 + half, (h % gh) * tk:(h % gh + 1) * tk] = p.astype(BF16)
                    m_ref[pl.ds(qs + r0, half), h:h + 1] = m
                    den_ref[pl.ds(qs + r0, half), h:h + 1] = den
                    dens.append(den)
                inv_scr[(2 * hp) // gh, r0:r0 + half, (hp % (gh // 2)) * 128:(hp % (gh // 2) + 1) * 128] = (
                    1.0 / jnp.where(lower_half, dens[0], dens[1]))
        for g in range(ngroup):
            cg = slice(g * gw, (g + 1) * gw)
            v_stack = jnp.concatenate([v_slot[s, pl.ds(qs, tk), cg] for s in range(gh)], axis=0)
            o = _dot(p_scr[g], v_stack)
            o_ref[pl.ds(qs, tq), cg] = (o * inv_scr[g]).astype(BF16)
        return carry

    lax.fori_loop(0, nsub, sub, 0)


def _attn_pattern(qkv, bias, batch, seq, dilation):
    w = ATT_STEPS
    L = seq // dilation
    tb = min(512, L)
    nblk = L // tb
    hb = tb // w
    nhalo = L // w

    def main(j):
        return pl.BlockSpec((None, tb, ATT_WIDTH), lambda b, r, i: (b, i, j * dilation + r))

    def prev(j):
        return pl.BlockSpec((None, w, ATT_WIDTH),
                            lambda b, r, i: (b, jnp.maximum(i * hb - 1, 0), j * dilation + r))

    def nxt(j):
        return pl.BlockSpec((None, w, ATT_WIDTH),
                            lambda b, r, i: (b, jnp.minimum((i + 1) * hb, nhalo - 1), j * dilation + r))

    nk, gh = tb + 2 * w, ATT_GROUP_HEADS
    return pl.pallas_call(
        functools.partial(_attn_kernel, tb=tb),
        grid=(batch, dilation, nblk),
        in_specs=[main(0), prev(1), main(1), nxt(1), prev(2), main(2), nxt(2),
                  pl.BlockSpec((3, ATT_HEADS, ATT_TQ, ATT_TK), lambda b, r, i: (0, 0, 0, 0))],
        out_specs=[pl.BlockSpec((None, tb, ATT_WIDTH), lambda b, r, i: (b, i, r)),
                   pl.BlockSpec((None, tb, 128), lambda b, r, i: (b, i, r)),
                   pl.BlockSpec((None, tb, 128), lambda b, r, i: (b, i, r))],
        out_shape=[jax.ShapeDtypeStruct((batch, L, dilation * ATT_WIDTH), BF16),
                   jax.ShapeDtypeStruct((batch, L, dilation * 128), F32),
                   jax.ShapeDtypeStruct((batch, L, dilation * 128), F32)],
        scratch_shapes=[pltpu.VMEM((nk, ATT_WIDTH), BF16),
                        pltpu.VMEM((nk, ATT_WIDTH), BF16),
                        pltpu.VMEM((gh, nk, ATT_WIDTH), BF16),
                        pltpu.VMEM((ATT_HEADS, ATT_TQ, ATT_TK), F32),
                        pltpu.VMEM((ATT_HEADS // gh, ATT_TQ, gh * ATT_TK), BF16),
                        pltpu.VMEM((ATT_HEADS // gh, ATT_TQ, gh * ATT_HEAD_DIM), F32)],
        compiler_params=_params(("arbitrary", "arbitrary", "arbitrary")),
        name=f"attn_d{dilation}",
    )(qkv, qkv, qkv, qkv, qkv, qkv, qkv, bias)


def _merge_kernel(*refs, tm):
    nres = len(RESIDUE_DILATIONS)
    o1_ref, m1_ref, d1_ref = refs[:3]
    res = [refs[3 + 3 * n:6 + 3 * n] for n in range(nres)]
    ag_ref, e_ref, out_ref = refs[3 + 3 * nres:6 + 3 * nres]
    scr = [refs[6 + 3 * nres + 3 * n:9 + 3 * nres + 3 * n] for n in range(nres)]

    ncol = ATT_WIDTH // 128
    for d, (o_ref, m_ref, d_ref), (so, sm, sd) in zip(RESIDUE_DILATIONS, res, scr):
        for r in range(d):
            for c in range(ncol):
                col = r * ATT_WIDTH + c * 128
                so[c, pl.ds(r, tm // d, stride=d), :] = o_ref[:, col:col + 128].astype(F32)
            sm[pl.ds(r, tm // d, stride=d), :] = m_ref[:, r * 128:(r + 1) * 128]
            sd[pl.ds(r, tm // d, stride=d), :] = d_ref[:, r * 128:(r + 1) * 128]

    scr_o = [s[0] for s in scr]
    lses = [m1_ref[...] + jnp.log(d1_ref[...])] + [sm[...] + jnp.log(sd[...]) for _, sm, sd in scr]
    mx = functools.reduce(jnp.maximum, lses)
    es = [jnp.exp(l - mx) for l in lses]
    inv = 1.0 / functools.reduce(jnp.add, es)
    e = e_ref[...]

    def expand(wgt):
        hi, lo = _split_bf16(wgt)
        return _dot(hi, e) + _dot(lo, e)

    outs = [o1_ref[...].astype(F32)] + [jnp.concatenate([so[c] for c in range(ncol)], axis=1)
                                        for so in scr_o]
    att = functools.reduce(jnp.add, [expand(ei * inv) * o for ei, o in zip(es, outs)])
    out_ref[...] = (att * _silu(ag_ref[...].astype(F32))).astype(BF16)


def _merge(nat, res, proj):
    M = proj.shape[0]
    tm = 512
    expand = np.zeros((128, ATT_WIDTH), np.float32)
    for h in range(ATT_HEADS):
        expand[h, h * ATT_HEAD_DIM:(h + 1) * ATT_HEAD_DIM] = 1.0
    row = lambda rows, width: pl.BlockSpec((rows, width), lambda i: (i, 0))
    triple = lambda d: [row(tm // d, d * ATT_WIDTH), row(tm // d, d * 128), row(tm // d, d * 128)]
    in_specs = (triple(1) + [s for d in RESIDUE_DILATIONS for s in triple(d)]
                + [pl.BlockSpec((tm, ATT_WIDTH), lambda i: (i, COL_AG // ATT_WIDTH)),
                   pl.BlockSpec((128, ATT_WIDTH), lambda i: (0, 0))])
    scratch = []
    for _ in RESIDUE_DILATIONS:
        scratch += [pltpu.VMEM((ATT_WIDTH // 128, tm, 128), F32),
                    pltpu.VMEM((tm, 128), F32), pltpu.VMEM((tm, 128), F32)]
    return pl.pallas_call(
        functools.partial(_merge_kernel, tm=tm),
        grid=(M // tm,),
        in_specs=in_specs,
        out_specs=row(tm, ATT_WIDTH),
        out_shape=jax.ShapeDtypeStruct((M, ATT_WIDTH), BF16),
        scratch_shapes=scratch,
        compiler_params=_params(("arbitrary",)),
        name="merge",
    )(*nat, *[a for t in res for a in t], proj, jnp.asarray(expand, BF16))


def _gla_kernel(*refs, ts, reverse):
    if reverse:
        (q_ref, k_ref, v_ref, lr_ref, up_ref, gb_ref, of_ref, gg_ref, gain_ref,
         out_ref, state, qf_scr, kd_scr, ks_scr, oin_scr, st_scr) = refs
    else:
        (q_ref, k_ref, v_ref, lr_ref, up_ref, gb_ref,
         out_ref, state, qf_scr, kd_scr, ks_scr, oin_scr, st_scr) = refs
    C = GLA_CHUNK
    nchunk = ts // C
    chunks = [slice(c * C, (c + 1) * C) for c in range(nchunk)]

    @pl.when(pl.program_id(2) == 0)
    def _():
        state[...] = jnp.zeros_like(state)

    lr_hi, lr_lo = _split_bf16(lr_ref[...])
    up_hi, up_lo = _split_bf16(up_ref[...])
    z = _dot(lr_hi, up_hi) + _dot(lr_hi, up_lo) + _dot(lr_lo, up_hi) + gb_ref[...]
    log_g = (jnp.minimum(z, 0.0) - jnp.log(1.0 + jnp.exp(-jnp.abs(z)))) * (1.0 / GLA_GATE_NORM)


    row = lax.broadcasted_iota(jnp.int32, (C, GLA_DK), 0)
    dec_cols = []
    for rows in chunks:
        b = log_g[rows]
        s = 1
        while s < C:
            if reverse:
                b = b + jnp.where(row < C - s, pltpu.roll(b, C - s, 0), 0.0)
            else:
                b = b + jnp.where(row >= s, pltpu.roll(b, s, 0), 0.0)
            s *= 2
        b_edge = b[0:1] if reverse else b[C - 1:C]
        q = q_ref[rows, :].astype(F32)
        k = k_ref[rows, :].astype(F32)
        qf_scr[rows, :] = (q * jnp.exp(b) * (GLA_DK ** -0.5)).astype(BF16)
        kd_scr[rows, :] = (k * jnp.exp(-b)).astype(BF16)
        ks_scr[rows, :] = (k * jnp.exp(b_edge - b)).astype(BF16)
        dec = jnp.broadcast_to(jnp.exp(b_edge), (GLA_DK, GLA_DK)).T
        dec_cols.append(jnp.concatenate([dec, dec], axis=1))

    ri = lax.broadcasted_iota(jnp.int32, (C, C), 0)
    ci = lax.broadcasted_iota(jnp.int32, (C, C), 1)
    keep = (ci >= ri) if reverse else (ci <= ri)
    atts = [jnp.where(keep, _dot_nt(qf_scr[rows, :], kd_scr[rows, :]), 0.0).astype(BF16)
            for rows in chunks]

    kvs = []
    for rows, att in zip(chunks, atts):
        v = v_ref[rows, :]
        oin_scr[rows, :] = _dot(att, v)
        kvs.append(_dot_tn(ks_scr[rows, :], v))

    st = state[...]
    order = list(range(nchunk - 1, -1, -1) if reverse else range(nchunk))
    for c in order:
        st_scr[c] = st.astype(BF16)
        st = st * dec_cols[c] + kvs[c]
    state[...] = st

    for c in order:
        rows = chunks[c]
        o = oin_scr[rows, :] + _dot(qf_scr[rows, :], st_scr[c])
        if reverse:
            tot = o + of_ref[rows, :]
            ms = jnp.mean(tot * tot, axis=-1, keepdims=True)
            g_o = tot * lax.rsqrt(ms + EPS) * gain_ref[...]
            out_ref[rows, :] = (g_o * _silu(gg_ref[rows, :].astype(F32))).astype(BF16)
        else:
            out_ref[rows, :] = o


def _gla_direction(proj, lr, up_pad, gate_bias, batch, seq, reverse, o_fwd=None, gain=None):
    ts = 512
    nstep = seq // ts
    C = GLA_CHUNK
    p3 = proj.reshape(batch, seq, PROJ_WIDTH)
    lr3 = lr.reshape(batch, seq, LR_PAD)

    def step(i):
        return nstep - 1 - i if reverse else i

    def seq_block(width, col0):
        return pl.BlockSpec((None, ts, width), lambda b, h, i: (b, step(i), col0 // width + h))

    in_specs = [seq_block(GLA_DK, COL_GQ), seq_block(GLA_DK, COL_GK), seq_block(GLA_DV, COL_GV),
                pl.BlockSpec((None, ts, LR_PAD), lambda b, h, i: (b, step(i), 0)),
                pl.BlockSpec((LR_PAD, GLA_DK), lambda b, h, i: (0, h)),
                pl.BlockSpec((1, GLA_DK), lambda b, h, i: (0, h))]
    args = [p3, p3, p3, lr3, up_pad, gate_bias.reshape(1, GLA_KEY_WIDTH)]
    if reverse:
        in_specs += [pl.BlockSpec((None, ts, GLA_DV), lambda b, h, i: (b, step(i), h)),
                     seq_block(GLA_DV, COL_GG),
                     pl.BlockSpec((1, GLA_DV), lambda b, h, i: (0, h))]
        args += [o_fwd, p3, gain.reshape(1, GLA_WIDTH)]
        out_dtype = BF16
    else:
        out_dtype = F32
    return pl.pallas_call(
        functools.partial(_gla_kernel, ts=ts, reverse=reverse),
        grid=(batch, GLA_HEADS, nstep),
        in_specs=in_specs,
        out_specs=pl.BlockSpec((None, ts, GLA_DV), lambda b, h, i: (b, step(i), h)),
        out_shape=jax.ShapeDtypeStruct((batch, seq, GLA_WIDTH), out_dtype),
        scratch_shapes=[pltpu.VMEM((GLA_DK, GLA_DV), F32),
                        pltpu.VMEM((ts, GLA_DK), BF16),
                        pltpu.VMEM((ts, GLA_DK), BF16),
                        pltpu.VMEM((ts, GLA_DK), BF16),
                        pltpu.VMEM((ts, GLA_DV), F32),
                        pltpu.VMEM((ts // C, GLA_DK, GLA_DV), BF16)],
        compiler_params=_params(("arbitrary", "arbitrary", "arbitrary")),
        name="gla_bwd" if reverse else "gla_fwd",
    )(*args)


def _outproj_kernel(a_ref, g_ref, wa_ref, wg_ref, x_ref, gate_ref, fg_ref, o_ref, *, final):
    y = _dot(a_ref[...], wa_ref[...]) + _dot(g_ref[...], wg_ref[...])
    xn = x_ref[...] + gate_ref[...] * y
    if final:
        ms = jnp.mean(xn * xn, axis=-1, keepdims=True)
        xn = xn * lax.rsqrt(ms + EPS) * fg_ref[...]
    o_ref[...] = xn


def _outproj(a_out, g_out, w_out_bf16, x2, gate, final_gain, seq, final):
    M, D = x2.shape
    tm = 256
    bpt = seq // tm
    return pl.pallas_call(
        functools.partial(_outproj_kernel, final=final),
        grid=(M // tm,),
        in_specs=[pl.BlockSpec((tm, ATT_WIDTH), lambda i: (i, 0)),
                  pl.BlockSpec((tm, GLA_WIDTH), lambda i: (i, 0)),
                  pl.BlockSpec((ATT_WIDTH, D), lambda i: (0, 0)),
                  pl.BlockSpec((GLA_WIDTH, D), lambda i: (1, 0)),
                  pl.BlockSpec((tm, D), lambda i: (i, 0)),
                  pl.BlockSpec((None, 1, D), lambda i: (i // bpt, 0, 0)),
                  pl.BlockSpec((1, D), lambda i: (0, 0))],
        out_specs=pl.BlockSpec((tm, D), lambda i: (i, 0)),
        out_shape=jax.ShapeDtypeStruct((M, D), F32),
        compiler_params=_params(("arbitrary",)),
        name="outproj",
    )(a_out, g_out, w_out_bf16, w_out_bf16, x2, gate, final_gain.reshape(1, D))


def kernel(x, c, w_cond, b_cond, w_in, gla_gate_up_fwd, gla_gate_bias_fwd, gla_gate_up_bwd,
           gla_gate_bias_bwd, gla_norm_gain, rel_bias, w_out, final_gain):
    B, S, D = x.shape
    depth = w_cond.shape[0]
    R = GLA_GATE_RANK
    xs = x.reshape(B * S, D)
    for layer in range(depth):
        mod = _mod(c, w_cond[layer], b_cond[layer])
        shift, scale, gate = [m.reshape(B, 1, D) for m in jnp.split(mod, 3, axis=-1)]

        col_scale = jnp.where(jnp.arange(PROJ_WIDTH) < ATT_WIDTH, ATT_HEAD_DIM ** -0.5, 1.0)
        w_main = (w_in[layer, :, :PROJ_WIDTH] * col_scale).astype(BF16)
        w_main = w_main.reshape(D, PROJ_WIDTH // ATT_WIDTH, ATT_WIDTH).transpose(1, 0, 2)
        w_lr = jnp.pad(w_in[layer, :, PROJ_WIDTH:], ((0, 0), (0, LR_PAD - 2 * R))).astype(BF16)
        proj, lr, *res_qkv = _inproj(xs, scale, shift, w_main, w_lr, S)

        def rows2d(t, d):
            return [a.reshape(B * S // d, -1) for a in t]

        nat = rows2d(_attn_pattern(proj.reshape(B, S, PROJ_WIDTH), _bias_tiles(rel_bias, 1), B, S, 1), 1)
        res = [rows2d(_attn_pattern(qkv.reshape(B, S // d, -1), _bias_tiles(rel_bias, d), B, S, d), d)
               for d, qkv in zip(RESIDUE_DILATIONS, res_qkv)]
        a_out = _merge(nat, res, proj)

        up_f = jnp.pad(gla_gate_up_fwd[layer], ((0, LR_PAD - R), (0, 0)))
        up_b = jnp.pad(gla_gate_up_bwd[layer], ((R, LR_PAD - 2 * R), (0, 0)))
        o_fwd = _gla_direction(proj, lr, up_f, gla_gate_bias_fwd[layer], B, S, reverse=False)
        g_out = _gla_direction(proj, lr, up_b, gla_gate_bias_bwd[layer], B, S, reverse=True,
                               o_fwd=o_fwd, gain=gla_norm_gain[layer])

        xs = _outproj(a_out, g_out.reshape(B * S, GLA_WIDTH), w_out[layer].astype(BF16),
                      xs, gate, final_gain, S, final=layer == depth - 1)
    return xs.reshape(B, S, D)
```

```python
import functools
import math

import jax
import jax.numpy as jnp
import numpy as np
from jax import lax
from jax.experimental import pallas as pl
from jax.experimental.pallas import tpu as pltpu

D_MODEL = 2048
ATT_WIDTH = 1024
ATT_HEADS = 16
ATT_HEAD_DIM = 64
DILATED_PATTERNS = ((128, 1), (512, 4), (2048, 16))
ATT_STEPS = 64
GLA_WIDTH = 1024
GLA_HEADS = 4
GLA_KEY_WIDTH = 512
GLA_DK = 128
GLA_DV = 256
GLA_GATE_RANK = 16
GLA_GATE_NORM = 16.0
GLA_CHUNK = 64
REL_BUCKETS = 32
REL_MAX_DIST = 1024
EPS = 1e-6
NEG_INF = -1e30

PROJ_WIDTH = 4 * ATT_WIDTH + 2 * GLA_KEY_WIDTH + 2 * GLA_WIDTH
COL_AQ, COL_AK, COL_AV, COL_AG = 0, 1024, 2048, 3072
COL_GQ, COL_GK, COL_GV, COL_GG = 4096, 4608, 5120, 6144
LR_PAD = 128
ATT_QKV_TILES = 3
RESIDUE_DILATIONS = tuple(d for _, d in DILATED_PATTERNS if d > 1)
ATT_TQ = 128
ATT_TK = ATT_TQ + 2 * ATT_STEPS
ATT_GROUP_HEADS = 4

VMEM_LIMIT = 56 * 1024 * 1024

BF16 = jnp.bfloat16
F32 = jnp.float32


def _params(sem):
    return pltpu.CompilerParams(dimension_semantics=sem, vmem_limit_bytes=VMEM_LIMIT)


def _dot(a, b):
    return jnp.dot(a, b, preferred_element_type=F32)


def _dot_nt(a, b):
    return lax.dot_general(a, b, (((1,), (1,)), ((), ())), preferred_element_type=F32)


def _dot_tn(a, b):
    return lax.dot_general(a, b, (((0,), (0,)), ((), ())), preferred_element_type=F32)


def _split_bf16(x):
    hi = x.astype(BF16)
    lo = (x - hi.astype(F32)).astype(BF16)
    return hi, lo


def _silu(x):
    return x / (1.0 + jnp.exp(-x))


def _mod_kernel(c_ref, w_ref, b_ref, o_ref):
    s = _silu(c_ref[...])
    o_ref[...] = jnp.dot(s, w_ref[...], preferred_element_type=F32,
                         precision=lax.Precision.HIGHEST) + b_ref[...]


def _mod(c, w_cond, b_cond):
    B, D = c.shape
    N = w_cond.shape[1]
    tn = 768
    cp = jnp.pad(c, ((0, 8 - B), (0, 0)))
    out = pl.pallas_call(
        _mod_kernel,
        grid=(N // tn,),
        in_specs=[pl.BlockSpec((8, D), lambda j: (0, 0)),
                  pl.BlockSpec((D, tn), lambda j: (0, j)),
                  pl.BlockSpec((1, tn), lambda j: (0, j))],
        out_specs=pl.BlockSpec((8, tn), lambda j: (0, j)),
        out_shape=jax.ShapeDtypeStruct((8, N), F32),
        compiler_params=_params(("arbitrary",)),
        name="mod",
    )(cp, w_cond, b_cond.reshape(1, N))
    return out[:B]


def _inproj_kernel(x_ref, scale_ref, shift_ref, w_ref, wlr_ref, p_ref, lr_ref, *rest, tm, tn):
    nres = len(RESIDUE_DILATIONS)
    res_refs, h_scr, acc_scr = rest[:nres], rest[nres], rest[nres + 1:]
    j = pl.program_id(1)

    @pl.when(j == 0)
    def _():
        x = x_ref[...]
        ms = jnp.mean(x * x, axis=-1, keepdims=True)
        h = x * lax.rsqrt(ms + EPS) * (1.0 + scale_ref[...]) + shift_ref[...]
        hb = h.astype(BF16)
        h_scr[...] = hb
        lr_ref[...] = _dot(hb, wlr_ref[...])

    @pl.when(j >= ATT_QKV_TILES)
    def _():
        p_ref[...] = _dot(h_scr[...], w_ref[j]).astype(BF16)

    @pl.when(j < ATT_QKV_TILES)
    def _():
        h = h_scr[...]
        chunk = 256
        for c0 in range(0, tn, chunk):
            acc = _dot(h, w_ref[j, :, c0:c0 + chunk])
            p_ref[:, c0:c0 + chunk] = acc.astype(BF16)
            for c in range(c0 // 128, (c0 + chunk) // 128):
                lanes = slice(c * 128 - c0, (c + 1) * 128 - c0)
                src, prev_d = acc_scr[0], 1
                src[c] = acc[:, lanes]
                for lvl, (ref, d) in enumerate(zip(res_refs, RESIDUE_DILATIONS)):
                    ratio, n = d // prev_d, tm // d
                    dst = acc_scr[lvl + 1] if lvl + 1 < len(RESIDUE_DILATIONS) else None
                    for rp in range(prev_d):
                        for a in range(ratio):
                            r = rp + prev_d * a
                            rows = src[c, pl.ds(rp * (tm // prev_d) + a, n, stride=ratio), :]
                            ref[:, r * tn + c * 128:r * tn + (c + 1) * 128] = rows.astype(BF16)
                            if dst is not None:
                                dst[c, r * n:(r + 1) * n, :] = rows
                    src, prev_d = dst, d


def _inproj(x2, scale, shift, w_main, w_lr, seq):
    M, D = x2.shape
    tm, tn = 512, ATT_WIDTH
    bpt = seq // tm
    last_qkv = ATT_QKV_TILES - 1
    res_specs = [pl.BlockSpec((tm // d, d * tn), lambda i, j: (i, jnp.minimum(j, last_qkv)))
                 for d in RESIDUE_DILATIONS]
    res_shapes = [jax.ShapeDtypeStruct((M // d, ATT_QKV_TILES * d * tn), BF16)
                  for d in RESIDUE_DILATIONS]
    return pl.pallas_call(
        functools.partial(_inproj_kernel, tm=tm, tn=tn),
        grid=(M // tm, PROJ_WIDTH // tn),
        in_specs=[pl.BlockSpec((tm, D), lambda i, j: (i, 0)),
                  pl.BlockSpec((None, 1, D), lambda i, j: (i // bpt, 0, 0)),
                  pl.BlockSpec((None, 1, D), lambda i, j: (i // bpt, 0, 0)),
                  pl.BlockSpec((PROJ_WIDTH // tn, D, tn), lambda i, j: (0, 0, 0),
                               pipeline_mode=pl.Buffered(1)),
                  pl.BlockSpec((D, LR_PAD), lambda i, j: (0, 0))],
        out_specs=[pl.BlockSpec((tm, tn), lambda i, j: (i, j)),
                   pl.BlockSpec((tm, LR_PAD), lambda i, j: (i, 0))] + res_specs,
        out_shape=[jax.ShapeDtypeStruct((M, PROJ_WIDTH), BF16),
                   jax.ShapeDtypeStruct((M, LR_PAD), F32)] + res_shapes,
        scratch_shapes=[pltpu.VMEM((tm, D), BF16)]
                       + [pltpu.VMEM((tn // 128, tm, 128), F32) for _ in RESIDUE_DILATIONS],
        compiler_params=_params(("arbitrary", "arbitrary")),
        name="inproj",
    )(x2, scale, shift, w_main, w_lr)


def _t5_bucket_np(rel):
    nb = REL_BUCKETS // 2
    max_exact = nb // 2
    n = np.abs(rel)
    large = max_exact + (np.log(np.maximum(n, 1) / max_exact)
                         / np.log(REL_MAX_DIST / max_exact) * (nb - max_exact)).astype(np.int32)
    large = np.minimum(large, nb - 1)
    return (np.where(rel > 0, nb, 0) + np.where(n < max_exact, n, large)).astype(np.int32)


def _bias_kernel(rbt_ref, bucket_ref, mask_ref, o_ref):
    rbt = rbt_ref[...]
    bucket = bucket_ref[...]
    ids = lax.broadcasted_iota(jnp.int32, (REL_BUCKETS, bucket.shape[1]), 0)
    onehot = jnp.where(ids == bucket, 1.0, 0.0).astype(BF16)
    hi = rbt.astype(BF16)
    rest = rbt - hi.astype(F32)
    mid = rest.astype(BF16)
    lo = (rest - mid.astype(F32)).astype(BF16)
    tbl = _dot(hi, onehot) + _dot(mid, onehot) + _dot(lo, onehot)
    for v in range(3):
        o_ref[v] = jnp.where(mask_ref[v] > 0.5, tbl, NEG_INF)


def _bias_tiles(rel_bias, dilation):
    w, tq, tk = ATT_STEPS, ATT_TQ, ATT_TK
    qi = np.arange(tq)[:, None]
    kj = np.arange(tk)[None, :]
    step = kj - w - qi
    band = np.abs(step) <= w
    bucket = _t5_bucket_np(step * dilation).reshape(1, tq * tk)
    masks = np.stack([band & (kj >= w), band, band & (kj < tk - w)]).astype(np.float32)
    masks = masks.reshape(3, 1, tq * tk)
    out = pl.pallas_call(
        _bias_kernel,
        out_shape=jax.ShapeDtypeStruct((3, ATT_HEADS, tq * tk), F32),
        compiler_params=pltpu.CompilerParams(vmem_limit_bytes=VMEM_LIMIT),
        name=f"bias_d{dilation}",
    )(rel_bias.T, jnp.asarray(bucket), jnp.asarray(masks))
    return out.reshape(3, ATT_HEADS, tq, tk)


def _attn_kernel(q_ref, kp_ref, km_ref, kn_ref, vp_ref, vm_ref, vn_ref, bias_ref,
                 o_ref, m_ref, den_ref, k_lo, k_hi, v_slot, s_scr, p_scr, inv_scr, *, tb):
    w, tq, tk = ATT_STEPS, ATT_TQ, ATT_TK
    nsub = tb // tq
    npair = ATT_HEADS // 2
    gh = ATT_GROUP_HEADS
    gw = gh * ATT_HEAD_DIM
    ngroup = ATT_HEADS // gh
    nk = tb + 2 * w
    i = pl.program_id(2)
    first = i == 0
    last = i == pl.num_programs(2) - 1

    lane = lax.broadcasted_iota(jnp.int32, (1, ATT_WIDTH), 1)
    lower = (lane % 128) < ATT_HEAD_DIM
    slot = (lane % gw) // ATT_HEAD_DIM
    row = 0
    for kpart, vpart in ((kp_ref, vp_ref), (km_ref, vm_ref), (kn_ref, vn_ref)):
        rows = slice(row, row + kpart.shape[0])
        x = kpart[...]
        k_lo[rows] = jnp.where(lower, x, jnp.zeros_like(x))
        k_hi[rows] = jnp.where(lower, jnp.zeros_like(x), x)
        x = vpart[...]
        for s in range(gh):
            v_slot[s, rows] = jnp.where(slot == s, x, jnp.zeros_like(x))
        row += kpart.shape[0]

    half = tq // 2
    lower_half = lax.broadcasted_iota(jnp.int32, (half, 128), 1) < ATT_HEAD_DIM

    def sub(j, carry):
        qs = pl.multiple_of(j * tq, tq)
        var = jnp.where(jnp.logical_and(first, j == 0), 0,
                        jnp.where(jnp.logical_and(last, j == nsub - 1), 2, 1))
        for hp in range(npair):
            cp = slice(hp * 128, (hp + 1) * 128)
            q = q_ref[pl.ds(qs, tq), cp]
            s_scr[2 * hp] = _dot_nt(q, k_lo[pl.ds(qs, tk), cp])
            s_scr[2 * hp + 1] = _dot_nt(q, k_hi[pl.ds(qs, tk), cp])
        m_ref[pl.ds(qs, tq), :] = jnp.zeros((tq, 128), F32)
        den_ref[pl.ds(qs, tq), :] = jnp.ones((tq, 128), F32)
        for hp in range(npair):
            for r0 in (0, half):
                dens = []
                for h in (2 * hp, 2 * hp + 1):
                    s = s_scr[h, r0:r0 + half, :] + bias_ref[var, h, r0:r0 + half, :]
                    m = jnp.max(s, axis=-1, keepdims=True)
                    p = jnp.exp(s - m)
                    den = jnp.sum(p, axis=-1, keepdims=True)
                    p_scr[h // gh, r0:r0 + half, (h % gh) * tk:(h % gh + 1) * tk] = p.astype(BF16)
                    m_ref[pl.ds(qs + r0, half), h:h + 1] = m
                    den_ref[pl.ds(qs + r0, half), h:h + 1] = den
                    dens.append(den)
                inv_scr[(2 * hp) // gh, r0:r0 + half, (hp % (gh // 2)) * 128:(hp % (gh // 2) + 1) * 128] = (
                    1.0 / jnp.where(lower_half, dens[0], dens[1]))
        for g in range(ngroup):
            cg = slice(g * gw, (g + 1) * gw)
            v_stack = jnp.concatenate([v_slot[s, pl.ds(qs, tk), cg] for s in range(gh)], axis=0)
            o = _dot(p_scr[g], v_stack)
            o_ref[pl.ds(qs, tq), cg] = (o * inv_scr[g]).astype(BF16)
        return carry

    lax.fori_loop(0, nsub, sub, 0)


def _attn_pattern(qkv, bias, batch, seq, dilation):
    w = ATT_STEPS
    L = seq // dilation
    tb = min(512, L)
    nblk = L // tb
    hb = tb // w
    nhalo = L // w

    def main(j):
        return pl.BlockSpec((None, tb, ATT_WIDTH), lambda b, r, i: (b, i, j * dilation + r))

    def prev(j):
        return pl.BlockSpec((None, w, ATT_WIDTH),
                            lambda b, r, i: (b, jnp.maximum(i * hb - 1, 0), j * dilation + r))

    def nxt(j):
        return pl.BlockSpec((None, w, ATT_WIDTH),
                            lambda b, r, i: (b, jnp.minimum((i + 1) * hb, nhalo - 1), j * dilation + r))

    nk, gh = tb + 2 * w, ATT_GROUP_HEADS
    return pl.pallas_call(
        functools.partial(_attn_kernel, tb=tb),
        grid=(batch, dilation, nblk),
        in_specs=[main(0), prev(1), main(1), nxt(1), prev(2), main(2), nxt(2),
                  pl.BlockSpec((3, ATT_HEADS, ATT_TQ, ATT_TK), lambda b, r, i: (0, 0, 0, 0))],
        out_specs=[pl.BlockSpec((None, tb, ATT_WIDTH), lambda b, r, i: (b, i, r)),
                   pl.BlockSpec((None, tb, 128), lambda b, r, i: (b, i, r)),
                   pl.BlockSpec((None, tb, 128), lambda b, r, i: (b, i, r))],
        out_shape=[jax.ShapeDtypeStruct((batch, L, dilation * ATT_WIDTH), BF16),
                   jax.ShapeDtypeStruct((batch, L, dilation * 128), F32),
                   jax.ShapeDtypeStruct((batch, L, dilation * 128), F32)],
        scratch_shapes=[pltpu.VMEM((nk, ATT_WIDTH), BF16),
                        pltpu.VMEM((nk, ATT_WIDTH), BF16),
                        pltpu.VMEM((gh, nk, ATT_WIDTH), BF16),
                        pltpu.VMEM((ATT_HEADS, ATT_TQ, ATT_TK), F32),
                        pltpu.VMEM((ATT_HEADS // gh, ATT_TQ, gh * ATT_TK), BF16),
                        pltpu.VMEM((ATT_HEADS // gh, ATT_TQ, gh * ATT_HEAD_DIM), F32)],
        compiler_params=_params(("arbitrary", "arbitrary", "arbitrary")),
        name=f"attn_d{dilation}",
    )(qkv, qkv, qkv, qkv, qkv, qkv, qkv, bias)


def _merge_kernel(*refs, tm):
    nres = len(RESIDUE_DILATIONS)
    o1_ref, m1_ref, d1_ref = refs[:3]
    res = [refs[3 + 3 * n:6 + 3 * n] for n in range(nres)]
    ag_ref, e_ref, out_ref = refs[3 + 3 * nres:6 + 3 * nres]
    scr = [refs[6 + 3 * nres + 3 * n:9 + 3 * nres + 3 * n] for n in range(nres)]

    ncol = ATT_WIDTH // 128
    for d, (o_ref, m_ref, d_ref), (so, sm, sd) in zip(RESIDUE_DILATIONS, res, scr):
        for r in range(d):
            for c in range(ncol):
                col = r * ATT_WIDTH + c * 128
                so[c, pl.ds(r, tm // d, stride=d), :] = o_ref[:, col:col + 128].astype(F32)
            sm[pl.ds(r, tm // d, stride=d), :] = m_ref[:, r * 128:(r + 1) * 128]
            sd[pl.ds(r, tm // d, stride=d), :] = d_ref[:, r * 128:(r + 1) * 128]

    scr_o = [s[0] for s in scr]
    lses = [m1_ref[...] + jnp.log(d1_ref[...])] + [sm[...] + jnp.log(sd[...]) for _, sm, sd in scr]
    mx = functools.reduce(jnp.maximum, lses)
    es = [jnp.exp(l - mx) for l in lses]
    inv = 1.0 / functools.reduce(jnp.add, es)
    e = e_ref[...]

    def expand(wgt):
        hi, lo = _split_bf16(wgt)
        return _dot(hi, e) + _dot(lo, e)

    outs = [o1_ref[...].astype(F32)] + [jnp.concatenate([so[c] for c in range(ncol)], axis=1)
                                        for so in scr_o]
    att = functools.reduce(jnp.add, [expand(ei * inv) * o for ei, o in zip(es, outs)])
    out_ref[...] = (att * _silu(ag_ref[...].astype(F32))).astype(BF16)


def _merge(nat, res, proj):
    M = proj.shape[0]
    tm = 512
    expand = np.zeros((128, ATT_WIDTH), np.float32)
    for h in range(ATT_HEADS):
        expand[h, h * ATT_HEAD_DIM:(h + 1) * ATT_HEAD_DIM] = 1.0
    row = lambda rows, width: pl.BlockSpec((rows, width), lambda i: (i, 0))
    triple = lambda d: [row(tm // d, d * ATT_WIDTH), row(tm // d, d * 128), row(tm // d, d * 128)]
    in_specs = (triple(1) + [s for d in RESIDUE_DILATIONS for s in triple(d)]
                + [pl.BlockSpec((tm, ATT_WIDTH), lambda i: (i, COL_AG // ATT_WIDTH)),
                   pl.BlockSpec((128, ATT_WIDTH), lambda i: (0, 0))])
    scratch = []
    for _ in RESIDUE_DILATIONS:
        scratch += [pltpu.VMEM((ATT_WIDTH // 128, tm, 128), F32),
                    pltpu.VMEM((tm, 128), F32), pltpu.VMEM((tm, 128), F32)]
    return pl.pallas_call(
        functools.partial(_merge_kernel, tm=tm),
        grid=(M // tm,),
        in_specs=in_specs,
        out_specs=row(tm, ATT_WIDTH),
        out_shape=jax.ShapeDtypeStruct((M, ATT_WIDTH), BF16),
        scratch_shapes=scratch,
        compiler_params=_params(("arbitrary",)),
        name="merge",
    )(*nat, *[a for t in res for a in t], proj, jnp.asarray(expand, BF16))


def _gla_kernel(*refs, ts, nstep):
    fwd_in, bwd_in = refs[:5], refs[5:10]
    (upf_ref, upb_ref, gbf_ref, gbb_ref, gain_ref, out_ref,
     state, o_acc, qf_scr, kd_scr, ks_scr, oin_scr, st_scr) = refs[10:]
    C = GLA_CHUNK
    nchunk = ts // C
    chunks = [slice(c * C, (c + 1) * C) for c in range(nchunk)]
    i = pl.program_id(2)
    dirs = [(0, False, fwd_in, upf_ref, gbf_ref), (1, True, bwd_in, upb_ref, gbb_ref)]

    @pl.when(i == 0)
    def _():
        state[...] = jnp.zeros_like(state)


    log_gs = []
    for d, reverse, (q_ref, k_ref, v_ref, lr_ref, gg_ref), up_ref, gb_ref in dirs:
        lr_hi, lr_lo = _split_bf16(lr_ref[...])
        up_hi, up_lo = _split_bf16(up_ref[...])
        z = _dot(lr_hi, up_hi) + _dot(lr_hi, up_lo) + _dot(lr_lo, up_hi) + gb_ref[...]
        log_gs.append((jnp.minimum(z, 0.0) - jnp.log(1.0 + jnp.exp(-jnp.abs(z))))
                      * (1.0 / GLA_GATE_NORM))

    row = lax.broadcasted_iota(jnp.int32, (C, GLA_DK), 0)
    dec_cols = [[], []]
    for d, reverse, (q_ref, k_ref, v_ref, lr_ref, gg_ref), up_ref, gb_ref in dirs:
        for rows in chunks:
            b = log_gs[d][rows]
            s = 1
            while s < C:
                if reverse:
                    b = b + jnp.where(row < C - s, pltpu.roll(b, C - s, 0), 0.0)
                else:
                    b = b + jnp.where(row >= s, pltpu.roll(b, s, 0), 0.0)
                s *= 2
            b_edge = b[0:1] if reverse else b[C - 1:C]
            q = q_ref[rows, :].astype(F32)
            k = k_ref[rows, :].astype(F32)
            qf_scr[d, rows, :] = (q * jnp.exp(b) * (GLA_DK ** -0.5)).astype(BF16)
            kd_scr[d, rows, :] = (k * jnp.exp(-b)).astype(BF16)
            ks_scr[d, rows, :] = (k * jnp.exp(b_edge - b)).astype(BF16)
            dec = jnp.broadcast_to(jnp.exp(b_edge), (GLA_DK, GLA_DK)).T
            dec_cols[d].append(jnp.concatenate([dec, dec], axis=1))

    ri = lax.broadcasted_iota(jnp.int32, (C, C), 0)
    ci = lax.broadcasted_iota(jnp.int32, (C, C), 1)
    atts = [[jnp.where((ci >= ri) if reverse else (ci <= ri),
                       _dot_nt(qf_scr[d, rows, :], kd_scr[d, rows, :]), 0.0).astype(BF16)
             for rows in chunks] for d, reverse, *_ in dirs]

    kvs = [[], []]
    for d, reverse, (q_ref, k_ref, v_ref, lr_ref, gg_ref), up_ref, gb_ref in dirs:
        for rows, att in zip(chunks, atts[d]):
            v = v_ref[rows, :]
            oin_scr[d, rows, :] = _dot(att, v)
            kvs[d].append(_dot_tn(ks_scr[d, rows, :], v))

    orders = [list(range(nchunk)), list(range(nchunk - 1, -1, -1))]
    for d, reverse, *_ in dirs:
        st = state[d]
        for c in orders[d]:
            st_scr[d, c] = st.astype(BF16)
            st = st * dec_cols[d][c] + kvs[d][c]
        state[d] = st

    for d, reverse, *_ in dirs:
        for c in orders[d]:
            rows = chunks[c]
            oin_scr[d, rows, :] = oin_scr[d, rows, :] + _dot(qf_scr[d, rows, :], st_scr[d, c])

    blocks = [i, nstep - 1 - i]

    @pl.when(i < nstep // 2)
    def _():
        for d, reverse, *_ in dirs:
            base = pl.multiple_of(blocks[d] * ts, ts)
            for rows in chunks:
                o_acc[pl.ds(base + rows.start, C), :] = oin_scr[d, rows, :]

    @pl.when(i >= nstep // 2)
    def _():
        for d, reverse, (q_ref, k_ref, v_ref, lr_ref, gg_ref), up_ref, gb_ref in dirs:
            base = pl.multiple_of(blocks[d] * ts, ts)
            for rows in chunks:
                dst = pl.ds(base + rows.start, C)
                tot = oin_scr[d, rows, :] + o_acc[dst, :]
                ms = jnp.mean(tot * tot, axis=-1, keepdims=True)
                g_o = tot * lax.rsqrt(ms + EPS) * gain_ref[...]
                out_ref[dst, :] = (g_o * _silu(gg_ref[rows, :].astype(F32))).astype(BF16)


def _gla(proj, lr, up_f, up_b, bias_f, bias_b, gain, batch, seq):
    ts = 512
    nstep = seq // ts
    assert nstep % 2 == 0
    C = GLA_CHUNK
    p3 = proj.reshape(batch, seq, PROJ_WIDTH)
    lr3 = lr.reshape(batch, seq, LR_PAD)

    def direction_specs(step):
        def seq_block(width, col0):
            return pl.BlockSpec((None, ts, width), lambda b, h, i: (b, step(i), col0 // width + h))
        return [seq_block(GLA_DK, COL_GQ), seq_block(GLA_DK, COL_GK), seq_block(GLA_DV, COL_GV),
                pl.BlockSpec((None, ts, LR_PAD), lambda b, h, i: (b, step(i), 0)),
                seq_block(GLA_DV, COL_GG)]

    per_head = lambda rows, width: pl.BlockSpec((rows, width), lambda b, h, i: (0, h))
    in_specs = (direction_specs(lambda i: i) + direction_specs(lambda i: nstep - 1 - i)
                + [per_head(LR_PAD, GLA_DK), per_head(LR_PAD, GLA_DK),
                   per_head(1, GLA_DK), per_head(1, GLA_DK), per_head(1, GLA_DV)])
    dir_args = [p3, p3, p3, lr3, p3]
    return pl.pallas_call(
        functools.partial(_gla_kernel, ts=ts, nstep=nstep),
        grid=(batch, GLA_HEADS, nstep),
        in_specs=in_specs,
        out_specs=pl.BlockSpec((None, seq, GLA_DV), lambda b, h, i: (b, 0, h)),
        out_shape=jax.ShapeDtypeStruct((batch, seq, GLA_WIDTH), BF16),
        scratch_shapes=[pltpu.VMEM((2, GLA_DK, GLA_DV), F32),
                        pltpu.VMEM((seq, GLA_DV), F32),
                        pltpu.VMEM((2, ts, GLA_DK), BF16),
                        pltpu.VMEM((2, ts, GLA_DK), BF16),
                        pltpu.VMEM((2, ts, GLA_DK), BF16),
                        pltpu.VMEM((2, ts, GLA_DV), F32),
                        pltpu.VMEM((2, ts // C, GLA_DK, GLA_DV), BF16)],
        compiler_params=_params(("arbitrary", "arbitrary", "arbitrary")),
        name="gla",
    )(*dir_args, *dir_args, up_f, up_b, bias_f.reshape(1, GLA_KEY_WIDTH),
      bias_b.reshape(1, GLA_KEY_WIDTH), gain.reshape(1, GLA_WIDTH))


def _outproj_kernel(a_ref, g_ref, wa_ref, wg_ref, x_ref, gate_ref, fg_ref, o_ref, *, final):
    y = _dot(a_ref[...], wa_ref[...]) + _dot(g_ref[...], wg_ref[...])
    xn = x_ref[...] + gate_ref[...] * y
    if final:
        ms = jnp.mean(xn * xn, axis=-1, keepdims=True)
        xn = xn * lax.rsqrt(ms + EPS) * fg_ref[...]
    o_ref[...] = xn


def _outproj(a_out, g_out, w_out_bf16, x2, gate, final_gain, seq, final):
    M, D = x2.shape
    tm = 256
    bpt = seq // tm
    return pl.pallas_call(
        functools.partial(_outproj_kernel, final=final),
        grid=(M // tm,),
        in_specs=[pl.BlockSpec((tm, ATT_WIDTH), lambda i: (i, 0)),
                  pl.BlockSpec((tm, GLA_WIDTH), lambda i: (i, 0)),
                  pl.BlockSpec((ATT_WIDTH, D), lambda i: (0, 0)),
                  pl.BlockSpec((GLA_WIDTH, D), lambda i: (1, 0)),
                  pl.BlockSpec((tm, D), lambda i: (i, 0)),
                  pl.BlockSpec((None, 1, D), lambda i: (i // bpt, 0, 0)),
                  pl.BlockSpec((1, D), lambda i: (0, 0))],
        out_specs=pl.BlockSpec((tm, D), lambda i: (i, 0)),
        out_shape=jax.ShapeDtypeStruct((M, D), F32),
        compiler_params=_params(("arbitrary",)),
        name="outproj",
    )(a_out, g_out, w_out_bf16, w_out_bf16, x2, gate, final_gain.reshape(1, D))


def kernel(x, c, w_cond, b_cond, w_in, gla_gate_up_fwd, gla_gate_bias_fwd, gla_gate_up_bwd,
           gla_gate_bias_bwd, gla_norm_gain, rel_bias, w_out, final_gain):
    B, S, D = x.shape
    depth = w_cond.shape[0]
    R = GLA_GATE_RANK
    xs = x.reshape(B * S, D)
    for layer in range(depth):
        mod = _mod(c, w_cond[layer], b_cond[layer])
        shift, scale, gate = [m.reshape(B, 1, D) for m in jnp.split(mod, 3, axis=-1)]

        col_scale = jnp.where(jnp.arange(PROJ_WIDTH) < ATT_WIDTH, ATT_HEAD_DIM ** -0.5, 1.0)
        w_main = (w_in[layer, :, :PROJ_WIDTH] * col_scale).astype(BF16)
        w_main = w_main.reshape(D, PROJ_WIDTH // ATT_WIDTH, ATT_WIDTH).transpose(1, 0, 2)
        w_lr = jnp.pad(w_in[layer, :, PROJ_WIDTH:], ((0, 0), (0, LR_PAD - 2 * R))).astype(BF16)
        proj, lr, *res_qkv = _inproj(xs, scale, shift, w_main, w_lr, S)

        def rows2d(t, d):
            return [a.reshape(B * S // d, -1) for a in t]

        nat = rows2d(_attn_pattern(proj.reshape(B, S, PROJ_WIDTH), _bias_tiles(rel_bias, 1), B, S, 1), 1)
        res = [rows2d(_attn_pattern(qkv.reshape(B, S // d, -1), _bias_tiles(rel_bias, d), B, S, d), d)
               for d, qkv in zip(RESIDUE_DILATIONS, res_qkv)]
        a_out = _merge(nat, res, proj)

        up_f = jnp.pad(gla_gate_up_fwd[layer], ((0, LR_PAD - R), (0, 0)))
        up_b = jnp.pad(gla_gate_up_bwd[layer], ((R, LR_PAD - 2 * R), (0, 0)))
        g_out = _gla(proj, lr, up_f, up_b, gla_gate_bias_fwd[layer], gla_gate_bias_bwd[layer],
                     gla_norm_gain[layer], B, S)

        xs = _outproj(a_out, g_out.reshape(B * S, GLA_WIDTH), w_out[layer].astype(BF16),
                      xs, gate, final_gain, S, final=layer == depth - 1)
    return xs.reshape(B, S, D)
```

```python
import functools
import math

import jax
import jax.numpy as jnp
import numpy as np
from jax import lax
from jax.experimental import pallas as pl
from jax.experimental.pallas import tpu as pltpu

D_MODEL = 2048
ATT_WIDTH = 1024
ATT_HEADS = 16
ATT_HEAD_DIM = 64
DILATED_PATTERNS = ((128, 1), (512, 4), (2048, 16))
ATT_STEPS = 64
GLA_WIDTH = 1024
GLA_HEADS = 4
GLA_KEY_WIDTH = 512
GLA_DK = 128
GLA_DV = 256
GLA_GATE_RANK = 16
GLA_GATE_NORM = 16.0
GLA_CHUNK = 64
REL_BUCKETS = 32
REL_MAX_DIST = 1024
EPS = 1e-6
NEG_INF = -1e30

PROJ_WIDTH = 4 * ATT_WIDTH + 2 * GLA_KEY_WIDTH + 2 * GLA_WIDTH
COL_AQ, COL_AK, COL_AV, COL_AG = 0, 1024, 2048, 3072
COL_GQ, COL_GK, COL_GV, COL_GG = 4096, 4608, 5120, 6144
LR_PAD = 128
ATT_QKV_TILES = 3
RESIDUE_DILATIONS = tuple(d for _, d in DILATED_PATTERNS if d > 1)
ATT_TQ = 128
ATT_TK = ATT_TQ + 2 * ATT_STEPS
ATT_GROUP_HEADS = 4

VMEM_LIMIT = 56 * 1024 * 1024

BF16 = jnp.bfloat16
F32 = jnp.float32


def _params(sem):
    return pltpu.CompilerParams(dimension_semantics=sem, vmem_limit_bytes=VMEM_LIMIT)


def _dot(a, b):
    return jnp.dot(a, b, preferred_element_type=F32)


def _dot_nt(a, b):
    return lax.dot_general(a, b, (((1,), (1,)), ((), ())), preferred_element_type=F32)


def _dot_tn(a, b):
    return lax.dot_general(a, b, (((0,), (0,)), ((), ())), preferred_element_type=F32)


def _split_bf16(x):
    hi = x.astype(BF16)
    lo = (x - hi.astype(F32)).astype(BF16)
    return hi, lo


def _silu(x):
    return x / (1.0 + jnp.exp(-x))


def _mod_kernel(c_ref, w_ref, b_ref, o_ref):
    s = _silu(c_ref[...])
    o_ref[...] = jnp.dot(s, w_ref[...], preferred_element_type=F32,
                         precision=lax.Precision.HIGHEST) + b_ref[...]


def _mod(c, w_cond, b_cond):
    B, D = c.shape
    N = w_cond.shape[1]
    tn = 768
    cp = jnp.pad(c, ((0, 8 - B), (0, 0)))
    out = pl.pallas_call(
        _mod_kernel,
        grid=(N // tn,),
        in_specs=[pl.BlockSpec((8, D), lambda j: (0, 0)),
                  pl.BlockSpec((D, tn), lambda j: (0, j)),
                  pl.BlockSpec((1, tn), lambda j: (0, j))],
        out_specs=pl.BlockSpec((8, tn), lambda j: (0, j)),
        out_shape=jax.ShapeDtypeStruct((8, N), F32),
        compiler_params=_params(("arbitrary",)),
        name="mod",
    )(cp, w_cond, b_cond.reshape(1, N))
    return out[:B]


def _inproj_kernel(x_ref, scale_ref, shift_ref, w_ref, wlr_ref, p_ref, lr_ref, *rest, tm, tn):
    nres = len(RESIDUE_DILATIONS)
    res_refs, h_scr, acc_scr = rest[:nres], rest[nres], rest[nres + 1:]
    j = pl.program_id(1)

    @pl.when(j == 0)
    def _():
        x = x_ref[...]
        ms = jnp.mean(x * x, axis=-1, keepdims=True)
        h = x * lax.rsqrt(ms + EPS) * (1.0 + scale_ref[...]) + shift_ref[...]
        hb = h.astype(BF16)
        h_scr[...] = hb
        lr_ref[...] = _dot(hb, wlr_ref[...])

    @pl.when(j >= ATT_QKV_TILES)
    def _():
        w = w_ref[:, pl.ds(pl.multiple_of(j * tn, tn), tn)]
        p_ref[...] = _dot(h_scr[...], w).astype(BF16)

    @pl.when(j < ATT_QKV_TILES)
    def _():
        h = h_scr[...]
        chunk = 256
        for c0 in range(0, tn, chunk):
            acc = _dot(h, w_ref[:, pl.ds(pl.multiple_of(j * tn + c0, chunk), chunk)])
            p_ref[:, c0:c0 + chunk] = acc.astype(BF16)
            for c in range(c0 // 128, (c0 + chunk) // 128):
                lanes = slice(c * 128 - c0, (c + 1) * 128 - c0)
                src, prev_d = acc_scr[0], 1
                src[c] = acc[:, lanes]
                for lvl, (ref, d) in enumerate(zip(res_refs, RESIDUE_DILATIONS)):
                    ratio, n = d // prev_d, tm // d
                    dst = acc_scr[lvl + 1] if lvl + 1 < len(RESIDUE_DILATIONS) else None
                    for rp in range(prev_d):
                        for a in range(ratio):
                            r = rp + prev_d * a
                            rows = src[c, pl.ds(rp * (tm // prev_d) + a, n, stride=ratio), :]
                            ref[:, r * tn + c * 128:r * tn + (c + 1) * 128] = rows.astype(BF16)
                            if dst is not None:
                                dst[c, r * n:(r + 1) * n, :] = rows
                    src, prev_d = dst, d


def _inproj(x2, scale, shift, w_main, w_lr, seq):
    M, D = x2.shape
    tm, tn = 512, ATT_WIDTH
    bpt = seq // tm
    last_qkv = ATT_QKV_TILES - 1
    res_specs = [pl.BlockSpec((tm // d, d * tn), lambda i, j: (i, jnp.minimum(j, last_qkv)))
                 for d in RESIDUE_DILATIONS]
    res_shapes = [jax.ShapeDtypeStruct((M // d, ATT_QKV_TILES * d * tn), BF16)
                  for d in RESIDUE_DILATIONS]
    return pl.pallas_call(
        functools.partial(_inproj_kernel, tm=tm, tn=tn),
        grid=(M // tm, PROJ_WIDTH // tn),
        in_specs=[pl.BlockSpec((tm, D), lambda i, j: (i, 0)),
                  pl.BlockSpec((None, 1, D), lambda i, j: (i // bpt, 0, 0)),
                  pl.BlockSpec((None, 1, D), lambda i, j: (i // bpt, 0, 0)),
                  pl.BlockSpec(w_main.shape, lambda i, j: (0, 0), pipeline_mode=pl.Buffered(1)),
                  pl.BlockSpec((D, LR_PAD), lambda i, j: (0, 0))],
        out_specs=[pl.BlockSpec((tm, tn), lambda i, j: (i, j)),
                   pl.BlockSpec((tm, LR_PAD), lambda i, j: (i, 0))] + res_specs,
        out_shape=[jax.ShapeDtypeStruct((M, PROJ_WIDTH), BF16),
                   jax.ShapeDtypeStruct((M, LR_PAD), F32)] + res_shapes,
        scratch_shapes=[pltpu.VMEM((tm, D), BF16)]
                       + [pltpu.VMEM((tn // 128, tm, 128), F32) for _ in RESIDUE_DILATIONS],
        compiler_params=_params(("arbitrary", "arbitrary")),
        name="inproj",
    )(x2, scale, shift, w_main, w_lr)


def _t5_bucket_np(rel):
    nb = REL_BUCKETS // 2
    max_exact = nb // 2
    n = np.abs(rel)
    large = max_exact + (np.log(np.maximum(n, 1) / max_exact)
                         / np.log(REL_MAX_DIST / max_exact) * (nb - max_exact)).astype(np.int32)
    large = np.minimum(large, nb - 1)
    return (np.where(rel > 0, nb, 0) + np.where(n < max_exact, n, large)).astype(np.int32)


def _bias_kernel(rbt_ref, bucket_ref, mask_ref, o_ref):
    rbt = rbt_ref[...]
    bucket = bucket_ref[...]
    ids = lax.broadcasted_iota(jnp.int32, (REL_BUCKETS, bucket.shape[1]), 0)
    onehot = jnp.where(ids == bucket, 1.0, 0.0).astype(BF16)
    hi = rbt.astype(BF16)
    rest = rbt - hi.astype(F32)
    mid = rest.astype(BF16)
    lo = (rest - mid.astype(F32)).astype(BF16)
    tbl = _dot(hi, onehot) + _dot(mid, onehot) + _dot(lo, onehot)
    for v in range(3):
        o_ref[v] = jnp.where(mask_ref[v] > 0.5, tbl, NEG_INF)


def _bias_tiles(rel_bias, dilation):
    w, tq, tk = ATT_STEPS, ATT_TQ, ATT_TK
    qi = np.arange(tq)[:, None]
    kj = np.arange(tk)[None, :]
    step = kj - w - qi
    band = np.abs(step) <= w
    bucket = _t5_bucket_np(step * dilation).reshape(1, tq * tk)
    masks = np.stack([band & (kj >= w), band, band & (kj < tk - w)]).astype(np.float32)
    masks = masks.reshape(3, 1, tq * tk)
    out = pl.pallas_call(
        _bias_kernel,
        out_shape=jax.ShapeDtypeStruct((3, ATT_HEADS, tq * tk), F32),
        compiler_params=pltpu.CompilerParams(vmem_limit_bytes=VMEM_LIMIT),
        name=f"bias_d{dilation}",
    )(rel_bias.T, jnp.asarray(bucket), jnp.asarray(masks))
    return out.reshape(3, ATT_HEADS, tq, tk)


def _attn_kernel(q_ref, kp_ref, km_ref, kn_ref, vp_ref, vm_ref, vn_ref, bias_ref,
                 o_ref, m_ref, den_ref, kbuf, v_slot, s_scr, p_scr, inv_scr, *, tb):
    w, tq, tk = ATT_STEPS, ATT_TQ, ATT_TK
    nsub = tb // tq
    npair = ATT_HEADS // 2
    gh = ATT_GROUP_HEADS
    gw = gh * ATT_HEAD_DIM
    ngroup = ATT_HEADS // gh
    nk = tb + 2 * w
    i = pl.program_id(2)
    first = i == 0
    last = i == pl.num_programs(2) - 1

    lane = lax.broadcasted_iota(jnp.int32, (1, ATT_WIDTH), 1)
    slot = (lane % gw) // ATT_HEAD_DIM
    row = 0
    for kpart, vpart in ((kp_ref, vp_ref), (km_ref, vm_ref), (kn_ref, vn_ref)):
        rows = slice(row, row + kpart.shape[0])
        kbuf[rows] = kpart[...]
        x = vpart[...]
        for s in range(gh):
            v_slot[s, rows] = jnp.where(slot == s, x, jnp.zeros_like(x))
        row += kpart.shape[0]

    half = tq // 2
    lower_half = lax.broadcasted_iota(jnp.int32, (half, 128), 1) < ATT_HEAD_DIM
    lower_q = lax.broadcasted_iota(jnp.int32, (tq, 128), 1) < ATT_HEAD_DIM

    def sub(j, carry):
        qs = pl.multiple_of(j * tq, tq)
        var = jnp.where(jnp.logical_and(first, j == 0), 0,
                        jnp.where(jnp.logical_and(last, j == nsub - 1), 2, 1))
        for hp in range(npair):
            cp = slice(hp * 128, (hp + 1) * 128)
            q = q_ref[pl.ds(qs, tq), cp]
            zero = jnp.zeros_like(q)
            q2 = jnp.concatenate([jnp.where(lower_q, q, zero), jnp.where(lower_q, zero, q)], axis=0)
            s2 = _dot_nt(q2, kbuf[pl.ds(qs, tk), cp])
            s_scr[2 * hp] = s2[:tq]
            s_scr[2 * hp + 1] = s2[tq:]
        m_ref[pl.ds(qs, tq), :] = jnp.zeros((tq, 128), F32)
        den_ref[pl.ds(qs, tq), :] = jnp.ones((tq, 128), F32)
        for hp in range(npair):
            for r0 in (0, half):
                dens = []
                for h in (2 * hp, 2 * hp + 1):
                    s = s_scr[h, r0:r0 + half, :] + bias_ref[var, h, r0:r0 + half, :]
                    m = jnp.max(s, axis=-1, keepdims=True)
                    p = jnp.exp(s - m)
                    den = jnp.sum(p, axis=-1, keepdims=True)
                    p_scr[h // gh, r0:r0 + half, (h % gh) * tk:(h % gh + 1) * tk] = p.astype(BF16)
                    m_ref[pl.ds(qs + r0, half), h:h + 1] = m
                    den_ref[pl.ds(qs + r0, half), h:h + 1] = den
                    dens.append(den)
                inv_scr[(2 * hp) // gh, r0:r0 + half, (hp % (gh // 2)) * 128:(hp % (gh // 2) + 1) * 128] = (
                    1.0 / jnp.where(lower_half, dens[0], dens[1]))
        for g in range(ngroup):
            cg = slice(g * gw, (g + 1) * gw)
            v_stack = jnp.concatenate([v_slot[s, pl.ds(qs, tk), cg] for s in range(gh)], axis=0)
            o = _dot(p_scr[g], v_stack)
            o_ref[pl.ds(qs, tq), cg] = (o * inv_scr[g]).astype(BF16)
        return carry

    lax.fori_loop(0, nsub, sub, 0)


def _attn_pattern(qkv, bias, batch, seq, dilation):
    w = ATT_STEPS
    L = seq // dilation
    tb = min(1024, L)
    nblk = L // tb
    hb = tb // w
    nhalo = L // w

    def main(j):
        return pl.BlockSpec((None, tb, ATT_WIDTH), lambda b, r, i: (b, i, j * dilation + r))

    def prev(j):
        return pl.BlockSpec((None, w, ATT_WIDTH),
                            lambda b, r, i: (b, jnp.maximum(i * hb - 1, 0), j * dilation + r))

    def nxt(j):
        return pl.BlockSpec((None, w, ATT_WIDTH),
                            lambda b, r, i: (b, jnp.minimum((i + 1) * hb, nhalo - 1), j * dilation + r))

    nk, gh = tb + 2 * w, ATT_GROUP_HEADS
    return pl.pallas_call(
        functools.partial(_attn_kernel, tb=tb),
        grid=(batch, dilation, nblk),
        in_specs=[main(0), prev(1), main(1), nxt(1), prev(2), main(2), nxt(2),
                  pl.BlockSpec((3, ATT_HEADS, ATT_TQ, ATT_TK), lambda b, r, i: (0, 0, 0, 0),
                               pipeline_mode=pl.Buffered(1))],
        out_specs=[pl.BlockSpec((None, tb, ATT_WIDTH), lambda b, r, i: (b, i, r)),
                   pl.BlockSpec((None, tb, 128), lambda b, r, i: (b, i, r)),
                   pl.BlockSpec((None, tb, 128), lambda b, r, i: (b, i, r))],
        out_shape=[jax.ShapeDtypeStruct((batch, L, dilation * ATT_WIDTH), BF16),
                   jax.ShapeDtypeStruct((batch, L, dilation * 128), F32),
                   jax.ShapeDtypeStruct((batch, L, dilation * 128), F32)],
        scratch_shapes=[pltpu.VMEM((nk, ATT_WIDTH), BF16),
                        pltpu.VMEM((gh, nk, ATT_WIDTH), BF16),
                        pltpu.VMEM((ATT_HEADS, ATT_TQ, ATT_TK), F32),
                        pltpu.VMEM((ATT_HEADS // gh, ATT_TQ, gh * ATT_TK), BF16),
                        pltpu.VMEM((ATT_HEADS // gh, ATT_TQ, gh * ATT_HEAD_DIM), F32)],
        compiler_params=_params(("arbitrary", "arbitrary", "arbitrary")),
        name=f"attn_d{dilation}",
    )(qkv, qkv, qkv, qkv, qkv, qkv, qkv, bias)


def _merge_kernel(*refs, tm):
    nres = len(RESIDUE_DILATIONS)
    o1_ref, m1_ref, d1_ref = refs[:3]
    res = [refs[3 + 3 * n:6 + 3 * n] for n in range(nres)]
    ag_ref, e_ref, out_ref = refs[3 + 3 * nres:6 + 3 * nres]
    scr = [refs[6 + 3 * nres + 3 * n:9 + 3 * nres + 3 * n] for n in range(nres)]

    ncol = ATT_WIDTH // 128
    for d, (o_ref, m_ref, d_ref), (so, sm, sd) in zip(RESIDUE_DILATIONS, res, scr):
        for r in range(d):
            for c in range(ncol):
                col = r * ATT_WIDTH + c * 128
                so[c, pl.ds(r, tm // d, stride=d), :] = o_ref[:, col:col + 128].astype(F32)
            sm[pl.ds(r, tm // d, stride=d), :] = m_ref[:, r * 128:(r + 1) * 128]
            sd[pl.ds(r, tm // d, stride=d), :] = d_ref[:, r * 128:(r + 1) * 128]

    rc, gw = 128, 256
    head_lane = lax.broadcasted_iota(jnp.int32, (rc, 128), 1) < ATT_HEADS
    for r0 in range(0, tm, rc):
        rows = slice(r0, r0 + rc)
        lses = ([m1_ref[rows, :] + jnp.log(d1_ref[rows, :])]
                + [sm[rows, :] + jnp.log(sd[rows, :]) for _, sm, sd in scr])
        mx = functools.reduce(jnp.maximum, lses)
        es = [jnp.exp(l - mx) for l in lses]
        inv = 1.0 / functools.reduce(jnp.add, es)
        packed = []
        for ei in es:
            wgt = jnp.where(head_lane, ei * inv, 0.0)
            hi = wgt.astype(BF16).astype(F32)
            lo = (wgt - hi).astype(BF16).astype(F32)
            packed.append((hi + pltpu.roll(lo, ATT_HEADS, 1)).astype(BF16))
        for c0 in range(0, ATT_WIDTH, gw):
            cols = slice(c0, c0 + gw)
            e = e_ref[:, cols]
            outs = [o1_ref[rows, cols].astype(F32)] + [
                jnp.concatenate([so[c, rows, :] for c in range(c0 // 128, (c0 + gw) // 128)], axis=1)
                for so, _, _ in scr]
            att = functools.reduce(jnp.add, [_dot(w, e) * o for w, o in zip(packed, outs)])
            out_ref[rows, cols] = (att * _silu(ag_ref[rows, cols].astype(F32))).astype(BF16)


def _merge(nat, res, proj):
    M = proj.shape[0]
    tm = 512
    expand = np.zeros((128, ATT_WIDTH), np.float32)
    for h in range(ATT_HEADS):
        expand[h, h * ATT_HEAD_DIM:(h + 1) * ATT_HEAD_DIM] = 1.0
        expand[ATT_HEADS + h, h * ATT_HEAD_DIM:(h + 1) * ATT_HEAD_DIM] = 1.0
    row = lambda rows, width: pl.BlockSpec((rows, width), lambda i: (i, 0))
    triple = lambda d: [row(tm // d, d * ATT_WIDTH), row(tm // d, d * 128), row(tm // d, d * 128)]
    in_specs = (triple(1) + [s for d in RESIDUE_DILATIONS for s in triple(d)]
                + [pl.BlockSpec((tm, ATT_WIDTH), lambda i: (i, COL_AG // ATT_WIDTH)),
                   pl.BlockSpec((128, ATT_WIDTH), lambda i: (0, 0))])
    scratch = []
    for _ in RESIDUE_DILATIONS:
        scratch += [pltpu.VMEM((ATT_WIDTH // 128, tm, 128), F32),
                    pltpu.VMEM((tm, 128), F32), pltpu.VMEM((tm, 128), F32)]
    return pl.pallas_call(
        functools.partial(_merge_kernel, tm=tm),
        grid=(M // tm,),
        in_specs=in_specs,
        out_specs=row(tm, ATT_WIDTH),
        out_shape=jax.ShapeDtypeStruct((M, ATT_WIDTH), BF16),
        scratch_shapes=scratch,
        compiler_params=_params(("arbitrary",)),
        name="merge",
    )(*nat, *[a for t in res for a in t], proj, jnp.asarray(expand, BF16))


def _gla_kernel(*refs, ts, nstep):
    fwd_in, bwd_in = refs[:5], refs[5:10]
    (upf_ref, upb_ref, gbf_ref, gbb_ref, gain_ref, out_ref,
     state, o_acc, qf_scr, kd_scr, ks_scr, oin_scr, st_scr) = refs[10:]
    C = GLA_CHUNK
    nchunk = ts // C
    chunks = [slice(c * C, (c + 1) * C) for c in range(nchunk)]
    i = pl.program_id(2)
    dirs = [(0, False, fwd_in, upf_ref, gbf_ref), (1, True, bwd_in, upb_ref, gbb_ref)]

    @pl.when(i == 0)
    def _():
        state[...] = jnp.zeros_like(state)


    log_gs = []
    for d, reverse, (q_ref, k_ref, v_ref, lr_ref, gg_ref), up_ref, gb_ref in dirs:
        lr_hi, lr_lo = _split_bf16(lr_ref[...])
        up_hi, up_lo = _split_bf16(up_ref[...])
        z = _dot(lr_hi, up_hi) + _dot(lr_hi, up_lo) + _dot(lr_lo, up_hi) + gb_ref[...]
        log_gs.append((jnp.minimum(z, 0.0) - jnp.log(1.0 + jnp.exp(-jnp.abs(z))))
                      * (1.0 / GLA_GATE_NORM))

    row = lax.broadcasted_iota(jnp.int32, (C, GLA_DK), 0)
    dec_cols = [[], []]
    for d, reverse, (q_ref, k_ref, v_ref, lr_ref, gg_ref), up_ref, gb_ref in dirs:
        for rows in chunks:
            b = log_gs[d][rows]
            s = 1
            while s < C:
                if reverse:
                    b = b + jnp.where(row < C - s, pltpu.roll(b, C - s, 0), 0.0)
                else:
                    b = b + jnp.where(row >= s, pltpu.roll(b, s, 0), 0.0)
                s *= 2
            b_edge = b[0:1] if reverse else b[C - 1:C]
            q = q_ref[rows, :].astype(F32)
            k = k_ref[rows, :].astype(F32)
            qf_scr[d, rows, :] = (q * jnp.exp(b) * (GLA_DK ** -0.5)).astype(BF16)
            kd_scr[d, rows, :] = (k * jnp.exp(-b)).astype(BF16)
            ks_scr[d, rows, :] = (k * jnp.exp(b_edge - b)).astype(BF16)
            dec = jnp.broadcast_to(jnp.exp(b_edge), (GLA_DK, GLA_DK)).T
            dec_cols[d].append(jnp.concatenate([dec, dec], axis=1))

    ri = lax.broadcasted_iota(jnp.int32, (C, C), 0)
    ci = lax.broadcasted_iota(jnp.int32, (C, C), 1)
    atts = [[jnp.where((ci >= ri) if reverse else (ci <= ri),
                       _dot_nt(qf_scr[d, rows, :], kd_scr[d, rows, :]), 0.0).astype(BF16)
             for rows in chunks] for d, reverse, *_ in dirs]

    kvs = [[], []]
    for d, reverse, (q_ref, k_ref, v_ref, lr_ref, gg_ref), up_ref, gb_ref in dirs:
        for rows, att in zip(chunks, atts[d]):
            v = v_ref[rows, :]
            oin_scr[d, rows, :] = _dot(att, v)
            kvs[d].append(_dot_tn(ks_scr[d, rows, :], v))

    orders = [list(range(nchunk)), list(range(nchunk - 1, -1, -1))]
    for d, reverse, *_ in dirs:
        st = state[d]
        for c in orders[d]:
            st_scr[d, c] = st.astype(BF16)
            st = st * dec_cols[d][c] + kvs[d][c]
        state[d] = st

    for d, reverse, *_ in dirs:
        for c in orders[d]:
            rows = chunks[c]
            oin_scr[d, rows, :] = oin_scr[d, rows, :] + _dot(qf_scr[d, rows, :], st_scr[d, c])

    blocks = [i, nstep - 1 - i]

    @pl.when(i < nstep // 2)
    def _():
        for d, reverse, *_ in dirs:
            base = pl.multiple_of(blocks[d] * ts, ts)
            for rows in chunks:
                o_acc[pl.ds(base + rows.start, C), :] = oin_scr[d, rows, :]

    @pl.when(i >= nstep // 2)
    def _():
        for d, reverse, (q_ref, k_ref, v_ref, lr_ref, gg_ref), up_ref, gb_ref in dirs:
            base = pl.multiple_of(blocks[d] * ts, ts)
            for rows in chunks:
                dst = pl.ds(base + rows.start, C)
                tot = oin_scr[d, rows, :] + o_acc[dst, :]
                ms = jnp.mean(tot * tot, axis=-1, keepdims=True)
                g_o = tot * lax.rsqrt(ms + EPS) * gain_ref[...]
                out_ref[dst, :] = (g_o * _silu(gg_ref[rows, :].astype(F32))).astype(BF16)


def _gla(proj, lr, up_f, up_b, bias_f, bias_b, gain, batch, seq):
    ts = 512
    nstep = seq // ts
    assert nstep % 2 == 0
    C = GLA_CHUNK
    p3 = proj.reshape(batch, seq, PROJ_WIDTH)
    lr3 = lr.reshape(batch, seq, LR_PAD)

    def direction_specs(step):
        def seq_block(width, col0):
            return pl.BlockSpec((None, ts, width), lambda b, h, i: (b, step(i), col0 // width + h))
        return [seq_block(GLA_DK, COL_GQ), seq_block(GLA_DK, COL_GK), seq_block(GLA_DV, COL_GV),
                pl.BlockSpec((None, ts, LR_PAD), lambda b, h, i: (b, step(i), 0)),
                seq_block(GLA_DV, COL_GG)]

    per_head = lambda rows, width: pl.BlockSpec((rows, width), lambda b, h, i: (0, h))
    in_specs = (direction_specs(lambda i: i) + direction_specs(lambda i: nstep - 1 - i)
                + [per_head(LR_PAD, GLA_DK), per_head(LR_PAD, GLA_DK),
                   per_head(1, GLA_DK), per_head(1, GLA_DK), per_head(1, GLA_DV)])
    dir_args = [p3, p3, p3, lr3, p3]
    return pl.pallas_call(
        functools.partial(_gla_kernel, ts=ts, nstep=nstep),
        grid=(batch, GLA_HEADS, nstep),
        in_specs=in_specs,
        out_specs=pl.BlockSpec((None, seq, GLA_DV), lambda b, h, i: (b, 0, h)),
        out_shape=jax.ShapeDtypeStruct((batch, seq, GLA_WIDTH), BF16),
        scratch_shapes=[pltpu.VMEM((2, GLA_DK, GLA_DV), F32),
                        pltpu.VMEM((seq, GLA_DV), F32),
                        pltpu.VMEM((2, ts, GLA_DK), BF16),
                        pltpu.VMEM((2, ts, GLA_DK), BF16),
                        pltpu.VMEM((2, ts, GLA_DK), BF16),
                        pltpu.VMEM((2, ts, GLA_DV), F32),
                        pltpu.VMEM((2, ts // C, GLA_DK, GLA_DV), BF16)],
        compiler_params=_params(("arbitrary", "arbitrary", "arbitrary")),
        name="gla",
    )(*dir_args, *dir_args, up_f, up_b, bias_f.reshape(1, GLA_KEY_WIDTH),
      bias_b.reshape(1, GLA_KEY_WIDTH), gain.reshape(1, GLA_WIDTH))


def _outproj_kernel(a_ref, g_ref, wa_ref, wg_ref, x_ref, gate_ref, fg_ref, o_ref, *, final):
    y = _dot(a_ref[...], wa_ref[...]) + _dot(g_ref[...], wg_ref[...])
    xn = x_ref[...] + gate_ref[...] * y
    if final:
        ms = jnp.mean(xn * xn, axis=-1, keepdims=True)
        xn = xn * lax.rsqrt(ms + EPS) * fg_ref[...]
    o_ref[...] = xn


def _outproj(a_out, g_out, w_out_bf16, x2, gate, final_gain, seq, final):
    M, D = x2.shape
    tm = 256
    bpt = seq // tm
    return pl.pallas_call(
        functools.partial(_outproj_kernel, final=final),
        grid=(M // tm,),
        in_specs=[pl.BlockSpec((tm, ATT_WIDTH), lambda i: (i, 0)),
                  pl.BlockSpec((tm, GLA_WIDTH), lambda i: (i, 0)),
                  pl.BlockSpec((ATT_WIDTH, D), lambda i: (0, 0)),
                  pl.BlockSpec((GLA_WIDTH, D), lambda i: (1, 0)),
                  pl.BlockSpec((tm, D), lambda i: (i, 0)),
                  pl.BlockSpec((None, 1, D), lambda i: (i // bpt, 0, 0)),
                  pl.BlockSpec((1, D), lambda i: (0, 0))],
        out_specs=pl.BlockSpec((tm, D), lambda i: (i, 0)),
        out_shape=jax.ShapeDtypeStruct((M, D), F32),
        compiler_params=_params(("arbitrary",)),
        name="outproj",
    )(a_out, g_out, w_out_bf16, w_out_bf16, x2, gate, final_gain.reshape(1, D))


def kernel(x, c, w_cond, b_cond, w_in, gla_gate_up_fwd, gla_gate_bias_fwd, gla_gate_up_bwd,
           gla_gate_bias_bwd, gla_norm_gain, rel_bias, w_out, final_gain):
    B, S, D = x.shape
    depth = w_cond.shape[0]
    R = GLA_GATE_RANK
    xs = x.reshape(B * S, D)
    for layer in range(depth):
        mod = _mod(c, w_cond[layer], b_cond[layer])
        shift, scale, gate = [m.reshape(B, 1, D) for m in jnp.split(mod, 3, axis=-1)]

        col_scale = jnp.where(jnp.arange(w_in.shape[-1]) < ATT_WIDTH, ATT_HEAD_DIM ** -0.5, 1.0)
        w_main = (w_in[layer] * col_scale).astype(BF16)
        w_lr = jnp.pad(w_main[:, PROJ_WIDTH:], ((0, 0), (0, LR_PAD - 2 * R)))
        proj, lr, *res_qkv = _inproj(xs, scale, shift, w_main, w_lr, S)

        def rows2d(t, d):
            return [a.reshape(B * S // d, -1) for a in t]

        nat = rows2d(_attn_pattern(proj.reshape(B, S, PROJ_WIDTH), _bias_tiles(rel_bias, 1), B, S, 1), 1)
        res = [rows2d(_attn_pattern(qkv.reshape(B, S // d, -1), _bias_tiles(rel_bias, d), B, S, d), d)
               for d, qkv in zip(RESIDUE_DILATIONS, res_qkv)]
        a_out = _merge(nat, res, proj)

        up_f = jnp.pad(gla_gate_up_fwd[layer], ((0, LR_PAD - R), (0, 0)))
        up_b = jnp.pad(gla_gate_up_bwd[layer], ((R, LR_PAD - 2 * R), (0, 0)))
        g_out = _gla(proj, lr, up_f, up_b, gla_gate_bias_fwd[layer], gla_gate_bias_bwd[layer],
                     gla_norm_gain[layer], B, S)

        xs = _outproj(a_out, g_out.reshape(B * S, GLA_WIDTH), w_out[layer].astype(BF16),
                      xs, gate, final_gain, S, final=layer == depth - 1)
    return xs.reshape(B, S, D)
```

```python
import functools
import math

import jax
import jax.numpy as jnp
import numpy as np
from jax import lax
from jax.experimental import pallas as pl
from jax.experimental.pallas import tpu as pltpu

D_MODEL = 2048
ATT_WIDTH = 1024
ATT_HEADS = 16
ATT_HEAD_DIM = 64
DILATED_PATTERNS = ((128, 1), (512, 4), (2048, 16))
ATT_STEPS = 64
GLA_WIDTH = 1024
GLA_HEADS = 4
GLA_KEY_WIDTH = 512
GLA_DK = 128
GLA_DV = 256
GLA_GATE_RANK = 16
GLA_GATE_NORM = 16.0
GLA_CHUNK = 64
REL_BUCKETS = 32
REL_MAX_DIST = 1024
EPS = 1e-6
NEG_INF = -1e30

PROJ_WIDTH = 4 * ATT_WIDTH + 2 * GLA_KEY_WIDTH + 2 * GLA_WIDTH
COL_AQ, COL_AK, COL_AV, COL_AG = 0, 1024, 2048, 3072
COL_GQ, COL_GK, COL_GV, COL_GG = 4096, 4608, 5120, 6144
LR_PAD = 128
ATT_QKV_TILES = 3
RESIDUE_DILATIONS = tuple(d for _, d in DILATED_PATTERNS if d > 1)
ATT_TQ = 128
ATT_TK = ATT_TQ + 2 * ATT_STEPS
ATT_GROUP_HEADS = 4
ATT_HEAD_SETS = 1

VMEM_LIMIT = 56 * 1024 * 1024

BF16 = jnp.bfloat16
F32 = jnp.float32


def _params(sem):
    return pltpu.CompilerParams(dimension_semantics=sem, vmem_limit_bytes=VMEM_LIMIT)


def _dot(a, b):
    return jnp.dot(a, b, preferred_element_type=F32)


def _dot_nt(a, b):
    return lax.dot_general(a, b, (((1,), (1,)), ((), ())), preferred_element_type=F32)


def _dot_tn(a, b):
    return lax.dot_general(a, b, (((0,), (0,)), ((), ())), preferred_element_type=F32)


def _split_bf16(x):
    hi = x.astype(BF16)
    lo = (x - hi.astype(F32)).astype(BF16)
    return hi, lo


def _silu(x):
    return x / (1.0 + jnp.exp(-x))


def _mod_kernel(c_ref, w_ref, b_ref, o_ref):
    s = _silu(c_ref[...])
    o_ref[...] = jnp.dot(s, w_ref[...], preferred_element_type=F32,
                         precision=lax.Precision.HIGHEST) + b_ref[...]


def _mod(c, w_cond, b_cond):
    B, D = c.shape
    N = w_cond.shape[1]
    tn = 768
    cp = jnp.pad(c, ((0, 8 - B), (0, 0)))
    out = pl.pallas_call(
        _mod_kernel,
        grid=(N // tn,),
        in_specs=[pl.BlockSpec((8, D), lambda j: (0, 0)),
                  pl.BlockSpec((D, tn), lambda j: (0, j)),
                  pl.BlockSpec((1, tn), lambda j: (0, j))],
        out_specs=pl.BlockSpec((8, tn), lambda j: (0, j)),
        out_shape=jax.ShapeDtypeStruct((8, N), F32),
        compiler_params=_params(("arbitrary",)),
        name="mod",
    )(cp, w_cond, b_cond.reshape(1, N))
    return out[:B]


def _inproj_kernel(x_ref, scale_ref, shift_ref, w_ref, p_ref, lr_ref, *rest, tm, tn):
    nres = len(RESIDUE_DILATIONS)
    res_refs, h_scr, acc_scr = rest[:nres], rest[nres], rest[nres + 1:]
    j = pl.program_id(1)

    @pl.when(j == 0)
    def _():
        x = x_ref[...]
        ms = jnp.mean(x * x, axis=-1, keepdims=True)
        h = x * lax.rsqrt(ms + EPS) * (1.0 + scale_ref[...]) + shift_ref[...]
        hb = h.astype(BF16)
        h_scr[...] = hb
        lr_ref[...] = _dot(hb, w_ref[:, PROJ_WIDTH:PROJ_WIDTH + LR_PAD])

    @pl.when(j >= ATT_QKV_TILES)
    def _():
        w = w_ref[:, pl.ds(pl.multiple_of(j * tn, tn), tn)]
        p_ref[...] = _dot(h_scr[...], w).astype(BF16)

    @pl.when(j < ATT_QKV_TILES)
    def _():
        h = h_scr[...]
        chunk = 256
        for c0 in range(0, tn, chunk):
            acc = _dot(h, w_ref[:, pl.ds(pl.multiple_of(j * tn + c0, chunk), chunk)])
            p_ref[:, c0:c0 + chunk] = acc.astype(BF16)
            for c in range(c0 // 128, (c0 + chunk) // 128):
                lanes = slice(c * 128 - c0, (c + 1) * 128 - c0)
                src, prev_d = acc_scr[0], 1
                src[c] = acc[:, lanes]
                for lvl, (ref, d) in enumerate(zip(res_refs, RESIDUE_DILATIONS)):
                    ratio, n = d // prev_d, tm // d
                    dst = acc_scr[lvl + 1] if lvl + 1 < len(RESIDUE_DILATIONS) else None
                    for rp in range(prev_d):
                        for a in range(ratio):
                            r = rp + prev_d * a
                            rows = src[c, pl.ds(rp * (tm // prev_d) + a, n, stride=ratio), :]
                            ref[:, r * tn + c * 128:r * tn + (c + 1) * 128] = rows.astype(BF16)
                            if dst is not None:
                                dst[c, r * n:(r + 1) * n, :] = rows
                    src, prev_d = dst, d


def _inproj(x2, scale, shift, w_main, seq):
    M, D = x2.shape
    tm, tn = 512, ATT_WIDTH
    bpt = seq // tm
    last_qkv = ATT_QKV_TILES - 1
    res_specs = [pl.BlockSpec((tm // d, d * tn), lambda i, j: (i, jnp.minimum(j, last_qkv)))
                 for d in RESIDUE_DILATIONS]
    res_shapes = [jax.ShapeDtypeStruct((M // d, ATT_QKV_TILES * d * tn), BF16)
                  for d in RESIDUE_DILATIONS]
    return pl.pallas_call(
        functools.partial(_inproj_kernel, tm=tm, tn=tn),
        grid=(M // tm, PROJ_WIDTH // tn),
        in_specs=[pl.BlockSpec((tm, D), lambda i, j: (i, 0)),
                  pl.BlockSpec((None, 1, D), lambda i, j: (i // bpt, 0, 0)),
                  pl.BlockSpec((None, 1, D), lambda i, j: (i // bpt, 0, 0)),
                  pl.BlockSpec(w_main.shape, lambda i, j: (0, 0), pipeline_mode=pl.Buffered(1))],
        out_specs=[pl.BlockSpec((tm, tn), lambda i, j: (i, j)),
                   pl.BlockSpec((tm, LR_PAD), lambda i, j: (i, 0))] + res_specs,
        out_shape=[jax.ShapeDtypeStruct((M, PROJ_WIDTH), BF16),
                   jax.ShapeDtypeStruct((M, LR_PAD), F32)] + res_shapes,
        scratch_shapes=[pltpu.VMEM((tm, D), BF16)]
                       + [pltpu.VMEM((tn // 128, tm, 128), F32) for _ in RESIDUE_DILATIONS],
        compiler_params=_params(("arbitrary", "arbitrary")),
        name="inproj",
    )(x2, scale, shift, w_main)


def _t5_bucket_np(rel):
    nb = REL_BUCKETS // 2
    max_exact = nb // 2
    n = np.abs(rel)
    large = max_exact + (np.log(np.maximum(n, 1) / max_exact)
                         / np.log(REL_MAX_DIST / max_exact) * (nb - max_exact)).astype(np.int32)
    large = np.minimum(large, nb - 1)
    return (np.where(rel > 0, nb, 0) + np.where(n < max_exact, n, large)).astype(np.int32)


def _bias_kernel(rbt_ref, bucket_ref, mask_ref, o_ref):
    rbt = rbt_ref[...]
    bucket = bucket_ref[...]
    ids = lax.broadcasted_iota(jnp.int32, (REL_BUCKETS, bucket.shape[1]), 0)
    onehot = jnp.where(ids == bucket, 1.0, 0.0).astype(BF16)
    hi = rbt.astype(BF16)
    rest = rbt - hi.astype(F32)
    mid = rest.astype(BF16)
    lo = (rest - mid.astype(F32)).astype(BF16)
    tbl = _dot(hi, onehot) + _dot(mid, onehot) + _dot(lo, onehot)
    for v in range(3):
        o_ref[v] = jnp.where(mask_ref[v] > 0.5, tbl, NEG_INF)


def _bias_tiles(rel_bias, dilation):
    w, tq, tk = ATT_STEPS, ATT_TQ, ATT_TK
    qi = np.arange(tq)[:, None]
    kj = np.arange(tk)[None, :]
    step = kj - w - qi
    band = np.abs(step) <= w
    bucket = _t5_bucket_np(step * dilation).reshape(1, tq * tk)
    masks = np.stack([band & (kj >= w), band, band & (kj < tk - w)]).astype(np.float32)
    masks = masks.reshape(3, 1, tq * tk)
    out = pl.pallas_call(
        _bias_kernel,
        out_shape=jax.ShapeDtypeStruct((3, ATT_HEADS, tq * tk), F32),
        compiler_params=pltpu.CompilerParams(vmem_limit_bytes=VMEM_LIMIT),
        name=f"bias_d{dilation}",
    )(rel_bias.T, jnp.asarray(bucket), jnp.asarray(masks))
    return out.reshape(3, ATT_HEADS, tq, tk)


def _attn_kernel(q_ref, kp_ref, km_ref, kn_ref, vp_ref, vm_ref, vn_ref, bias_ref,
                 o_ref, m_ref, den_ref, kbuf, v_slot, s_scr, p_scr, inv_scr, *, tb):
    w, tq, tk = ATT_STEPS, ATT_TQ, ATT_TK
    nsub = tb // tq
    npair = ATT_HEADS // 2
    gh = ATT_GROUP_HEADS
    gw = gh * ATT_HEAD_DIM
    ngroup = ATT_HEADS // gh
    nk = tb + 2 * w
    i = pl.program_id(2)
    first = i == 0
    last = i == pl.num_programs(2) - 1

    lane = lax.broadcasted_iota(jnp.int32, (1, ATT_WIDTH), 1)
    slot = (lane % gw) // ATT_HEAD_DIM
    row = 0
    for kpart, vpart in ((kp_ref, vp_ref), (km_ref, vm_ref), (kn_ref, vn_ref)):
        rows = slice(row, row + kpart.shape[0])
        kbuf[rows] = kpart[...]
        x = vpart[...]
        for s in range(gh):
            v_slot[s, rows] = jnp.where(slot == s, x, jnp.zeros_like(x))
        row += kpart.shape[0]

    half = tq // 2
    lower_half = lax.broadcasted_iota(jnp.int32, (half, 128), 1) < ATT_HEAD_DIM
    lower_q = lax.broadcasted_iota(jnp.int32, (tq, 128), 1) < ATT_HEAD_DIM

    nsets = ATT_HEAD_SETS

    def head_pairs(hs):
        return range(hs * npair // nsets, (hs + 1) * npair // nsets)

    def tile(j):
        return pl.multiple_of(j * tq, tq), 0

    def logits(j, hs):
        qs, buf = tile(j)
        for hp in head_pairs(hs):
            cp = slice(hp * 128, (hp + 1) * 128)
            q = q_ref[pl.ds(qs, tq), cp]
            zero = jnp.zeros_like(q)
            q2 = jnp.concatenate([jnp.where(lower_q, q, zero), jnp.where(lower_q, zero, q)], axis=0)
            s2 = _dot_nt(q2, kbuf[pl.ds(qs, tk), cp])
            s_scr[buf, 2 * hp] = s2[:tq]
            s_scr[buf, 2 * hp + 1] = s2[tq:]

    def softmax(j, hs):
        qs, buf = tile(j)
        var = jnp.where(jnp.logical_and(first, j == 0), 0,
                        jnp.where(jnp.logical_and(last, j == nsub - 1), 2, 1))
        if hs == 0:
            m_ref[pl.ds(qs, tq), :] = jnp.zeros((tq, 128), F32)
            den_ref[pl.ds(qs, tq), :] = jnp.ones((tq, 128), F32)
        for hp in head_pairs(hs):
            for r0 in (0, half):
                dens = []
                for h in (2 * hp, 2 * hp + 1):
                    s = s_scr[buf, h, r0:r0 + half, :] + bias_ref[var, h, r0:r0 + half, :]
                    m = jnp.max(s, axis=-1, keepdims=True)
                    p = jnp.exp(s - m)
                    den = jnp.sum(p, axis=-1, keepdims=True)
                    p_scr[buf, h // gh, r0:r0 + half, (h % gh) * tk:(h % gh + 1) * tk] = p.astype(BF16)
                    m_ref[pl.ds(qs + r0, half), h:h + 1] = m
                    den_ref[pl.ds(qs + r0, half), h:h + 1] = den
                    dens.append(den)
                pair = hp % (gh // 2)
                inv_scr[buf, (2 * hp) // gh, r0:r0 + half, pair * 128:(pair + 1) * 128] = (
                    1.0 / jnp.where(lower_half, dens[0], dens[1]))

    def outputs(j, hs):
        qs, buf = tile(j)
        for g in range(hs * ngroup // nsets, (hs + 1) * ngroup // nsets):
            cg = slice(g * gw, (g + 1) * gw)
            v_stack = jnp.concatenate([v_slot[s, pl.ds(qs, tk), cg] for s in range(gh)], axis=0)
            o = _dot(p_scr[buf, g], v_stack)
            o_ref[pl.ds(qs, tq), cg] = (o * inv_scr[buf, g]).astype(BF16)

    for hs in range(nsets):
        def one_tile(j, carry, hs=hs):
            logits(j, hs)
            softmax(j, hs)
            outputs(j, hs)
            return carry

        lax.fori_loop(0, nsub, one_tile, 0)


def _attn_pattern(qkv, bias, batch, seq, dilation):
    w = ATT_STEPS
    L = seq // dilation
    tb = min(1024, L)
    nblk = L // tb
    hb = tb // w
    nhalo = L // w

    def main(j):
        return pl.BlockSpec((None, tb, ATT_WIDTH), lambda b, r, i: (b, i, j * dilation + r))

    def prev(j):
        return pl.BlockSpec((None, w, ATT_WIDTH),
                            lambda b, r, i: (b, jnp.maximum(i * hb - 1, 0), j * dilation + r))

    def nxt(j):
        return pl.BlockSpec((None, w, ATT_WIDTH),
                            lambda b, r, i: (b, jnp.minimum((i + 1) * hb, nhalo - 1), j * dilation + r))

    nk, gh = tb + 2 * w, ATT_GROUP_HEADS
    return pl.pallas_call(
        functools.partial(_attn_kernel, tb=tb),
        grid=(batch, dilation, nblk),
        in_specs=[main(0), prev(1), main(1), nxt(1), prev(2), main(2), nxt(2),
                  pl.BlockSpec((3, ATT_HEADS, ATT_TQ, ATT_TK), lambda b, r, i: (0, 0, 0, 0),
                               pipeline_mode=pl.Buffered(1))],
        out_specs=[pl.BlockSpec((None, tb, ATT_WIDTH), lambda b, r, i: (b, i, r)),
                   pl.BlockSpec((None, tb, 128), lambda b, r, i: (b, i, r)),
                   pl.BlockSpec((None, tb, 128), lambda b, r, i: (b, i, r))],
        out_shape=[jax.ShapeDtypeStruct((batch, L, dilation * ATT_WIDTH), BF16),
                   jax.ShapeDtypeStruct((batch, L, dilation * 128), F32),
                   jax.ShapeDtypeStruct((batch, L, dilation * 128), F32)],
        scratch_shapes=[pltpu.VMEM((nk, ATT_WIDTH), BF16),
                        pltpu.VMEM((gh, nk, ATT_WIDTH), BF16),
                        pltpu.VMEM((1, ATT_HEADS, ATT_TQ, ATT_TK), F32),
                        pltpu.VMEM((1, ATT_HEADS // gh, ATT_TQ, gh * ATT_TK), BF16),
                        pltpu.VMEM((1, ATT_HEADS // gh, ATT_TQ, gh * ATT_HEAD_DIM), F32)],
        compiler_params=_params(("arbitrary", "arbitrary", "arbitrary")),
        name=f"attn_d{dilation}",
    )(qkv, qkv, qkv, qkv, qkv, qkv, qkv, bias)


def _merge_kernel(*refs, tm):
    nres = len(RESIDUE_DILATIONS)
    o1_ref, m1_ref, d1_ref = refs[:3]
    res = [refs[3 + 3 * n:6 + 3 * n] for n in range(nres)]
    ag_ref, e_ref, out_ref = refs[3 + 3 * nres:6 + 3 * nres]
    scr = [refs[6 + 3 * nres + 3 * n:9 + 3 * nres + 3 * n] for n in range(nres)]

    ncol = ATT_WIDTH // 128
    for d, (o_ref, m_ref, d_ref), (so, sm, sd) in zip(RESIDUE_DILATIONS, res, scr):
        for r in range(d):
            for c in range(ncol):
                col = r * ATT_WIDTH + c * 128
                so[c, pl.ds(r, tm // d, stride=d), :] = o_ref[:, col:col + 128].astype(F32)
            sm[pl.ds(r, tm // d, stride=d), :] = m_ref[:, r * 128:(r + 1) * 128]
            sd[pl.ds(r, tm // d, stride=d), :] = d_ref[:, r * 128:(r + 1) * 128]

    rc, gw = 128, 256
    head_lane = lax.broadcasted_iota(jnp.int32, (rc, 128), 1) < ATT_HEADS
    for r0 in range(0, tm, rc):
        rows = slice(r0, r0 + rc)
        lses = ([m1_ref[rows, :] + jnp.log(d1_ref[rows, :])]
                + [sm[rows, :] + jnp.log(sd[rows, :]) for _, sm, sd in scr])
        mx = functools.reduce(jnp.maximum, lses)
        es = [jnp.exp(l - mx) for l in lses]
        inv = 1.0 / functools.reduce(jnp.add, es)
        packed = []
        for ei in es:
            wgt = jnp.where(head_lane, ei * inv, 0.0)
            hi = wgt.astype(BF16).astype(F32)
            lo = (wgt - hi).astype(BF16).astype(F32)
            packed.append((hi + pltpu.roll(lo, ATT_HEADS, 1)).astype(BF16))
        for c0 in range(0, ATT_WIDTH, gw):
            cols = slice(c0, c0 + gw)
            e = e_ref[:, cols]
            outs = [o1_ref[rows, cols].astype(F32)] + [
                jnp.concatenate([so[c, rows, :] for c in range(c0 // 128, (c0 + gw) // 128)], axis=1)
                for so, _, _ in scr]
            att = functools.reduce(jnp.add, [_dot(w, e) * o for w, o in zip(packed, outs)])
            out_ref[rows, cols] = (att * _silu(ag_ref[rows, cols].astype(F32))).astype(BF16)


def _merge(nat, res, proj):
    M = proj.shape[0]
    tm = 512
    expand = np.zeros((128, ATT_WIDTH), np.float32)
    for h in range(ATT_HEADS):
        expand[h, h * ATT_HEAD_DIM:(h + 1) * ATT_HEAD_DIM] = 1.0
        expand[ATT_HEADS + h, h * ATT_HEAD_DIM:(h + 1) * ATT_HEAD_DIM] = 1.0
    row = lambda rows, width: pl.BlockSpec((rows, width), lambda i: (i, 0))
    triple = lambda d: [row(tm // d, d * ATT_WIDTH), row(tm // d, d * 128), row(tm // d, d * 128)]
    in_specs = (triple(1) + [s for d in RESIDUE_DILATIONS for s in triple(d)]
                + [pl.BlockSpec((tm, ATT_WIDTH), lambda i: (i, COL_AG // ATT_WIDTH)),
                   pl.BlockSpec((128, ATT_WIDTH), lambda i: (0, 0))])
    scratch = []
    for _ in RESIDUE_DILATIONS:
        scratch += [pltpu.VMEM((ATT_WIDTH // 128, tm, 128), F32),
                    pltpu.VMEM((tm, 128), F32), pltpu.VMEM((tm, 128), F32)]
    return pl.pallas_call(
        functools.partial(_merge_kernel, tm=tm),
        grid=(M // tm,),
        in_specs=in_specs,
        out_specs=row(tm, ATT_WIDTH),
        out_shape=jax.ShapeDtypeStruct((M, ATT_WIDTH), BF16),
        scratch_shapes=scratch,
        compiler_params=_params(("arbitrary",)),
        name="merge",
    )(*nat, *[a for t in res for a in t], proj, jnp.asarray(expand, BF16))


def _gla_kernel(*refs, ts, nstep):
    fwd_in, bwd_in = refs[:5], refs[5:10]
    (upf_ref, upb_ref, gbf_ref, gbb_ref, gain_ref, out_ref,
     state, o_acc, qf_scr, kd_scr, ks_scr, oin_scr, st_scr) = refs[10:]
    C = GLA_CHUNK
    nchunk = ts // C
    chunks = [slice(c * C, (c + 1) * C) for c in range(nchunk)]
    i = pl.program_id(2)
    dirs = [(0, False, fwd_in, upf_ref, gbf_ref), (1, True, bwd_in, upb_ref, gbb_ref)]

    @pl.when(i == 0)
    def _():
        state[...] = jnp.zeros_like(state)


    log_gs = []
    for d, reverse, (q_ref, k_ref, v_ref, lr_ref, gg_ref), up_ref, gb_ref in dirs:
        lr_hi, lr_lo = _split_bf16(lr_ref[...])
        up_hi, up_lo = _split_bf16(up_ref[...])
        z = _dot(lr_hi, up_hi) + _dot(lr_hi, up_lo) + _dot(lr_lo, up_hi) + gb_ref[...]
        log_gs.append((jnp.minimum(z, 0.0) - jnp.log(1.0 + jnp.exp(-jnp.abs(z))))
                      * (1.0 / GLA_GATE_NORM))

    ri = lax.broadcasted_iota(jnp.int32, (C, C), 0)
    ci = lax.broadcasted_iota(jnp.int32, (C, C), 1)
    dec_cols = [[], []]
    for d, reverse, (q_ref, k_ref, v_ref, lr_ref, gg_ref), up_ref, gb_ref in dirs:
        tri = jnp.where((ci >= ri) if reverse else (ci <= ri), 1.0, 0.0).astype(BF16)
        for rows in chunks:
            g_hi, g_lo = _split_bf16(log_gs[d][rows])
            cum = _dot(tri, jnp.concatenate([g_hi, g_lo], axis=1))
            b = cum[:, :GLA_DK] + cum[:, GLA_DK:]
            b_edge = b[0:1] if reverse else b[C - 1:C]
            q = q_ref[rows, :].astype(F32)
            k = k_ref[rows, :].astype(F32)
            qf_scr[d, rows, :] = (q * jnp.exp(b) * (GLA_DK ** -0.5)).astype(BF16)
            kd_scr[d, rows, :] = (k * jnp.exp(-b)).astype(BF16)
            ks_scr[d, rows, :] = (k * jnp.exp(b_edge - b)).astype(BF16)
            dec = jnp.broadcast_to(jnp.exp(b_edge), (GLA_DK, GLA_DK)).T
            dec_cols[d].append(jnp.concatenate([dec, dec], axis=1))

    atts = [[jnp.where((ci >= ri) if reverse else (ci <= ri),
                       _dot_nt(qf_scr[d, rows, :], kd_scr[d, rows, :]), 0.0).astype(BF16)
             for rows in chunks] for d, reverse, *_ in dirs]

    kvs = [[], []]
    for d, reverse, (q_ref, k_ref, v_ref, lr_ref, gg_ref), up_ref, gb_ref in dirs:
        for rows, att in zip(chunks, atts[d]):
            v = v_ref[rows, :]
            oin_scr[d, rows, :] = _dot(att, v)
            kvs[d].append(_dot_tn(ks_scr[d, rows, :], v))

    orders = [list(range(nchunk)), list(range(nchunk - 1, -1, -1))]
    for d, reverse, *_ in dirs:
        st = state[d]
        for c in orders[d]:
            st_scr[d, c] = st.astype(BF16)
            st = st * dec_cols[d][c] + kvs[d][c]
        state[d] = st

    for d, reverse, *_ in dirs:
        for c in orders[d]:
            rows = chunks[c]
            oin_scr[d, rows, :] = oin_scr[d, rows, :] + _dot(qf_scr[d, rows, :], st_scr[d, c])

    blocks = [i, nstep - 1 - i]

    @pl.when(i < nstep // 2)
    def _():
        for d, reverse, *_ in dirs:
            base = pl.multiple_of(blocks[d] * ts, ts)
            for rows in chunks:
                o_acc[pl.ds(base + rows.start, C), :] = oin_scr[d, rows, :]

    @pl.when(i >= nstep // 2)
    def _():
        for d, reverse, (q_ref, k_ref, v_ref, lr_ref, gg_ref), up_ref, gb_ref in dirs:
            base = pl.multiple_of(blocks[d] * ts, ts)
            for rows in chunks:
                dst = pl.ds(base + rows.start, C)
                tot = oin_scr[d, rows, :] + o_acc[dst, :]
                ms = jnp.mean(tot * tot, axis=-1, keepdims=True)
                g_o = tot * lax.rsqrt(ms + EPS) * gain_ref[...]
                out_ref[dst, :] = (g_o * _silu(gg_ref[rows, :].astype(F32))).astype(BF16)


def _gla(proj, lr, up_f, up_b, bias_f, bias_b, gain, batch, seq):
    ts = 1024
    nstep = seq // ts
    assert nstep % 2 == 0
    C = GLA_CHUNK
    p3 = proj.reshape(batch, seq, PROJ_WIDTH)
    lr3 = lr.reshape(batch, seq, LR_PAD)

    def direction_specs(step):
        def seq_block(width, col0):
            return pl.BlockSpec((None, ts, width), lambda b, h, i: (b, step(i), col0 // width + h))
        return [seq_block(GLA_DK, COL_GQ), seq_block(GLA_DK, COL_GK), seq_block(GLA_DV, COL_GV),
                pl.BlockSpec((None, ts, LR_PAD), lambda b, h, i: (b, step(i), 0)),
                seq_block(GLA_DV, COL_GG)]

    per_head = lambda rows, width: pl.BlockSpec((rows, width), lambda b, h, i: (0, h))
    in_specs = (direction_specs(lambda i: i) + direction_specs(lambda i: nstep - 1 - i)
                + [per_head(LR_PAD, GLA_DK), per_head(LR_PAD, GLA_DK),
                   per_head(1, GLA_DK), per_head(1, GLA_DK), per_head(1, GLA_DV)])
    dir_args = [p3, p3, p3, lr3, p3]
    return pl.pallas_call(
        functools.partial(_gla_kernel, ts=ts, nstep=nstep),
        grid=(batch, GLA_HEADS, nstep),
        in_specs=in_specs,
        out_specs=pl.BlockSpec((None, seq, GLA_DV), lambda b, h, i: (b, 0, h)),
        out_shape=jax.ShapeDtypeStruct((batch, seq, GLA_WIDTH), BF16),
        scratch_shapes=[pltpu.VMEM((2, GLA_DK, GLA_DV), F32),
                        pltpu.VMEM((seq, GLA_DV), F32),
                        pltpu.VMEM((2, ts, GLA_DK), BF16),
                        pltpu.VMEM((2, ts, GLA_DK), BF16),
                        pltpu.VMEM((2, ts, GLA_DK), BF16),
                        pltpu.VMEM((2, ts, GLA_DV), F32),
                        pltpu.VMEM((2, ts // C, GLA_DK, GLA_DV), BF16)],
        compiler_params=_params(("arbitrary", "arbitrary", "arbitrary")),
        name="gla",
    )(*dir_args, *dir_args, up_f, up_b, bias_f.reshape(1, GLA_KEY_WIDTH),
      bias_b.reshape(1, GLA_KEY_WIDTH), gain.reshape(1, GLA_WIDTH))


def _outproj_kernel(a_ref, g_ref, wa_ref, wg_ref, x_ref, gate_ref, fg_ref, o_ref, *, final):
    y = _dot(a_ref[...], wa_ref[...]) + _dot(g_ref[...], wg_ref[...])
    xn = x_ref[...] + gate_ref[...] * y
    if final:
        ms = jnp.mean(xn * xn, axis=-1, keepdims=True)
        xn = xn * lax.rsqrt(ms + EPS) * fg_ref[...]
    o_ref[...] = xn


def _outproj(a_out, g_out, w_out_bf16, x2, gate, final_gain, seq, final):
    M, D = x2.shape
    tm = 256
    bpt = seq // tm
    return pl.pallas_call(
        functools.partial(_outproj_kernel, final=final),
        grid=(M // tm,),
        in_specs=[pl.BlockSpec((tm, ATT_WIDTH), lambda i: (i, 0)),
                  pl.BlockSpec((tm, GLA_WIDTH), lambda i: (i, 0)),
                  pl.BlockSpec((ATT_WIDTH, D), lambda i: (0, 0)),
                  pl.BlockSpec((GLA_WIDTH, D), lambda i: (1, 0)),
                  pl.BlockSpec((tm, D), lambda i: (i, 0)),
                  pl.BlockSpec((None, 1, D), lambda i: (i // bpt, 0, 0)),
                  pl.BlockSpec((1, D), lambda i: (0, 0))],
        out_specs=pl.BlockSpec((tm, D), lambda i: (i, 0)),
        out_shape=jax.ShapeDtypeStruct((M, D), F32),
        compiler_params=_params(("arbitrary",)),
        name="outproj",
    )(a_out, g_out, w_out_bf16, w_out_bf16, x2, gate, final_gain.reshape(1, D))


def kernel(x, c, w_cond, b_cond, w_in, gla_gate_up_fwd, gla_gate_bias_fwd, gla_gate_up_bwd,
           gla_gate_bias_bwd, gla_norm_gain, rel_bias, w_out, final_gain):
    B, S, D = x.shape
    depth = w_cond.shape[0]
    R = GLA_GATE_RANK
    xs = x.reshape(B * S, D)
    for layer in range(depth):
        mod = _mod(c, w_cond[layer], b_cond[layer])
        shift, scale, gate = [m.reshape(B, 1, D) for m in jnp.split(mod, 3, axis=-1)]

        col_scale = jnp.where(jnp.arange(w_in.shape[-1]) < ATT_WIDTH, ATT_HEAD_DIM ** -0.5, 1.0)
        w_main = jnp.pad(w_in[layer] * col_scale,
                         ((0, 0), (0, PROJ_WIDTH + LR_PAD - w_in.shape[-1]))).astype(BF16)
        proj, lr, *res_qkv = _inproj(xs, scale, shift, w_main, S)

        def rows2d(t, d):
            return [a.reshape(B * S // d, -1) for a in t]

        nat = rows2d(_attn_pattern(proj.reshape(B, S, PROJ_WIDTH), _bias_tiles(rel_bias, 1), B, S, 1), 1)
        res = [rows2d(_attn_pattern(qkv.reshape(B, S // d, -1), _bias_tiles(rel_bias, d), B, S, d), d)
               for d, qkv in zip(RESIDUE_DILATIONS, res_qkv)]
        a_out = _merge(nat, res, proj)

        up_f = jnp.pad(gla_gate_up_fwd[layer], ((0, LR_PAD - R), (0, 0)))
        up_b = jnp.pad(gla_gate_up_bwd[layer], ((R, LR_PAD - 2 * R), (0, 0)))
        g_out = _gla(proj, lr, up_f, up_b, gla_gate_bias_fwd[layer], gla_gate_bias_bwd[layer],
                     gla_norm_gain[layer], B, S)

        xs = _outproj(a_out, g_out.reshape(B * S, GLA_WIDTH), w_out[layer].astype(BF16),
                      xs, gate, final_gain, S, final=layer == depth - 1)
    return xs.reshape(B, S, D)
```

```python
import functools
import math

import jax
import jax.numpy as jnp
import numpy as np
from jax import lax
from jax.experimental import pallas as pl
from jax.experimental.pallas import tpu as pltpu

D_MODEL = 2048
ATT_WIDTH = 1024
ATT_HEADS = 16
ATT_HEAD_DIM = 64
DILATED_PATTERNS = ((128, 1), (512, 4), (2048, 16))
ATT_STEPS = 64
GLA_WIDTH = 1024
GLA_HEADS = 4
GLA_KEY_WIDTH = 512
GLA_DK = 128
GLA_DV = 256
GLA_GATE_RANK = 16
GLA_GATE_NORM = 16.0
GLA_CHUNK = 64
REL_BUCKETS = 32
REL_MAX_DIST = 1024
EPS = 1e-6
NEG_INF = -1e30

PROJ_WIDTH = 4 * ATT_WIDTH + 2 * GLA_KEY_WIDTH + 2 * GLA_WIDTH
COL_AQ, COL_AK, COL_AV, COL_AG = 0, 1024, 2048, 3072
COL_GQ, COL_GK, COL_GV, COL_GG = 4096, 4608, 5120, 6144
LR_PAD = 128
ATT_QKV_TILES = 3
RESIDUE_DILATIONS = tuple(d for _, d in DILATED_PATTERNS if d > 1)
ATT_TQ = 128
ATT_TK = ATT_TQ + 2 * ATT_STEPS
ATT_GROUP_HEADS = 4
ATT_HEAD_SETS = 1

VMEM_LIMIT = 56 * 1024 * 1024

BF16 = jnp.bfloat16
F32 = jnp.float32


def _params(sem):
    return pltpu.CompilerParams(dimension_semantics=sem, vmem_limit_bytes=VMEM_LIMIT)


def _dot(a, b):
    return jnp.dot(a, b, preferred_element_type=F32)


def _dot_nt(a, b):
    return lax.dot_general(a, b, (((1,), (1,)), ((), ())), preferred_element_type=F32)


def _dot_tn(a, b):
    return lax.dot_general(a, b, (((0,), (0,)), ((), ())), preferred_element_type=F32)


def _split_bf16(x):
    hi = x.astype(BF16)
    lo = (x - hi.astype(F32)).astype(BF16)
    return hi, lo


def _silu(x):
    return x / (1.0 + jnp.exp(-x))


def _mod_kernel(c_ref, w_ref, b_ref, o_ref):
    s = _silu(c_ref[...])
    o_ref[...] = jnp.dot(s, w_ref[...], preferred_element_type=F32,
                         precision=lax.Precision.HIGHEST) + b_ref[...]


def _mod(c, w_cond, b_cond):
    B, D = c.shape
    N = w_cond.shape[1]
    tn = 768
    cp = jnp.pad(c, ((0, 8 - B), (0, 0)))
    out = pl.pallas_call(
        _mod_kernel,
        grid=(N // tn,),
        in_specs=[pl.BlockSpec((8, D), lambda j: (0, 0)),
                  pl.BlockSpec((D, tn), lambda j: (0, j)),
                  pl.BlockSpec((1, tn), lambda j: (0, j))],
        out_specs=pl.BlockSpec((8, tn), lambda j: (0, j)),
        out_shape=jax.ShapeDtypeStruct((8, N), F32),
        compiler_params=_params(("arbitrary",)),
        name="mod",
    )(cp, w_cond, b_cond.reshape(1, N))
    return out[:B]


def _wprep_kernel(w_ref, o_ref, *, tn, ncols):
    col = pl.program_id(0) * tn + lax.broadcasted_iota(jnp.int32, (1, tn), 1)
    scale = jnp.where(col < ATT_WIDTH, ATT_HEAD_DIM ** -0.5, 1.0)
    o_ref[...] = jnp.where(col < ncols, w_ref[...] * scale, 0.0).astype(BF16)


def _wprep(w):
    D, ncols = w.shape
    width = PROJ_WIDTH + LR_PAD
    tn = 384
    assert width % tn == 0
    return pl.pallas_call(
        functools.partial(_wprep_kernel, tn=tn, ncols=ncols),
        grid=(width // tn,),
        in_specs=[pl.BlockSpec((D, tn), lambda j: (0, j))],
        out_specs=pl.BlockSpec((D, tn), lambda j: (0, j)),
        out_shape=jax.ShapeDtypeStruct((D, width), BF16),
        compiler_params=_params(("arbitrary",)),
        name="wprep",
    )(w)


def _inproj_kernel(x_ref, scale_ref, shift_ref, w_ref, p_ref, lr_ref, *rest, tm, tn):
    nres = len(RESIDUE_DILATIONS)
    res_refs, h_scr, acc_scr = rest[:nres], rest[nres], rest[nres + 1:]
    j = pl.program_id(1)

    @pl.when(j == 0)
    def _():
        x = x_ref[...]
        ms = jnp.mean(x * x, axis=-1, keepdims=True)
        h = x * lax.rsqrt(ms + EPS) * (1.0 + scale_ref[...]) + shift_ref[...]
        hb = h.astype(BF16)
        h_scr[...] = hb
        lr_ref[...] = _dot(hb, w_ref[:, PROJ_WIDTH:PROJ_WIDTH + LR_PAD])

    @pl.when(j >= ATT_QKV_TILES)
    def _():
        w = w_ref[:, pl.ds(pl.multiple_of(j * tn, tn), tn)]
        p_ref[...] = _dot(h_scr[...], w).astype(BF16)

    @pl.when(j < ATT_QKV_TILES)
    def _():
        h = h_scr[...]
        chunk = 256
        for c0 in range(0, tn, chunk):
            acc = _dot(h, w_ref[:, pl.ds(pl.multiple_of(j * tn + c0, chunk), chunk)])
            p_ref[:, c0:c0 + chunk] = acc.astype(BF16)
            for c in range(c0 // 128, (c0 + chunk) // 128):
                lanes = slice(c * 128 - c0, (c + 1) * 128 - c0)
                src, prev_d = acc_scr[0], 1
                src[c] = acc[:, lanes]
                for lvl, (ref, d) in enumerate(zip(res_refs, RESIDUE_DILATIONS)):
                    ratio, n = d // prev_d, tm // d
                    dst = acc_scr[lvl + 1] if lvl + 1 < len(RESIDUE_DILATIONS) else None
                    for rp in range(prev_d):
                        for a in range(ratio):
                            r = rp + prev_d * a
                            rows = src[c, pl.ds(rp * (tm // prev_d) + a, n, stride=ratio), :]
                            ref[:, r * tn + c * 128:r * tn + (c + 1) * 128] = rows.astype(BF16)
                            if dst is not None:
                                dst[c, r * n:(r + 1) * n, :] = rows
                    src, prev_d = dst, d


def _inproj(x2, scale, shift, w_main, seq):
    M, D = x2.shape
    tm, tn = 512, ATT_WIDTH
    bpt = seq // tm
    last_qkv = ATT_QKV_TILES - 1
    res_specs = [pl.BlockSpec((tm // d, d * tn), lambda i, j: (i, jnp.minimum(j, last_qkv)))
                 for d in RESIDUE_DILATIONS]
    res_shapes = [jax.ShapeDtypeStruct((M // d, ATT_QKV_TILES * d * tn), BF16)
                  for d in RESIDUE_DILATIONS]
    return pl.pallas_call(
        functools.partial(_inproj_kernel, tm=tm, tn=tn),
        grid=(M // tm, PROJ_WIDTH // tn),
        in_specs=[pl.BlockSpec((tm, D), lambda i, j: (i, 0)),
                  pl.BlockSpec((None, 1, D), lambda i, j: (i // bpt, 0, 0)),
                  pl.BlockSpec((None, 1, D), lambda i, j: (i // bpt, 0, 0)),
                  pl.BlockSpec(w_main.shape, lambda i, j: (0, 0), pipeline_mode=pl.Buffered(1))],
        out_specs=[pl.BlockSpec((tm, tn), lambda i, j: (i, j)),
                   pl.BlockSpec((tm, LR_PAD), lambda i, j: (i, 0))] + res_specs,
        out_shape=[jax.ShapeDtypeStruct((M, PROJ_WIDTH), BF16),
                   jax.ShapeDtypeStruct((M, LR_PAD), F32)] + res_shapes,
        scratch_shapes=[pltpu.VMEM((tm, D), BF16)]
                       + [pltpu.VMEM((tn // 128, tm, 128), F32) for _ in RESIDUE_DILATIONS],
        compiler_params=_params(("arbitrary", "arbitrary")),
        name="inproj",
    )(x2, scale, shift, w_main)


def _t5_bucket_np(rel):
    nb = REL_BUCKETS // 2
    max_exact = nb // 2
    n = np.abs(rel)
    large = max_exact + (np.log(np.maximum(n, 1) / max_exact)
                         / np.log(REL_MAX_DIST / max_exact) * (nb - max_exact)).astype(np.int32)
    large = np.minimum(large, nb - 1)
    return (np.where(rel > 0, nb, 0) + np.where(n < max_exact, n, large)).astype(np.int32)


def _bias_kernel(rbt_ref, bucket_ref, mask_ref, o_ref):
    rbt = rbt_ref[...]
    bucket = bucket_ref[...]
    ids = lax.broadcasted_iota(jnp.int32, (REL_BUCKETS, bucket.shape[1]), 0)
    onehot = jnp.where(ids == bucket, 1.0, 0.0).astype(BF16)
    hi = rbt.astype(BF16)
    rest = rbt - hi.astype(F32)
    mid = rest.astype(BF16)
    lo = (rest - mid.astype(F32)).astype(BF16)
    tbl = _dot(hi, onehot) + _dot(mid, onehot) + _dot(lo, onehot)
    for v in range(3):
        o_ref[v] = jnp.where(mask_ref[v] > 0.5, tbl, NEG_INF)


def _bias_tiles(rel_bias, dilation):
    w, tq, tk = ATT_STEPS, ATT_TQ, ATT_TK
    qi = np.arange(tq)[:, None]
    kj = np.arange(tk)[None, :]
    step = kj - w - qi
    band = np.abs(step) <= w
    bucket = _t5_bucket_np(step * dilation).reshape(1, tq * tk)
    masks = np.stack([band & (kj >= w), band, band & (kj < tk - w)]).astype(np.float32)
    masks = masks.reshape(3, 1, tq * tk)
    out = pl.pallas_call(
        _bias_kernel,
        out_shape=jax.ShapeDtypeStruct((3, ATT_HEADS, tq * tk), F32),
        compiler_params=pltpu.CompilerParams(vmem_limit_bytes=VMEM_LIMIT),
        name=f"bias_d{dilation}",
    )(rel_bias.T, jnp.asarray(bucket), jnp.asarray(masks))
    return out.reshape(3, ATT_HEADS, tq, tk)


def _attn_kernel(q_ref, kp_ref, km_ref, kn_ref, vp_ref, vm_ref, vn_ref, bias_ref,
                 o_ref, m_ref, den_ref, kbuf, v_slot, s_scr, p_scr, inv_scr, *, tb):
    w, tq, tk = ATT_STEPS, ATT_TQ, ATT_TK
    nsub = tb // tq
    npair = ATT_HEADS // 2
    gh = ATT_GROUP_HEADS
    gw = gh * ATT_HEAD_DIM
    ngroup = ATT_HEADS // gh
    nk = tb + 2 * w
    i = pl.program_id(2)
    first = i == 0
    last = i == pl.num_programs(2) - 1

    lane = lax.broadcasted_iota(jnp.int32, (1, ATT_WIDTH), 1)
    slot = (lane % gw) // ATT_HEAD_DIM
    row = 0
    for kpart, vpart in ((kp_ref, vp_ref), (km_ref, vm_ref), (kn_ref, vn_ref)):
        rows = slice(row, row + kpart.shape[0])
        kbuf[rows] = kpart[...]
        x = vpart[...]
        for s in range(gh):
            v_slot[s, rows] = jnp.where(slot == s, x, jnp.zeros_like(x))
        row += kpart.shape[0]

    half = tq // 2
    lower_half = lax.broadcasted_iota(jnp.int32, (half, 128), 1) < ATT_HEAD_DIM
    lower_q = lax.broadcasted_iota(jnp.int32, (tq, 128), 1) < ATT_HEAD_DIM

    nsets = ATT_HEAD_SETS

    def head_pairs(hs):
        return range(hs * npair // nsets, (hs + 1) * npair // nsets)

    def tile(j):
        return pl.multiple_of(j * tq, tq), 0

    def logits(j, hs):
        qs, buf = tile(j)
        for hp in head_pairs(hs):
            cp = slice(hp * 128, (hp + 1) * 128)
            q = q_ref[pl.ds(qs, tq), cp]
            zero = jnp.zeros_like(q)
            q2 = jnp.concatenate([jnp.where(lower_q, q, zero), jnp.where(lower_q, zero, q)], axis=0)
            s2 = _dot_nt(q2, kbuf[pl.ds(qs, tk), cp])
            s_scr[buf, 2 * hp] = s2[:tq]
            s_scr[buf, 2 * hp + 1] = s2[tq:]

    def softmax(j, hs):
        qs, buf = tile(j)
        var = jnp.where(jnp.logical_and(first, j == 0), 0,
                        jnp.where(jnp.logical_and(last, j == nsub - 1), 2, 1))
        if hs == 0:
            m_ref[pl.ds(qs, tq), :] = jnp.zeros((tq, 128), F32)
            den_ref[pl.ds(qs, tq), :] = jnp.ones((tq, 128), F32)
        for hp in head_pairs(hs):
            for r0 in (0, half):
                dens = []
                for h in (2 * hp, 2 * hp + 1):
                    s = s_scr[buf, h, r0:r0 + half, :] + bias_ref[var, h, r0:r0 + half, :]
                    m = jnp.max(s, axis=-1, keepdims=True)
                    p = jnp.exp(s - m)
                    den = jnp.sum(p, axis=-1, keepdims=True)
                    p_scr[buf, h // gh, r0:r0 + half, (h % gh) * tk:(h % gh + 1) * tk] = p.astype(BF16)
                    m_ref[pl.ds(qs + r0, half), h:h + 1] = m
                    den_ref[pl.ds(qs + r0, half), h:h + 1] = den
                    dens.append(den)
                pair = hp % (gh // 2)
                inv_scr[buf, (2 * hp) // gh, r0:r0 + half, pair * 128:(pair + 1) * 128] = (
                    1.0 / jnp.where(lower_half, dens[0], dens[1]))

    def outputs(j, hs):
        qs, buf = tile(j)
        for g in range(hs * ngroup // nsets, (hs + 1) * ngroup // nsets):
            cg = slice(g * gw, (g + 1) * gw)
            v_stack = jnp.concatenate([v_slot[s, pl.ds(qs, tk), cg] for s in range(gh)], axis=0)
            o = _dot(p_scr[buf, g], v_stack)
            o_ref[pl.ds(qs, tq), cg] = (o * inv_scr[buf, g]).astype(BF16)

    for hs in range(nsets):
        def one_tile(j, carry, hs=hs):
            logits(j, hs)
            softmax(j, hs)
            outputs(j, hs)
            return carry

        lax.fori_loop(0, nsub, one_tile, 0)


def _attn_pattern(qkv, bias, batch, seq, dilation):
    w = ATT_STEPS
    L = seq // dilation
    tb = min(1024, L)
    nblk = L // tb
    hb = tb // w
    nhalo = L // w

    def main(j):
        return pl.BlockSpec((None, tb, ATT_WIDTH), lambda b, r, i: (b, i, j * dilation + r))

    def prev(j):
        return pl.BlockSpec((None, w, ATT_WIDTH),
                            lambda b, r, i: (b, jnp.maximum(i * hb - 1, 0), j * dilation + r))

    def nxt(j):
        return pl.BlockSpec((None, w, ATT_WIDTH),
                            lambda b, r, i: (b, jnp.minimum((i + 1) * hb, nhalo - 1), j * dilation + r))

    nk, gh = tb + 2 * w, ATT_GROUP_HEADS
    return pl.pallas_call(
        functools.partial(_attn_kernel, tb=tb),
        grid=(batch, dilation, nblk),
        in_specs=[main(0), prev(1), main(1), nxt(1), prev(2), main(2), nxt(2),
                  pl.BlockSpec((3, ATT_HEADS, ATT_TQ, ATT_TK), lambda b, r, i: (0, 0, 0, 0),
                               pipeline_mode=pl.Buffered(1))],
        out_specs=[pl.BlockSpec((None, tb, ATT_WIDTH), lambda b, r, i: (b, i, r)),
                   pl.BlockSpec((None, tb, 128), lambda b, r, i: (b, i, r)),
                   pl.BlockSpec((None, tb, 128), lambda b, r, i: (b, i, r))],
        out_shape=[jax.ShapeDtypeStruct((batch, L, dilation * ATT_WIDTH), BF16),
                   jax.ShapeDtypeStruct((batch, L, dilation * 128), F32),
                   jax.ShapeDtypeStruct((batch, L, dilation * 128), F32)],
        scratch_shapes=[pltpu.VMEM((nk, ATT_WIDTH), BF16),
                        pltpu.VMEM((gh, nk, ATT_WIDTH), BF16),
                        pltpu.VMEM((1, ATT_HEADS, ATT_TQ, ATT_TK), F32),
                        pltpu.VMEM((1, ATT_HEADS // gh, ATT_TQ, gh * ATT_TK), BF16),
                        pltpu.VMEM((1, ATT_HEADS // gh, ATT_TQ, gh * ATT_HEAD_DIM), F32)],
        compiler_params=_params(("arbitrary", "arbitrary", "arbitrary")),
        name=f"attn_d{dilation}",
    )(qkv, qkv, qkv, qkv, qkv, qkv, qkv, bias)


def _merge_kernel(*refs, tm):
    nres = len(RESIDUE_DILATIONS)
    o1_ref, m1_ref, d1_ref = refs[:3]
    res = [refs[3 + 3 * n:6 + 3 * n] for n in range(nres)]
    ag_ref, e_ref, out_ref = refs[3 + 3 * nres:6 + 3 * nres]
    scr = [refs[6 + 3 * nres + 3 * n:9 + 3 * nres + 3 * n] for n in range(nres)]
    tmp = refs[6 + 6 * nres]

    ncol = ATT_WIDTH // 128
    for d, (o_ref, m_ref, d_ref), (so, sm, sd) in zip(RESIDUE_DILATIONS, res, scr):
        prev = d // 4 if d > 4 else 1
        for r in range(d):
            rp, a = r % prev, r // prev
            for c in range(ncol):
                col = r * ATT_WIDTH + c * 128
                rows = o_ref[:, col:col + 128].astype(F32)
                if prev == 1:
                    so[c, pl.ds(r, tm // d, stride=d), :] = rows
                else:
                    tmp[c, pl.ds(rp * (tm // prev) + a, tm // d, stride=d // prev), :] = rows
            sm[pl.ds(r, tm // d, stride=d), :] = m_ref[:, r * 128:(r + 1) * 128]
            sd[pl.ds(r, tm // d, stride=d), :] = d_ref[:, r * 128:(r + 1) * 128]
        if prev > 1:
            n = tm // prev
            for rp in range(prev):
                for c in range(ncol):
                    so[c, pl.ds(rp, n, stride=prev), :] = tmp[c, rp * n:(rp + 1) * n, :]

    rc, gw = 128, 256
    head_lane = lax.broadcasted_iota(jnp.int32, (rc, 128), 1) < ATT_HEADS
    for r0 in range(0, tm, rc):
        rows = slice(r0, r0 + rc)
        lses = ([m1_ref[rows, :] + jnp.log(d1_ref[rows, :])]
                + [sm[rows, :] + jnp.log(sd[rows, :]) for _, sm, sd in scr])
        mx = functools.reduce(jnp.maximum, lses)
        es = [jnp.exp(l - mx) for l in lses]
        inv = 1.0 / functools.reduce(jnp.add, es)
        packed = []
        for ei in es:
            wgt = jnp.where(head_lane, ei * inv, 0.0)
            hi = wgt.astype(BF16).astype(F32)
            lo = (wgt - hi).astype(BF16).astype(F32)
            packed.append((hi + pltpu.roll(lo, ATT_HEADS, 1)).astype(BF16))
        for c0 in range(0, ATT_WIDTH, gw):
            cols = slice(c0, c0 + gw)
            e = e_ref[:, cols]
            outs = [o1_ref[rows, cols].astype(F32)] + [
                jnp.concatenate([so[c, rows, :] for c in range(c0 // 128, (c0 + gw) // 128)], axis=1)
                for so, _, _ in scr]
            att = functools.reduce(jnp.add, [_dot(w, e) * o for w, o in zip(packed, outs)])
            out_ref[rows, cols] = (att * _silu(ag_ref[rows, cols].astype(F32))).astype(BF16)


def _merge(nat, res, proj):
    M = proj.shape[0]
    tm = 512
    expand = np.zeros((128, ATT_WIDTH), np.float32)
    for h in range(ATT_HEADS):
        expand[h, h * ATT_HEAD_DIM:(h + 1) * ATT_HEAD_DIM] = 1.0
        expand[ATT_HEADS + h, h * ATT_HEAD_DIM:(h + 1) * ATT_HEAD_DIM] = 1.0
    row = lambda rows, width: pl.BlockSpec((rows, width), lambda i: (i, 0))
    triple = lambda d: [row(tm // d, d * ATT_WIDTH), row(tm // d, d * 128), row(tm // d, d * 128)]
    in_specs = (triple(1) + [s for d in RESIDUE_DILATIONS for s in triple(d)]
                + [pl.BlockSpec((tm, ATT_WIDTH), lambda i: (i, COL_AG // ATT_WIDTH)),
                   pl.BlockSpec((128, ATT_WIDTH), lambda i: (0, 0))])
    scratch = []
    for _ in RESIDUE_DILATIONS:
        scratch += [pltpu.VMEM((ATT_WIDTH // 128, tm, 128), F32),
                    pltpu.VMEM((tm, 128), F32), pltpu.VMEM((tm, 128), F32)]
    scratch.append(pltpu.VMEM((ATT_WIDTH // 128, tm, 128), F32))
    return pl.pallas_call(
        functools.partial(_merge_kernel, tm=tm),
        grid=(M // tm,),
        in_specs=in_specs,
        out_specs=row(tm, ATT_WIDTH),
        out_shape=jax.ShapeDtypeStruct((M, ATT_WIDTH), BF16),
        scratch_shapes=scratch,
        compiler_params=_params(("arbitrary",)),
        name="merge",
    )(*nat, *[a for t in res for a in t], proj, jnp.asarray(expand, BF16))


def _gla_kernel(*refs, ts, nstep):
    fwd_in, bwd_in = refs[:5], refs[5:10]
    (upf_ref, upb_ref, gbf_ref, gbb_ref, gain_ref, out_ref,
     state, o_acc, qf_scr, kd_scr, ks_scr, oin_scr, st_scr) = refs[10:]
    C = GLA_CHUNK
    nchunk = ts // C
    chunks = [slice(c * C, (c + 1) * C) for c in range(nchunk)]
    i = pl.program_id(2)
    dirs = [(0, False, fwd_in, upf_ref, gbf_ref), (1, True, bwd_in, upb_ref, gbb_ref)]

    @pl.when(i == 0)
    def _():
        state[...] = jnp.zeros_like(state)


    log_gs = []
    for d, reverse, (q_ref, k_ref, v_ref, lr_ref, gg_ref), up_ref, gb_ref in dirs:
        lr_hi, lr_lo = _split_bf16(lr_ref[...])
        up_hi, up_lo = _split_bf16(up_ref[...])
        z = _dot(lr_hi, up_hi) + _dot(lr_hi, up_lo) + _dot(lr_lo, up_hi) + gb_ref[...]
        log_gs.append((jnp.minimum(z, 0.0) - jnp.log(1.0 + jnp.exp(-jnp.abs(z))))
                      * (1.0 / GLA_GATE_NORM))

    ri = lax.broadcasted_iota(jnp.int32, (C, C), 0)
    ci = lax.broadcasted_iota(jnp.int32, (C, C), 1)
    dec_cols = [[], []]
    for d, reverse, (q_ref, k_ref, v_ref, lr_ref, gg_ref), up_ref, gb_ref in dirs:
        tri = jnp.where((ci >= ri) if reverse else (ci <= ri), 1.0, 0.0).astype(BF16)
        for rows in chunks:
            g_hi, g_lo = _split_bf16(log_gs[d][rows])
            cum = _dot(tri, jnp.concatenate([g_hi, g_lo], axis=1))
            b = cum[:, :GLA_DK] + cum[:, GLA_DK:]
            b_edge = b[0:1] if reverse else b[C - 1:C]
            q = q_ref[rows, :].astype(F32)
            k = k_ref[rows, :].astype(F32)
            qf_scr[d, rows, :] = (q * jnp.exp(b) * (GLA_DK ** -0.5)).astype(BF16)
            kd_scr[d, rows, :] = (k * jnp.exp(-b)).astype(BF16)
            ks_scr[d, rows, :] = (k * jnp.exp(b_edge - b)).astype(BF16)
            dec = jnp.broadcast_to(jnp.exp(b_edge), (GLA_DK, GLA_DK)).T
            dec_cols[d].append(jnp.concatenate([dec, dec], axis=1))

    atts = [[jnp.where((ci >= ri) if reverse else (ci <= ri),
                       _dot_nt(qf_scr[d, rows, :], kd_scr[d, rows, :]), 0.0).astype(BF16)
             for rows in chunks] for d, reverse, *_ in dirs]

    kvs = [[], []]
    for d, reverse, (q_ref, k_ref, v_ref, lr_ref, gg_ref), up_ref, gb_ref in dirs:
        for rows, att in zip(chunks, atts[d]):
            v = v_ref[rows, :]
            oin_scr[d, rows, :] = _dot(att, v)
            kvs[d].append(_dot_tn(ks_scr[d, rows, :], v))

    orders = [list(range(nchunk)), list(range(nchunk - 1, -1, -1))]
    for d, reverse, *_ in dirs:
        st = state[d]
        for c in orders[d]:
            st_scr[d, c] = st.astype(BF16)
            st = st * dec_cols[d][c] + kvs[d][c]
        state[d] = st

    for d, reverse, *_ in dirs:
        for c in orders[d]:
            rows = chunks[c]
            oin_scr[d, rows, :] = oin_scr[d, rows, :] + _dot(qf_scr[d, rows, :], st_scr[d, c])

    blocks = [i, nstep - 1 - i]

    @pl.when(i < nstep // 2)
    def _():
        for d, reverse, *_ in dirs:
            base = pl.multiple_of(blocks[d] * ts, ts)
            for rows in chunks:
                o_acc[pl.ds(base + rows.start, C), :] = oin_scr[d, rows, :]

    @pl.when(i >= nstep // 2)
    def _():
        for d, reverse, (q_ref, k_ref, v_ref, lr_ref, gg_ref), up_ref, gb_ref in dirs:
            base = pl.multiple_of(blocks[d] * ts, ts)
            for rows in chunks:
                dst = pl.ds(base + rows.start, C)
                tot = oin_scr[d, rows, :] + o_acc[dst, :]
                ms = jnp.mean(tot * tot, axis=-1, keepdims=True)
                g_o = tot * lax.rsqrt(ms + EPS) * gain_ref[...]
                out_ref[dst, :] = (g_o * _silu(gg_ref[rows, :].astype(F32))).astype(BF16)


def _gla(proj, lr, up_f, up_b, bias_f, bias_b, gain, batch, seq):
    ts = 1024
    nstep = seq // ts
    assert nstep % 2 == 0
    C = GLA_CHUNK
    p3 = proj.reshape(batch, seq, PROJ_WIDTH)
    lr3 = lr.reshape(batch, seq, LR_PAD)

    def direction_specs(step):
        def seq_block(width, col0):
            return pl.BlockSpec((None, ts, width), lambda b, h, i: (b, step(i), col0 // width + h))
        return [seq_block(GLA_DK, COL_GQ), seq_block(GLA_DK, COL_GK), seq_block(GLA_DV, COL_GV),
                pl.BlockSpec((None, ts, LR_PAD), lambda b, h, i: (b, step(i), 0)),
                seq_block(GLA_DV, COL_GG)]

    per_head = lambda rows, width: pl.BlockSpec((rows, width), lambda b, h, i: (0, h))
    in_specs = (direction_specs(lambda i: i) + direction_specs(lambda i: nstep - 1 - i)
                + [per_head(LR_PAD, GLA_DK), per_head(LR_PAD, GLA_DK),
                   per_head(1, GLA_DK), per_head(1, GLA_DK), per_head(1, GLA_DV)])
    dir_args = [p3, p3, p3, lr3, p3]
    return pl.pallas_call(
        functools.partial(_gla_kernel, ts=ts, nstep=nstep),
        grid=(batch, GLA_HEADS, nstep),
        in_specs=in_specs,
        out_specs=pl.BlockSpec((None, seq, GLA_DV), lambda b, h, i: (b, 0, h)),
        out_shape=jax.ShapeDtypeStruct((batch, seq, GLA_WIDTH), BF16),
        scratch_shapes=[pltpu.VMEM((2, GLA_DK, GLA_DV), F32),
                        pltpu.VMEM((seq, GLA_DV), F32),
                        pltpu.VMEM((2, ts, GLA_DK), BF16),
                        pltpu.VMEM((2, ts, GLA_DK), BF16),
                        pltpu.VMEM((2, ts, GLA_DK), BF16),
                        pltpu.VMEM((2, ts, GLA_DV), F32),
                        pltpu.VMEM((2, ts // C, GLA_DK, GLA_DV), BF16)],
        compiler_params=_params(("arbitrary", "arbitrary", "arbitrary")),
        name="gla",
    )(*dir_args, *dir_args, up_f, up_b, bias_f.reshape(1, GLA_KEY_WIDTH),
      bias_b.reshape(1, GLA_KEY_WIDTH), gain.reshape(1, GLA_WIDTH))


def _outproj_kernel(a_ref, g_ref, wa_ref, wg_ref, x_ref, gate_ref, fg_ref, o_ref, *, final):
    y = _dot(a_ref[...], wa_ref[...]) + _dot(g_ref[...], wg_ref[...])
    xn = x_ref[...] + gate_ref[...] * y
    if final:
        ms = jnp.mean(xn * xn, axis=-1, keepdims=True)
        xn = xn * lax.rsqrt(ms + EPS) * fg_ref[...]
    o_ref[...] = xn


def _outproj(a_out, g_out, w_out_bf16, x2, gate, final_gain, seq, final):
    M, D = x2.shape
    tm = 512
    bpt = seq // tm
    return pl.pallas_call(
        functools.partial(_outproj_kernel, final=final),
        grid=(M // tm,),
        in_specs=[pl.BlockSpec((tm, ATT_WIDTH), lambda i: (i, 0)),
                  pl.BlockSpec((tm, GLA_WIDTH), lambda i: (i, 0)),
                  pl.BlockSpec((ATT_WIDTH, D), lambda i: (0, 0)),
                  pl.BlockSpec((GLA_WIDTH, D), lambda i: (1, 0)),
                  pl.BlockSpec((tm, D), lambda i: (i, 0)),
                  pl.BlockSpec((None, 1, D), lambda i: (i // bpt, 0, 0)),
                  pl.BlockSpec((1, D), lambda i: (0, 0))],
        out_specs=pl.BlockSpec((tm, D), lambda i: (i, 0)),
        out_shape=jax.ShapeDtypeStruct((M, D), F32),
        compiler_params=_params(("arbitrary",)),
        name="outproj",
    )(a_out, g_out, w_out_bf16, w_out_bf16, x2, gate, final_gain.reshape(1, D))


def kernel(x, c, w_cond, b_cond, w_in, gla_gate_up_fwd, gla_gate_bias_fwd, gla_gate_up_bwd,
           gla_gate_bias_bwd, gla_norm_gain, rel_bias, w_out, final_gain):
    B, S, D = x.shape
    depth = w_cond.shape[0]
    R = GLA_GATE_RANK
    xs = x.reshape(B * S, D)
    for layer in range(depth):
        mod = _mod(c, w_cond[layer], b_cond[layer])
        shift, scale, gate = [m.reshape(B, 1, D) for m in jnp.split(mod, 3, axis=-1)]

        proj, lr, *res_qkv = _inproj(xs, scale, shift, _wprep(w_in[layer]), S)

        def rows2d(t, d):
            return [a.reshape(B * S // d, -1) for a in t]

        nat = rows2d(_attn_pattern(proj.reshape(B, S, PROJ_WIDTH), _bias_tiles(rel_bias, 1), B, S, 1), 1)
        res = [rows2d(_attn_pattern(qkv.reshape(B, S // d, -1), _bias_tiles(rel_bias, d), B, S, d), d)
               for d, qkv in zip(RESIDUE_DILATIONS, res_qkv)]
        a_out = _merge(nat, res, proj)

        up_f = jnp.pad(gla_gate_up_fwd[layer], ((0, LR_PAD - R), (0, 0)))
        up_b = jnp.pad(gla_gate_up_bwd[layer], ((R, LR_PAD - 2 * R), (0, 0)))
        g_out = _gla(proj, lr, up_f, up_b, gla_gate_bias_fwd[layer], gla_gate_bias_bwd[layer],
                     gla_norm_gain[layer], B, S)

        xs = _outproj(a_out, g_out.reshape(B * S, GLA_WIDTH), w_out[layer].astype(BF16),
                      xs, gate, final_gain, S, final=layer == depth - 1)
    return xs.reshape(B, S, D)
```

```python
import functools
import math

import jax
import jax.numpy as jnp
import numpy as np
from jax import lax
from jax.experimental import pallas as pl
from jax.experimental.pallas import tpu as pltpu

D_MODEL = 2048
ATT_WIDTH = 1024
ATT_HEADS = 16
ATT_HEAD_DIM = 64
DILATED_PATTERNS = ((128, 1), (512, 4), (2048, 16))
ATT_STEPS = 64
GLA_WIDTH = 1024
GLA_HEADS = 4
GLA_KEY_WIDTH = 512
GLA_DK = 128
GLA_DV = 256
GLA_GATE_RANK = 16
GLA_GATE_NORM = 16.0
GLA_CHUNK = 64
REL_BUCKETS = 32
REL_MAX_DIST = 1024
EPS = 1e-6
NEG_INF = -1e30

PROJ_WIDTH = 4 * ATT_WIDTH + 2 * GLA_KEY_WIDTH + 2 * GLA_WIDTH
COL_AQ, COL_AK, COL_AV, COL_AG = 0, 1024, 2048, 3072
COL_GQ, COL_GK, COL_GV, COL_GG = 4096, 4608, 5120, 6144
LR_PAD = 128
ATT_QKV_TILES = 3
RESIDUE_DILATIONS = tuple(d for _, d in DILATED_PATTERNS if d > 1)
ATT_TQ = 128
ATT_TK = ATT_TQ + 2 * ATT_STEPS
ATT_GROUP_HEADS = 4
ATT_HEAD_SETS = 1

VMEM_LIMIT = 56 * 1024 * 1024

BF16 = jnp.bfloat16
F32 = jnp.float32


def _params(sem):
    return pltpu.CompilerParams(dimension_semantics=sem, vmem_limit_bytes=VMEM_LIMIT)


def _dot(a, b):
    return jnp.dot(a, b, preferred_element_type=F32)


def _dot_nt(a, b):
    return lax.dot_general(a, b, (((1,), (1,)), ((), ())), preferred_element_type=F32)


def _dot_tn(a, b):
    return lax.dot_general(a, b, (((0,), (0,)), ((), ())), preferred_element_type=F32)


def _split_bf16(x):
    hi = x.astype(BF16)
    lo = (x - hi.astype(F32)).astype(BF16)
    return hi, lo


def _silu(x):
    return x / (1.0 + jnp.exp(-x))


def _mod_kernel(c_ref, w_ref, b_ref, o_ref):
    s_hi, s_lo = _split_bf16(_silu(c_ref[...]))
    w_hi, w_lo = _split_bf16(w_ref[...])
    o_ref[...] = _dot(s_hi, w_hi) + _dot(s_lo, w_hi) + _dot(s_hi, w_lo) + b_ref[...]


def _mod(c, w_cond, b_cond):
    B, D = c.shape
    N = w_cond.shape[1]
    tn = 768
    cp = jnp.pad(c, ((0, 8 - B), (0, 0)))
    out = pl.pallas_call(
        _mod_kernel,
        grid=(N // tn,),
        in_specs=[pl.BlockSpec((8, D), lambda j: (0, 0)),
                  pl.BlockSpec((D, tn), lambda j: (0, j)),
                  pl.BlockSpec((1, tn), lambda j: (0, j))],
        out_specs=pl.BlockSpec((8, tn), lambda j: (0, j)),
        out_shape=jax.ShapeDtypeStruct((8, N), F32),
        compiler_params=_params(("arbitrary",)),
        name="mod",
    )(cp, w_cond, b_cond.reshape(1, N))
    return out[:B]


def _wprep_kernel(w_ref, o_ref, *, tn, ncols):
    col = pl.program_id(0) * tn + lax.broadcasted_iota(jnp.int32, (1, tn), 1)
    scale = jnp.where(col < ATT_WIDTH, ATT_HEAD_DIM ** -0.5, 1.0)
    o_ref[...] = jnp.where(col < ncols, w_ref[...] * scale, 0.0).astype(BF16)


def _wprep(w_all, layer):
    _, D, ncols = w_all.shape
    width = PROJ_WIDTH + LR_PAD
    tn = 384
    assert width % tn == 0
    return pl.pallas_call(
        functools.partial(_wprep_kernel, tn=tn, ncols=ncols),
        grid=(width // tn,),
        in_specs=[pl.BlockSpec((None, D, tn), lambda j: (layer, 0, j))],
        out_specs=pl.BlockSpec((D, tn), lambda j: (0, j)),
        out_shape=jax.ShapeDtypeStruct((D, width), BF16),
        compiler_params=_params(("arbitrary",)),
        name="wprep",
    )(w_all)


def _inproj_kernel(x_ref, scale_ref, shift_ref, w_ref, p_ref, lr_ref, *rest, tm, tn):
    nres = len(RESIDUE_DILATIONS)
    res_refs, h_scr, acc_scr = rest[:nres], rest[nres], rest[nres + 1:]
    j = pl.program_id(1)

    @pl.when(j == 0)
    def _():
        x = x_ref[...]
        ms = jnp.mean(x * x, axis=-1, keepdims=True)
        h = x * lax.rsqrt(ms + EPS) * (1.0 + scale_ref[...]) + shift_ref[...]
        hb = h.astype(BF16)
        h_scr[...] = hb
        lr_ref[...] = _dot(hb, w_ref[:, PROJ_WIDTH:PROJ_WIDTH + LR_PAD])

    @pl.when(j >= ATT_QKV_TILES)
    def _():
        w = w_ref[:, pl.ds(pl.multiple_of(j * tn, tn), tn)]
        p_ref[...] = _dot(h_scr[...], w).astype(BF16)

    @pl.when(j < ATT_QKV_TILES)
    def _():
        h = h_scr[...]
        chunk = 256
        for c0 in range(0, tn, chunk):
            acc = _dot(h, w_ref[:, pl.ds(pl.multiple_of(j * tn + c0, chunk), chunk)])
            p_ref[:, c0:c0 + chunk] = acc.astype(BF16)
            for c in range(c0 // 128, (c0 + chunk) // 128):
                lanes = slice(c * 128 - c0, (c + 1) * 128 - c0)
                src, prev_d = acc_scr[0], 1
                src[c] = acc[:, lanes]
                for lvl, (ref, d) in enumerate(zip(res_refs, RESIDUE_DILATIONS)):
                    ratio, n = d // prev_d, tm // d
                    dst = acc_scr[lvl + 1] if lvl + 1 < len(RESIDUE_DILATIONS) else None
                    for rp in range(prev_d):
                        for a in range(ratio):
                            r = rp + prev_d * a
                            rows = src[c, pl.ds(rp * (tm // prev_d) + a, n, stride=ratio), :]
                            ref[:, r * tn + c * 128:r * tn + (c + 1) * 128] = rows.astype(BF16)
                            if dst is not None:
                                dst[c, r * n:(r + 1) * n, :] = rows
                    src, prev_d = dst, d


def _inproj(x2, scale, shift, w_main, seq):
    M, D = x2.shape
    tm, tn = 512, ATT_WIDTH
    bpt = seq // tm
    last_qkv = ATT_QKV_TILES - 1
    res_specs = [pl.BlockSpec((tm // d, d * tn), lambda i, j: (i, jnp.minimum(j, last_qkv)))
                 for d in RESIDUE_DILATIONS]
    res_shapes = [jax.ShapeDtypeStruct((M // d, ATT_QKV_TILES * d * tn), BF16)
                  for d in RESIDUE_DILATIONS]
    return pl.pallas_call(
        functools.partial(_inproj_kernel, tm=tm, tn=tn),
        grid=(M // tm, PROJ_WIDTH // tn),
        in_specs=[pl.BlockSpec((tm, D), lambda i, j: (i, 0)),
                  pl.BlockSpec((None, 1, D), lambda i, j: (i // bpt, 0, 0)),
                  pl.BlockSpec((None, 1, D), lambda i, j: (i // bpt, 0, 0)),
                  pl.BlockSpec(w_main.shape, lambda i, j: (0, 0), pipeline_mode=pl.Buffered(1))],
        out_specs=[pl.BlockSpec((tm, tn), lambda i, j: (i, j)),
                   pl.BlockSpec((tm, LR_PAD), lambda i, j: (i, 0))] + res_specs,
        out_shape=[jax.ShapeDtypeStruct((M, PROJ_WIDTH), BF16),
                   jax.ShapeDtypeStruct((M, LR_PAD), F32)] + res_shapes,
        scratch_shapes=[pltpu.VMEM((tm, D), BF16)]
                       + [pltpu.VMEM((tn // 128, tm, 128), F32) for _ in RESIDUE_DILATIONS],
        compiler_params=_params(("arbitrary", "arbitrary")),
        name="inproj",
    )(x2, scale, shift, w_main)


def _t5_bucket_np(rel):
    nb = REL_BUCKETS // 2
    max_exact = nb // 2
    n = np.abs(rel)
    large = max_exact + (np.log(np.maximum(n, 1) / max_exact)
                         / np.log(REL_MAX_DIST / max_exact) * (nb - max_exact)).astype(np.int32)
    large = np.minimum(large, nb - 1)
    return (np.where(rel > 0, nb, 0) + np.where(n < max_exact, n, large)).astype(np.int32)


def _bias_kernel(rbt_ref, bucket_ref, mask_ref, o_ref):
    rbt = rbt_ref[...]
    bucket = bucket_ref[...]
    ids = lax.broadcasted_iota(jnp.int32, (REL_BUCKETS, bucket.shape[1]), 0)
    onehot = jnp.where(ids == bucket, 1.0, 0.0).astype(BF16)
    hi = rbt.astype(BF16)
    rest = rbt - hi.astype(F32)
    mid = rest.astype(BF16)
    lo = (rest - mid.astype(F32)).astype(BF16)
    tbl = _dot(hi, onehot) + _dot(mid, onehot) + _dot(lo, onehot)
    for v in range(3):
        o_ref[v] = jnp.where(mask_ref[v] > 0.5, tbl, NEG_INF)


def _bias_tiles(rel_bias, dilation):
    w, tq, tk = ATT_STEPS, ATT_TQ, ATT_TK
    qi = np.arange(tq)[:, None]
    kj = np.arange(tk)[None, :]
    step = kj - w - qi
    band = np.abs(step) <= w
    bucket = _t5_bucket_np(step * dilation).reshape(1, tq * tk)
    masks = np.stack([band & (kj >= w), band, band & (kj < tk - w)]).astype(np.float32)
    masks = masks.reshape(3, 1, tq * tk)
    out = pl.pallas_call(
        _bias_kernel,
        out_shape=jax.ShapeDtypeStruct((3, ATT_HEADS, tq * tk), F32),
        compiler_params=pltpu.CompilerParams(vmem_limit_bytes=VMEM_LIMIT),
        name=f"bias_d{dilation}",
    )(rel_bias.T, jnp.asarray(bucket), jnp.asarray(masks))
    return out.reshape(3, ATT_HEADS, tq, tk)


def _attn_kernel(q_ref, kp_ref, km_ref, kn_ref, vp_ref, vm_ref, vn_ref, bias_ref,
                 o_ref, m_ref, den_ref, kbuf, v_slot, s_scr, p_scr, inv_scr, *, tb):
    w, tq, tk = ATT_STEPS, ATT_TQ, ATT_TK
    nsub = tb // tq
    npair = ATT_HEADS // 2
    gh = ATT_GROUP_HEADS
    gw = gh * ATT_HEAD_DIM
    ngroup = ATT_HEADS // gh
    nk = tb + 2 * w
    i = pl.program_id(2)
    first = i == 0
    last = i == pl.num_programs(2) - 1

    lane = lax.broadcasted_iota(jnp.int32, (1, ATT_WIDTH), 1)
    slot = (lane % gw) // ATT_HEAD_DIM
    row = 0
    for kpart, vpart in ((kp_ref, vp_ref), (km_ref, vm_ref), (kn_ref, vn_ref)):
        rows = slice(row, row + kpart.shape[0])
        kbuf[rows] = kpart[...]
        x = vpart[...]
        for s in range(gh):
            v_slot[s, rows] = jnp.where(slot == s, x, jnp.zeros_like(x))
        row += kpart.shape[0]

    half = tq // 2
    lower_half = lax.broadcasted_iota(jnp.int32, (half, 128), 1) < ATT_HEAD_DIM
    lower_q = lax.broadcasted_iota(jnp.int32, (tq, 128), 1) < ATT_HEAD_DIM

    nsets = ATT_HEAD_SETS

    def head_pairs(hs):
        return range(hs * npair // nsets, (hs + 1) * npair // nsets)

    def tile(j):
        return pl.multiple_of(j * tq, tq)

    def logits(j, buf, hs):
        qs = tile(j)
        for hp in head_pairs(hs):
            cp = slice(hp * 128, (hp + 1) * 128)
            q = q_ref[pl.ds(qs, tq), cp]
            zero = jnp.zeros_like(q)
            q2 = jnp.concatenate([jnp.where(lower_q, q, zero), jnp.where(lower_q, zero, q)], axis=0)
            s2 = _dot_nt(q2, kbuf[pl.ds(qs, tk), cp])
            s_scr[buf, 2 * hp] = s2[:tq]
            s_scr[buf, 2 * hp + 1] = s2[tq:]

    def softmax(j, buf, hs):
        qs = tile(j)
        var = jnp.where(jnp.logical_and(first, j == 0), 0,
                        jnp.where(jnp.logical_and(last, j == nsub - 1), 2, 1))
        if hs == 0:
            m_ref[pl.ds(qs, tq), :] = jnp.zeros((tq, 128), F32)
            den_ref[pl.ds(qs, tq), :] = jnp.ones((tq, 128), F32)
        for hp in head_pairs(hs):
            for r0 in (0, half):
                dens = []
                for h in (2 * hp, 2 * hp + 1):
                    s = s_scr[buf, h, r0:r0 + half, :] + bias_ref[var, h, r0:r0 + half, :]
                    m = jnp.max(s, axis=-1, keepdims=True)
                    p = jnp.exp(s - m)
                    den = jnp.sum(p, axis=-1, keepdims=True)
                    p_scr[buf, h // gh, r0:r0 + half, (h % gh) * tk:(h % gh + 1) * tk] = p.astype(BF16)
                    m_ref[pl.ds(qs + r0, half), h:h + 1] = m
                    den_ref[pl.ds(qs + r0, half), h:h + 1] = den
                    dens.append(den)
                pair = hp % (gh // 2)
                inv_scr[buf, (2 * hp) // gh, r0:r0 + half, pair * 128:(pair + 1) * 128] = (
                    1.0 / jnp.where(lower_half, dens[0], dens[1]))

    def outputs(j, buf, hs):
        qs = tile(j)
        for g in range(hs * ngroup // nsets, (hs + 1) * ngroup // nsets):
            cg = slice(g * gw, (g + 1) * gw)
            v_stack = jnp.concatenate([v_slot[s, pl.ds(qs, tk), cg] for s in range(gh)], axis=0)
            o = _dot(p_scr[buf, g], v_stack)
            o_ref[pl.ds(qs, tq), cg] = (o * inv_scr[buf, g]).astype(BF16)

    for hs in range(nsets):
        def one_tile(j, carry, hs=hs):
            logits(j, 0, hs)
            softmax(j, 0, hs)
            outputs(j, 0, hs)
            return carry

        lax.fori_loop(0, nsub, one_tile, 0)


def _attn_pattern(qkv, bias, batch, seq, dilation):
    w = ATT_STEPS
    L = seq // dilation
    tb = min(1024, L)
    nblk = L // tb
    hb = tb // w
    nhalo = L // w

    def main(j):
        return pl.BlockSpec((None, tb, ATT_WIDTH), lambda b, r, i: (b, i, j * dilation + r))

    def prev(j):
        return pl.BlockSpec((None, w, ATT_WIDTH),
                            lambda b, r, i: (b, jnp.maximum(i * hb - 1, 0), j * dilation + r))

    def nxt(j):
        return pl.BlockSpec((None, w, ATT_WIDTH),
                            lambda b, r, i: (b, jnp.minimum((i + 1) * hb, nhalo - 1), j * dilation + r))

    nk, gh = tb + 2 * w, ATT_GROUP_HEADS
    return pl.pallas_call(
        functools.partial(_attn_kernel, tb=tb),
        grid=(batch, dilation, nblk),
        in_specs=[main(0), prev(1), main(1), nxt(1), prev(2), main(2), nxt(2),
                  pl.BlockSpec((3, ATT_HEADS, ATT_TQ, ATT_TK), lambda b, r, i: (0, 0, 0, 0),
                               pipeline_mode=pl.Buffered(1))],
        out_specs=[pl.BlockSpec((None, tb, ATT_WIDTH), lambda b, r, i: (b, i, r)),
                   pl.BlockSpec((None, tb, 128), lambda b, r, i: (b, i, r)),
                   pl.BlockSpec((None, tb, 128), lambda b, r, i: (b, i, r))],
        out_shape=[jax.ShapeDtypeStruct((batch, L, dilation * ATT_WIDTH), BF16),
                   jax.ShapeDtypeStruct((batch, L, dilation * 128), F32),
                   jax.ShapeDtypeStruct((batch, L, dilation * 128), F32)],
        scratch_shapes=[pltpu.VMEM((nk, ATT_WIDTH), BF16),
                        pltpu.VMEM((gh, nk, ATT_WIDTH), BF16),
                        pltpu.VMEM((1, ATT_HEADS, ATT_TQ, ATT_TK), F32),
                        pltpu.VMEM((1, ATT_HEADS // gh, ATT_TQ, gh * ATT_TK), BF16),
                        pltpu.VMEM((1, ATT_HEADS // gh, ATT_TQ, gh * ATT_HEAD_DIM), F32)],
        compiler_params=_params(("arbitrary", "arbitrary", "arbitrary")),
        name=f"attn_d{dilation}",
    )(qkv, qkv, qkv, qkv, qkv, qkv, qkv, bias)


def _merge_kernel(*refs, tm):
    nres = len(RESIDUE_DILATIONS)
    o1_ref, m1_ref, d1_ref = refs[:3]
    res = [refs[3 + 3 * n:6 + 3 * n] for n in range(nres)]
    ag_ref, e_ref, out_ref = refs[3 + 3 * nres:6 + 3 * nres]
    scr = [refs[6 + 3 * nres + 3 * n:9 + 3 * nres + 3 * n] for n in range(nres)]
    tmp = refs[6 + 6 * nres]

    ncol = ATT_WIDTH // 128
    for d, (o_ref, m_ref, d_ref), (so, sm, sd) in zip(RESIDUE_DILATIONS, res, scr):
        prev = d // 4 if d > 4 else 1
        for r in range(d):
            rp, a = r % prev, r // prev
            for c in range(ncol):
                col = r * ATT_WIDTH + c * 128
                rows = o_ref[:, col:col + 128].astype(F32)
                if prev == 1:
                    so[c, pl.ds(r, tm // d, stride=d), :] = rows
                else:
                    tmp[c, pl.ds(rp * (tm // prev) + a, tm // d, stride=d // prev), :] = rows
            sm[pl.ds(r, tm // d, stride=d), :] = m_ref[:, r * 128:(r + 1) * 128]
            sd[pl.ds(r, tm // d, stride=d), :] = d_ref[:, r * 128:(r + 1) * 128]
        if prev > 1:
            n = tm // prev
            for rp in range(prev):
                for c in range(ncol):
                    so[c, pl.ds(rp, n, stride=prev), :] = tmp[c, rp * n:(rp + 1) * n, :]

    rc, gw = 128, 256
    head_lane = lax.broadcasted_iota(jnp.int32, (rc, 128), 1) < ATT_HEADS
    for r0 in range(0, tm, rc):
        rows = slice(r0, r0 + rc)
        lses = ([m1_ref[rows, :] + jnp.log(d1_ref[rows, :])]
                + [sm[rows, :] + jnp.log(sd[rows, :]) for _, sm, sd in scr])
        mx = functools.reduce(jnp.maximum, lses)
        es = [jnp.exp(l - mx) for l in lses]
        inv = 1.0 / functools.reduce(jnp.add, es)
        packed = []
        for ei in es:
            wgt = jnp.where(head_lane, ei * inv, 0.0)
            hi = wgt.astype(BF16).astype(F32)
            lo = (wgt - hi).astype(BF16).astype(F32)
            packed.append((hi + pltpu.roll(lo, ATT_HEADS, 1)).astype(BF16))
        for c0 in range(0, ATT_WIDTH, gw):
            cols = slice(c0, c0 + gw)
            e = e_ref[:, cols]
            outs = [o1_ref[rows, cols].astype(F32)] + [
                jnp.concatenate([so[c, rows, :] for c in range(c0 // 128, (c0 + gw) // 128)], axis=1)
                for so, _, _ in scr]
            att = functools.reduce(jnp.add, [_dot(w, e) * o for w, o in zip(packed, outs)])
            out_ref[rows, cols] = (att * _silu(ag_ref[rows, cols].astype(F32))).astype(BF16)


def _merge(nat, res, proj):
    M = proj.shape[0]
    tm = 512
    expand = np.zeros((128, ATT_WIDTH), np.float32)
    for h in range(ATT_HEADS):
        expand[h, h * ATT_HEAD_DIM:(h + 1) * ATT_HEAD_DIM] = 1.0
        expand[ATT_HEADS + h, h * ATT_HEAD_DIM:(h + 1) * ATT_HEAD_DIM] = 1.0
    row = lambda rows, width: pl.BlockSpec((rows, width), lambda i: (i, 0))
    triple = lambda d: [row(tm // d, d * ATT_WIDTH), row(tm // d, d * 128), row(tm // d, d * 128)]
    in_specs = (triple(1) + [s for d in RESIDUE_DILATIONS for s in triple(d)]
                + [pl.BlockSpec((tm, ATT_WIDTH), lambda i: (i, COL_AG // ATT_WIDTH)),
                   pl.BlockSpec((128, ATT_WIDTH), lambda i: (0, 0))])
    scratch = []
    for _ in RESIDUE_DILATIONS:
        scratch += [pltpu.VMEM((ATT_WIDTH // 128, tm, 128), F32),
                    pltpu.VMEM((tm, 128), F32), pltpu.VMEM((tm, 128), F32)]
    scratch.append(pltpu.VMEM((ATT_WIDTH // 128, tm, 128), F32))
    return pl.pallas_call(
        functools.partial(_merge_kernel, tm=tm),
        grid=(M // tm,),
        in_specs=in_specs,
        out_specs=row(tm, ATT_WIDTH),
        out_shape=jax.ShapeDtypeStruct((M, ATT_WIDTH), BF16),
        scratch_shapes=scratch,
        compiler_params=_params(("arbitrary",)),
        name="merge",
    )(*nat, *[a for t in res for a in t], proj, jnp.asarray(expand, BF16))


def _gla_kernel(*refs, ts, nstep):
    fwd_in, bwd_in = refs[:5], refs[5:10]
    (upf_ref, upb_ref, gbf_ref, gbb_ref, gain_ref, out_ref,
     state, o_acc, qf_scr, kd_scr, ks_scr, oin_scr, st_scr) = refs[10:]
    C = GLA_CHUNK
    nchunk = ts // C
    chunks = [slice(c * C, (c + 1) * C) for c in range(nchunk)]
    i = pl.program_id(2)
    dirs = [(0, False, fwd_in, upf_ref, gbf_ref), (1, True, bwd_in, upb_ref, gbb_ref)]

    @pl.when(i == 0)
    def _():
        state[...] = jnp.zeros_like(state)


    log_gs = []
    for d, reverse, (q_ref, k_ref, v_ref, lr_ref, gg_ref), up_ref, gb_ref in dirs:
        lr_hi, lr_lo = _split_bf16(lr_ref[...])
        up_hi, up_lo = _split_bf16(up_ref[...])
        z = _dot(lr_hi, up_hi) + _dot(lr_hi, up_lo) + _dot(lr_lo, up_hi) + gb_ref[...]
        log_gs.append((jnp.minimum(z, 0.0) - jnp.log(1.0 + jnp.exp(-jnp.abs(z))))
                      * (1.0 / GLA_GATE_NORM))

    ri = lax.broadcasted_iota(jnp.int32, (C, C), 0)
    ci = lax.broadcasted_iota(jnp.int32, (C, C), 1)
    dec_cols = [[], []]
    for d, reverse, (q_ref, k_ref, v_ref, lr_ref, gg_ref), up_ref, gb_ref in dirs:
        tri = jnp.where((ci >= ri) if reverse else (ci <= ri), 1.0, 0.0).astype(BF16)
        for rows in chunks:
            g_hi, g_lo = _split_bf16(log_gs[d][rows])
            cum = _dot(tri, jnp.concatenate([g_hi, g_lo], axis=1))
            b = cum[:, :GLA_DK] + cum[:, GLA_DK:]
            b_edge = b[0:1] if reverse else b[C - 1:C]
            q = q_ref[rows, :].astype(F32)
            k = k_ref[rows, :].astype(F32)
            qf_scr[d, rows, :] = (q * jnp.exp(b) * (GLA_DK ** -0.5)).astype(BF16)
            kd_scr[d, rows, :] = (k * jnp.exp(-b)).astype(BF16)
            ks_scr[d, rows, :] = (k * jnp.exp(b_edge - b)).astype(BF16)
            dec = jnp.broadcast_to(jnp.exp(b_edge), (GLA_DK, GLA_DK)).T
            dec_cols[d].append(jnp.concatenate([dec, dec], axis=1))

    atts = [[jnp.where((ci >= ri) if reverse else (ci <= ri),
                       _dot_nt(qf_scr[d, rows, :], kd_scr[d, rows, :]), 0.0).astype(BF16)
             for rows in chunks] for d, reverse, *_ in dirs]

    kvs = [[], []]
    for d, reverse, (q_ref, k_ref, v_ref, lr_ref, gg_ref), up_ref, gb_ref in dirs:
        for rows, att in zip(chunks, atts[d]):
            v = v_ref[rows, :]
            oin_scr[d, rows, :] = _dot(att, v)
            kvs[d].append(_dot_tn(ks_scr[d, rows, :], v))

    orders = [list(range(nchunk)), list(range(nchunk - 1, -1, -1))]
    for d, reverse, *_ in dirs:
        st = state[d]
        for c in orders[d]:
            st_scr[d, c] = st.astype(BF16)
            st = st * dec_cols[d][c] + kvs[d][c]
        state[d] = st

    for d, reverse, *_ in dirs:
        for c in orders[d]:
            rows = chunks[c]
            oin_scr[d, rows, :] = oin_scr[d, rows, :] + _dot(qf_scr[d, rows, :], st_scr[d, c])

    blocks = [i, nstep - 1 - i]

    @pl.when(i < nstep // 2)
    def _():
        for d, reverse, *_ in dirs:
            base = pl.multiple_of(blocks[d] * ts, ts)
            for rows in chunks:
                o_acc[pl.ds(base + rows.start, C), :] = oin_scr[d, rows, :]

    @pl.when(i >= nstep // 2)
    def _():
        for d, reverse, (q_ref, k_ref, v_ref, lr_ref, gg_ref), up_ref, gb_ref in dirs:
            base = pl.multiple_of(blocks[d] * ts, ts)
            for rows in chunks:
                dst = pl.ds(base + rows.start, C)
                tot = oin_scr[d, rows, :] + o_acc[dst, :]
                ms = jnp.mean(tot * tot, axis=-1, keepdims=True)
                g_o = tot * lax.rsqrt(ms + EPS) * gain_ref[...]
                out_ref[dst, :] = (g_o * _silu(gg_ref[rows, :].astype(F32))).astype(BF16)


def _gla(proj, lr, up_f, up_b, bias_f, bias_b, gain, batch, seq):
    ts = 1024
    nstep = seq // ts
    assert nstep % 2 == 0
    C = GLA_CHUNK
    p3 = proj.reshape(batch, seq, PROJ_WIDTH)
    lr3 = lr.reshape(batch, seq, LR_PAD)

    def direction_specs(step):
        def seq_block(width, col0):
            return pl.BlockSpec((None, ts, width), lambda b, h, i: (b, step(i), col0 // width + h))
        return [seq_block(GLA_DK, COL_GQ), seq_block(GLA_DK, COL_GK), seq_block(GLA_DV, COL_GV),
                pl.BlockSpec((None, ts, LR_PAD), lambda b, h, i: (b, step(i), 0)),
                seq_block(GLA_DV, COL_GG)]

    per_head = lambda rows, width: pl.BlockSpec((rows, width), lambda b, h, i: (0, h))
    in_specs = (direction_specs(lambda i: i) + direction_specs(lambda i: nstep - 1 - i)
                + [per_head(LR_PAD, GLA_DK), per_head(LR_PAD, GLA_DK),
                   per_head(1, GLA_DK), per_head(1, GLA_DK), per_head(1, GLA_DV)])
    dir_args = [p3, p3, p3, lr3, p3]
    return pl.pallas_call(
        functools.partial(_gla_kernel, ts=ts, nstep=nstep),
        grid=(batch, GLA_HEADS, nstep),
        in_specs=in_specs,
        out_specs=pl.BlockSpec((None, seq, GLA_DV), lambda b, h, i: (b, 0, h)),
        out_shape=jax.ShapeDtypeStruct((batch, seq, GLA_WIDTH), BF16),
        scratch_shapes=[pltpu.VMEM((2, GLA_DK, GLA_DV), F32),
                        pltpu.VMEM((seq, GLA_DV), F32),
                        pltpu.VMEM((2, ts, GLA_DK), BF16),
                        pltpu.VMEM((2, ts, GLA_DK), BF16),
                        pltpu.VMEM((2, ts, GLA_DK), BF16),
                        pltpu.VMEM((2, ts, GLA_DV), F32),
                        pltpu.VMEM((2, ts // C, GLA_DK, GLA_DV), BF16)],
        compiler_params=_params(("arbitrary", "arbitrary", "arbitrary")),
        name="gla",
    )(*dir_args, *dir_args, up_f, up_b, bias_f.reshape(1, GLA_KEY_WIDTH),
      bias_b.reshape(1, GLA_KEY_WIDTH), gain.reshape(1, GLA_WIDTH))


def _outproj_kernel(a_ref, g_ref, wa_ref, wg_ref, x_ref, gate_ref, fg_ref, o_ref, *, final):
    y = _dot(a_ref[...], wa_ref[...]) + _dot(g_ref[...], wg_ref[...])
    xn = x_ref[...] + gate_ref[...] * y
    if final:
        ms = jnp.mean(xn * xn, axis=-1, keepdims=True)
        xn = xn * lax.rsqrt(ms + EPS) * fg_ref[...]
    o_ref[...] = xn


def _outproj(a_out, g_out, w_out_bf16, x2, gate, final_gain, seq, final):
    M, D = x2.shape
    tm = 512
    bpt = seq // tm
    return pl.pallas_call(
        functools.partial(_outproj_kernel, final=final),
        grid=(M // tm,),
        in_specs=[pl.BlockSpec((tm, ATT_WIDTH), lambda i: (i, 0)),
                  pl.BlockSpec((tm, GLA_WIDTH), lambda i: (i, 0)),
                  pl.BlockSpec((ATT_WIDTH, D), lambda i: (0, 0)),
                  pl.BlockSpec((GLA_WIDTH, D), lambda i: (1, 0)),
                  pl.BlockSpec((tm, D), lambda i: (i, 0)),
                  pl.BlockSpec((None, 1, D), lambda i: (i // bpt, 0, 0)),
                  pl.BlockSpec((1, D), lambda i: (0, 0))],
        out_specs=pl.BlockSpec((tm, D), lambda i: (i, 0)),
        out_shape=jax.ShapeDtypeStruct((M, D), F32),
        compiler_params=_params(("arbitrary",)),
        name="outproj",
    )(a_out, g_out, w_out_bf16, w_out_bf16, x2, gate, final_gain.reshape(1, D))


def kernel(x, c, w_cond, b_cond, w_in, gla_gate_up_fwd, gla_gate_bias_fwd, gla_gate_up_bwd,
           gla_gate_bias_bwd, gla_norm_gain, rel_bias, w_out, final_gain):
    B, S, D = x.shape
    depth = w_cond.shape[0]
    R = GLA_GATE_RANK
    xs = x.reshape(B * S, D)
    for layer in range(depth):
        mod = _mod(c, w_cond[layer], b_cond[layer])
        shift, scale, gate = [m.reshape(B, 1, D) for m in jnp.split(mod, 3, axis=-1)]

        proj, lr, *res_qkv = _inproj(xs, scale, shift, _wprep(w_in, layer), S)

        def rows2d(t, d):
            return [a.reshape(B * S // d, -1) for a in t]

        nat = rows2d(_attn_pattern(proj.reshape(B, S, PROJ_WIDTH), _bias_tiles(rel_bias, 1), B, S, 1), 1)
        res = [rows2d(_attn_pattern(qkv.reshape(B, S // d, -1), _bias_tiles(rel_bias, d), B, S, d), d)
               for d, qkv in zip(RESIDUE_DILATIONS, res_qkv)]
        a_out = _merge(nat, res, proj)

        up_f = jnp.pad(gla_gate_up_fwd[layer], ((0, LR_PAD - R), (0, 0)))
        up_b = jnp.pad(gla_gate_up_bwd[layer], ((R, LR_PAD - 2 * R), (0, 0)))
        g_out = _gla(proj, lr, up_f, up_b, gla_gate_bias_fwd[layer], gla_gate_bias_bwd[layer],
                     gla_norm_gain[layer], B, S)

        xs = _outproj(a_out, g_out.reshape(B * S, GLA_WIDTH), w_out[layer].astype(BF16),
                      xs, gate, final_gain, S, final=layer == depth - 1)
    return xs.reshape(B, S, D)
```

```python
import functools
import math

import jax
import jax.numpy as jnp
import numpy as np
from jax import lax
from jax.experimental import pallas as pl
from jax.experimental.pallas import tpu as pltpu

D_MODEL = 2048
ATT_WIDTH = 1024
ATT_HEADS = 16
ATT_HEAD_DIM = 64
DILATED_PATTERNS = ((128, 1), (512, 4), (2048, 16))
ATT_STEPS = 64
GLA_WIDTH = 1024
GLA_HEADS = 4
GLA_KEY_WIDTH = 512
GLA_DK = 128
GLA_DV = 256
GLA_GATE_RANK = 16
GLA_GATE_NORM = 16.0
GLA_CHUNK = 64
REL_BUCKETS = 32
REL_MAX_DIST = 1024
EPS = 1e-6
NEG_INF = -1e30

PROJ_WIDTH = 4 * ATT_WIDTH + 2 * GLA_KEY_WIDTH + 2 * GLA_WIDTH
COL_AQ, COL_AK, COL_AV, COL_AG = 0, 1024, 2048, 3072
COL_GQ, COL_GK, COL_GV, COL_GG = 4096, 4608, 5120, 6144
LR_PAD = 128
ATT_QKV_TILES = 3
RESIDUE_DILATIONS = tuple(d for _, d in DILATED_PATTERNS if d > 1)
ATT_TQ = 128
ATT_TK = ATT_TQ + 2 * ATT_STEPS
ATT_GROUP_HEADS = 4
ATT_HEAD_SETS = 1

VMEM_LIMIT = 56 * 1024 * 1024

BF16 = jnp.bfloat16
F32 = jnp.float32


def _params(sem):
    return pltpu.CompilerParams(dimension_semantics=sem, vmem_limit_bytes=VMEM_LIMIT)


def _dot(a, b):
    return jnp.dot(a, b, preferred_element_type=F32)


def _dot_nt(a, b):
    return lax.dot_general(a, b, (((1,), (1,)), ((), ())), preferred_element_type=F32)


def _dot_tn(a, b):
    return lax.dot_general(a, b, (((0,), (0,)), ((), ())), preferred_element_type=F32)


def _split_bf16(x):
    hi = x.astype(BF16)
    lo = (x - hi.astype(F32)).astype(BF16)
    return hi, lo


def _silu(x):
    return x / (1.0 + jnp.exp(-x))


def _mod_kernel(c_ref, w_ref, b_ref, o_ref):
    s_hi, s_lo = _split_bf16(_silu(c_ref[...]))
    w_hi, w_lo = _split_bf16(w_ref[...])
    o_ref[...] = _dot(s_hi, w_hi) + _dot(s_lo, w_hi) + _dot(s_hi, w_lo) + b_ref[...]


def _mod(c, w_cond, b_cond):
    B, D = c.shape
    N = w_cond.shape[1]
    tn = 768
    cp = jnp.pad(c, ((0, 8 - B), (0, 0)))
    out = pl.pallas_call(
        _mod_kernel,
        grid=(N // tn,),
        in_specs=[pl.BlockSpec((8, D), lambda j: (0, 0)),
                  pl.BlockSpec((D, tn), lambda j: (0, j)),
                  pl.BlockSpec((1, tn), lambda j: (0, j))],
        out_specs=pl.BlockSpec((8, tn), lambda j: (0, j)),
        out_shape=jax.ShapeDtypeStruct((8, N), F32),
        compiler_params=_params(("arbitrary",)),
        name="mod",
    )(cp, w_cond, b_cond.reshape(1, N))
    return out[:B]


def _wprep_kernel(w_ref, o_ref, *, tn, ncols):
    col = pl.program_id(0) * tn + lax.broadcasted_iota(jnp.int32, (1, tn), 1)
    scale = jnp.where(col < ATT_WIDTH, ATT_HEAD_DIM ** -0.5, 1.0)
    o_ref[...] = jnp.where(col < ncols, w_ref[...] * scale, 0.0).astype(BF16)


def _wprep(w_all, layer):
    _, D, ncols = w_all.shape
    width = PROJ_WIDTH + LR_PAD
    tn = 384
    assert width % tn == 0
    return pl.pallas_call(
        functools.partial(_wprep_kernel, tn=tn, ncols=ncols),
        grid=(width // tn,),
        in_specs=[pl.BlockSpec((None, D, tn), lambda j: (layer, 0, j))],
        out_specs=pl.BlockSpec((D, tn), lambda j: (0, j)),
        out_shape=jax.ShapeDtypeStruct((D, width), BF16),
        compiler_params=_params(("arbitrary",)),
        name="wprep",
    )(w_all)


def _inproj_kernel(x_ref, scale_ref, shift_ref, w_ref, p_ref, lr_ref, *rest, tm, tn):
    nres = len(RESIDUE_DILATIONS)
    res_refs, h_scr, acc_scr = rest[:nres], rest[nres], rest[nres + 1:]
    j = pl.program_id(1)

    @pl.when(j == 0)
    def _():
        x = x_ref[...]
        ms = jnp.mean(x * x, axis=-1, keepdims=True)
        h = x * lax.rsqrt(ms + EPS) * (1.0 + scale_ref[...]) + shift_ref[...]
        hb = h.astype(BF16)
        h_scr[...] = hb
        lr_ref[...] = _dot(hb, w_ref[:, PROJ_WIDTH:PROJ_WIDTH + LR_PAD])

    @pl.when(j >= ATT_QKV_TILES)
    def _():
        w = w_ref[:, pl.ds(pl.multiple_of(j * tn, tn), tn)]
        p_ref[...] = _dot(h_scr[...], w).astype(BF16)

    @pl.when(j < ATT_QKV_TILES)
    def _():
        h = h_scr[...]
        chunk = 256
        for c0 in range(0, tn, chunk):
            acc = _dot(h, w_ref[:, pl.ds(pl.multiple_of(j * tn + c0, chunk), chunk)])
            p_ref[:, c0:c0 + chunk] = acc.astype(BF16)
            for c in range(c0 // 128, (c0 + chunk) // 128):
                lanes = slice(c * 128 - c0, (c + 1) * 128 - c0)
                src, prev_d = acc_scr[0], 1
                src[c] = acc[:, lanes]
                for lvl, (ref, d) in enumerate(zip(res_refs, RESIDUE_DILATIONS)):
                    ratio, n = d // prev_d, tm // d
                    dst = acc_scr[lvl + 1] if lvl + 1 < len(RESIDUE_DILATIONS) else None
                    for rp in range(prev_d):
                        for a in range(ratio):
                            r = rp + prev_d * a
                            rows = src[c, pl.ds(rp * (tm // prev_d) + a, n, stride=ratio), :]
                            ref[:, r * tn + c * 128:r * tn + (c + 1) * 128] = rows.astype(BF16)
                            if dst is not None:
                                dst[c, r * n:(r + 1) * n, :] = rows
                    src, prev_d = dst, d


def _inproj(x2, scale, shift, w_main, seq):
    M, D = x2.shape
    tm, tn = 512, ATT_WIDTH
    bpt = seq // tm
    last_qkv = ATT_QKV_TILES - 1
    res_specs = [pl.BlockSpec((tm // d, d * tn), lambda i, j: (i, jnp.minimum(j, last_qkv)))
                 for d in RESIDUE_DILATIONS]
    res_shapes = [jax.ShapeDtypeStruct((M // d, ATT_QKV_TILES * d * tn), BF16)
                  for d in RESIDUE_DILATIONS]
    return pl.pallas_call(
        functools.partial(_inproj_kernel, tm=tm, tn=tn),
        grid=(M // tm, PROJ_WIDTH // tn),
        in_specs=[pl.BlockSpec((tm, D), lambda i, j: (i, 0)),
                  pl.BlockSpec((None, 1, D), lambda i, j: (i // bpt, 0, 0)),
                  pl.BlockSpec((None, 1, D), lambda i, j: (i // bpt, 0, 0)),
                  pl.BlockSpec(w_main.shape, lambda i, j: (0, 0), pipeline_mode=pl.Buffered(1))],
        out_specs=[pl.BlockSpec((tm, tn), lambda i, j: (i, j)),
                   pl.BlockSpec((tm, LR_PAD), lambda i, j: (i, 0))] + res_specs,
        out_shape=[jax.ShapeDtypeStruct((M, PROJ_WIDTH), BF16),
                   jax.ShapeDtypeStruct((M, LR_PAD), F32)] + res_shapes,
        scratch_shapes=[pltpu.VMEM((tm, D), BF16)]
                       + [pltpu.VMEM((tn // 128, tm, 128), F32) for _ in RESIDUE_DILATIONS],
        compiler_params=_params(("arbitrary", "arbitrary")),
        name="inproj",
    )(x2, scale, shift, w_main)


def _t5_bucket_np(rel):
    nb = REL_BUCKETS // 2
    max_exact = nb // 2
    n = np.abs(rel)
    large = max_exact + (np.log(np.maximum(n, 1) / max_exact)
                         / np.log(REL_MAX_DIST / max_exact) * (nb - max_exact)).astype(np.int32)
    large = np.minimum(large, nb - 1)
    return (np.where(rel > 0, nb, 0) + np.where(n < max_exact, n, large)).astype(np.int32)


def _bias_kernel(rbt_ref, bucket_ref, mask_ref, o_ref):
    rbt = rbt_ref[...]
    bucket = bucket_ref[...]
    ids = lax.broadcasted_iota(jnp.int32, (REL_BUCKETS, bucket.shape[1]), 0)
    onehot = jnp.where(ids == bucket, 1.0, 0.0).astype(BF16)
    hi = rbt.astype(BF16)
    rest = rbt - hi.astype(F32)
    mid = rest.astype(BF16)
    lo = (rest - mid.astype(F32)).astype(BF16)
    tbl = _dot(hi, onehot) + _dot(mid, onehot) + _dot(lo, onehot)
    for v in range(3):
        o_ref[v] = jnp.where(mask_ref[v] > 0.5, tbl, NEG_INF)


def _bias_tiles(rel_bias, dilation):
    w, tq, tk = ATT_STEPS, ATT_TQ, ATT_TK
    qi = np.arange(tq)[:, None]
    kj = np.arange(tk)[None, :]
    step = kj - w - qi
    band = np.abs(step) <= w
    bucket = _t5_bucket_np(step * dilation).reshape(1, tq * tk)
    masks = np.stack([band & (kj >= w), band, band & (kj < tk - w)]).astype(np.float32)
    masks = masks.reshape(3, 1, tq * tk)
    out = pl.pallas_call(
        _bias_kernel,
        out_shape=jax.ShapeDtypeStruct((3, ATT_HEADS, tq * tk), F32),
        compiler_params=pltpu.CompilerParams(vmem_limit_bytes=VMEM_LIMIT),
        name=f"bias_d{dilation}",
    )(rel_bias.T, jnp.asarray(bucket), jnp.asarray(masks))
    return out.reshape(3, ATT_HEADS, tq, tk)


def _attn_kernel(q_ref, kp_ref, km_ref, kn_ref, vp_ref, vm_ref, vn_ref, bias_ref,
                 o_ref, m_ref, den_ref, kbuf, v_slot, s_scr, p_scr, inv_scr, *, tb):
    w, tq, tk = ATT_STEPS, ATT_TQ, ATT_TK
    nsub = tb // tq
    npair = ATT_HEADS // 2
    gh = ATT_GROUP_HEADS
    gw = gh * ATT_HEAD_DIM
    ngroup = ATT_HEADS // gh
    nk = tb + 2 * w
    i = pl.program_id(2)
    first = i == 0
    last = i == pl.num_programs(2) - 1

    lane = lax.broadcasted_iota(jnp.int32, (1, ATT_WIDTH), 1)
    slot = (lane % gw) // ATT_HEAD_DIM
    row = 0
    for kpart, vpart in ((kp_ref, vp_ref), (km_ref, vm_ref), (kn_ref, vn_ref)):
        rows = slice(row, row + kpart.shape[0])
        kbuf[rows] = kpart[...]
        x = vpart[...]
        for s in range(gh):
            v_slot[s, rows] = jnp.where(slot == s, x, jnp.zeros_like(x))
        row += kpart.shape[0]

    half = tq // 2
    lower_half = lax.broadcasted_iota(jnp.int32, (half, 128), 1) < ATT_HEAD_DIM
    lower_q = lax.broadcasted_iota(jnp.int32, (tq, 128), 1) < ATT_HEAD_DIM

    nsets = ATT_HEAD_SETS

    def head_pairs(hs):
        return range(hs * npair // nsets, (hs + 1) * npair // nsets)

    def tile(j):
        return pl.multiple_of(j * tq, tq)

    def logits(j, buf, hs):
        qs = tile(j)
        for hp in head_pairs(hs):
            cp = slice(hp * 128, (hp + 1) * 128)
            q = q_ref[pl.ds(qs, tq), cp]
            zero = jnp.zeros_like(q)
            q2 = jnp.concatenate([jnp.where(lower_q, q, zero), jnp.where(lower_q, zero, q)], axis=0)
            s2 = _dot_nt(q2, kbuf[pl.ds(qs, tk), cp])
            s_scr[buf, 2 * hp] = s2[:tq]
            s_scr[buf, 2 * hp + 1] = s2[tq:]

    def softmax(j, buf, hs):
        qs = tile(j)
        var = jnp.where(jnp.logical_and(first, j == 0), 0,
                        jnp.where(jnp.logical_and(last, j == nsub - 1), 2, 1))
        if hs == 0:
            m_ref[pl.ds(qs, tq), :] = jnp.zeros((tq, 128), F32)
            den_ref[pl.ds(qs, tq), :] = jnp.ones((tq, 128), F32)
        for hp in head_pairs(hs):
            for r0 in (0, half):
                dens = []
                for h in (2 * hp, 2 * hp + 1):
                    s = s_scr[buf, h, r0:r0 + half, :] + bias_ref[var, h, r0:r0 + half, :]
                    m = jnp.max(s, axis=-1, keepdims=True)
                    p = jnp.exp(s - m)
                    den = jnp.sum(p, axis=-1, keepdims=True)
                    p_scr[buf, h // gh, r0:r0 + half, (h % gh) * tk:(h % gh + 1) * tk] = p.astype(BF16)
                    m_ref[pl.ds(qs + r0, half), h:h + 1] = m
                    den_ref[pl.ds(qs + r0, half), h:h + 1] = den
                    dens.append(den)
                pair = hp % (gh // 2)
                inv_scr[buf, (2 * hp) // gh, r0:r0 + half, pair * 128:(pair + 1) * 128] = (
                    1.0 / jnp.where(lower_half, dens[0], dens[1]))

    def outputs(j, buf, hs):
        qs = tile(j)
        for g in range(hs * ngroup // nsets, (hs + 1) * ngroup // nsets):
            cg = slice(g * gw, (g + 1) * gw)
            v_stack = jnp.concatenate([v_slot[s, pl.ds(qs, tk), cg] for s in range(gh)], axis=0)
            o = _dot(p_scr[buf, g], v_stack)
            o_ref[pl.ds(qs, tq), cg] = (o * inv_scr[buf, g]).astype(BF16)

    for hs in range(nsets):
        def one_tile(j, carry, hs=hs):
            logits(j, 0, hs)
            softmax(j, 0, hs)
            outputs(j, 0, hs)
            return carry

        lax.fori_loop(0, nsub, one_tile, 0)


def _attn_pattern(qkv, bias, batch, seq, dilation):
    w = ATT_STEPS
    L = seq // dilation
    tb = min(1024, L)
    nblk = L // tb
    hb = tb // w
    nhalo = L // w

    def main(j):
        return pl.BlockSpec((None, tb, ATT_WIDTH), lambda b, r, i: (b, i, j * dilation + r))

    def prev(j):
        return pl.BlockSpec((None, w, ATT_WIDTH),
                            lambda b, r, i: (b, jnp.maximum(i * hb - 1, 0), j * dilation + r))

    def nxt(j):
        return pl.BlockSpec((None, w, ATT_WIDTH),
                            lambda b, r, i: (b, jnp.minimum((i + 1) * hb, nhalo - 1), j * dilation + r))

    nk, gh = tb + 2 * w, ATT_GROUP_HEADS
    return pl.pallas_call(
        functools.partial(_attn_kernel, tb=tb),
        grid=(batch, dilation, nblk),
        in_specs=[main(0), prev(1), main(1), nxt(1), prev(2), main(2), nxt(2),
                  pl.BlockSpec((3, ATT_HEADS, ATT_TQ, ATT_TK), lambda b, r, i: (0, 0, 0, 0),
                               pipeline_mode=pl.Buffered(1))],
        out_specs=[pl.BlockSpec((None, tb, ATT_WIDTH), lambda b, r, i: (b, i, r)),
                   pl.BlockSpec((None, tb, 128), lambda b, r, i: (b, i, r)),
                   pl.BlockSpec((None, tb, 128), lambda b, r, i: (b, i, r))],
        out_shape=[jax.ShapeDtypeStruct((batch, L, dilation * ATT_WIDTH), BF16),
                   jax.ShapeDtypeStruct((batch, L, dilation * 128), F32),
                   jax.ShapeDtypeStruct((batch, L, dilation * 128), F32)],
        scratch_shapes=[pltpu.VMEM((nk, ATT_WIDTH), BF16),
                        pltpu.VMEM((gh, nk, ATT_WIDTH), BF16),
                        pltpu.VMEM((1, ATT_HEADS, ATT_TQ, ATT_TK), F32),
                        pltpu.VMEM((1, ATT_HEADS // gh, ATT_TQ, gh * ATT_TK), BF16),
                        pltpu.VMEM((1, ATT_HEADS // gh, ATT_TQ, gh * ATT_HEAD_DIM), F32)],
        compiler_params=_params(("arbitrary", "arbitrary", "arbitrary")),
        name=f"attn_d{dilation}",
    )(qkv, qkv, qkv, qkv, qkv, qkv, qkv, bias)


def _merge_kernel(*refs, tm):
    nres = len(RESIDUE_DILATIONS)
    o1_ref, m1_ref, d1_ref = refs[:3]
    res = [refs[3 + 3 * n:6 + 3 * n] for n in range(nres)]
    ag_ref, e_ref, out_ref = refs[3 + 3 * nres:6 + 3 * nres]
    scr = [refs[6 + 3 * nres + 3 * n:9 + 3 * nres + 3 * n] for n in range(nres)]
    tmp = refs[6 + 6 * nres]

    ncol = ATT_WIDTH // 128
    for d, (o_ref, m_ref, d_ref), (so, sm, sd) in zip(RESIDUE_DILATIONS, res, scr):
        prev = d // 4 if d > 4 else 1
        for r in range(d):
            rp, a = r % prev, r // prev
            for c in range(ncol):
                col = r * ATT_WIDTH + c * 128
                rows = o_ref[:, col:col + 128].astype(F32)
                if prev == 1:
                    so[c, pl.ds(r, tm // d, stride=d), :] = rows
                else:
                    tmp[c, pl.ds(rp * (tm // prev) + a, tm // d, stride=d // prev), :] = rows
            sm[pl.ds(r, tm // d, stride=d), :] = m_ref[:, r * 128:(r + 1) * 128]
            sd[pl.ds(r, tm // d, stride=d), :] = d_ref[:, r * 128:(r + 1) * 128]
        if prev > 1:
            n = tm // prev
            for rp in range(prev):
                for c in range(ncol):
                    so[c, pl.ds(rp, n, stride=prev), :] = tmp[c, rp * n:(rp + 1) * n, :]

    rc, gw = 128, 256
    head_lane = lax.broadcasted_iota(jnp.int32, (rc, 128), 1) < ATT_HEADS
    for r0 in range(0, tm, rc):
        rows = slice(r0, r0 + rc)
        lses = ([m1_ref[rows, :] + jnp.log(d1_ref[rows, :])]
                + [sm[rows, :] + jnp.log(sd[rows, :]) for _, sm, sd in scr])
        mx = functools.reduce(jnp.maximum, lses)
        es = [jnp.exp(l - mx) for l in lses]
        inv = 1.0 / functools.reduce(jnp.add, es)
        packed = []
        for ei in es:
            wgt = jnp.where(head_lane, ei * inv, 0.0)
            hi = wgt.astype(BF16).astype(F32)
            lo = (wgt - hi).astype(BF16).astype(F32)
            packed.append((hi + pltpu.roll(lo, ATT_HEADS, 1)).astype(BF16))
        for c0 in range(0, ATT_WIDTH, gw):
            cols = slice(c0, c0 + gw)
            e = e_ref[:, cols]
            outs = [o1_ref[rows, cols].astype(F32)] + [
                jnp.concatenate([so[c, rows, :] for c in range(c0 // 128, (c0 + gw) // 128)], axis=1)
                for so, _, _ in scr]
            att = functools.reduce(jnp.add, [_dot(w, e) * o for w, o in zip(packed, outs)])
            out_ref[rows, cols] = (att * _silu(ag_ref[rows, cols].astype(F32))).astype(BF16)


def _merge(nat, res, proj):
    M = proj.shape[0]
    tm = 512
    expand = np.zeros((128, ATT_WIDTH), np.float32)
    for h in range(ATT_HEADS):
        expand[h, h * ATT_HEAD_DIM:(h + 1) * ATT_HEAD_DIM] = 1.0
        expand[ATT_HEADS + h, h * ATT_HEAD_DIM:(h + 1) * ATT_HEAD_DIM] = 1.0
    row = lambda rows, width: pl.BlockSpec((rows, width), lambda i: (i, 0))
    triple = lambda d: [row(tm // d, d * ATT_WIDTH), row(tm // d, d * 128), row(tm // d, d * 128)]
    in_specs = (triple(1) + [s for d in RESIDUE_DILATIONS for s in triple(d)]
                + [pl.BlockSpec((tm, ATT_WIDTH), lambda i: (i, COL_AG // ATT_WIDTH)),
                   pl.BlockSpec((128, ATT_WIDTH), lambda i: (0, 0))])
    scratch = []
    for _ in RESIDUE_DILATIONS:
        scratch += [pltpu.VMEM((ATT_WIDTH // 128, tm, 128), F32),
                    pltpu.VMEM((tm, 128), F32), pltpu.VMEM((tm, 128), F32)]
    scratch.append(pltpu.VMEM((ATT_WIDTH // 128, tm, 128), F32))
    return pl.pallas_call(
        functools.partial(_merge_kernel, tm=tm),
        grid=(M // tm,),
        in_specs=in_specs,
        out_specs=row(tm, ATT_WIDTH),
        out_shape=jax.ShapeDtypeStruct((M, ATT_WIDTH), BF16),
        scratch_shapes=scratch,
        compiler_params=_params(("arbitrary",)),
        name="merge",
    )(*nat, *[a for t in res for a in t], proj, jnp.asarray(expand, BF16))


def _gla_kernel(*refs, ts, nstep):
    fwd_in, bwd_in = refs[:5], refs[5:10]
    (upf_ref, upb_ref, gbf_ref, gbb_ref, gain_ref, out_ref,
     state, o_acc, qf_scr, kd_scr, ks_scr, oin_scr, st_scr) = refs[10:]
    C = GLA_CHUNK
    nchunk = ts // C
    chunks = [slice(c * C, (c + 1) * C) for c in range(nchunk)]
    i = pl.program_id(2)
    dirs = [(0, False, fwd_in, upf_ref, gbf_ref), (1, True, bwd_in, upb_ref, gbb_ref)]

    @pl.when(i == 0)
    def _():
        state[...] = jnp.zeros_like(state)


    log_gs = []
    for d, reverse, (q_ref, k_ref, v_ref, lr_ref, gg_ref), up_ref, gb_ref in dirs:
        lr_hi, lr_lo = _split_bf16(lr_ref[...])
        up_hi, up_lo = _split_bf16(up_ref[...])
        z = _dot(lr_hi, up_hi) + _dot(lr_hi, up_lo) + _dot(lr_lo, up_hi) + gb_ref[...]
        log_gs.append((jnp.minimum(z, 0.0) - jnp.log(1.0 + jnp.exp(-jnp.abs(z))))
                      * (1.0 / GLA_GATE_NORM))

    ri = lax.broadcasted_iota(jnp.int32, (C, C), 0)
    ci = lax.broadcasted_iota(jnp.int32, (C, C), 1)
    dec_cols = [[], []]
    for d, reverse, (q_ref, k_ref, v_ref, lr_ref, gg_ref), up_ref, gb_ref in dirs:
        tri = jnp.where((ci >= ri) if reverse else (ci <= ri), 1.0, 0.0).astype(BF16)
        for rows in chunks:
            g_hi, g_lo = _split_bf16(log_gs[d][rows])
            cum = _dot(tri, jnp.concatenate([g_hi, g_lo], axis=1))
            b = cum[:, :GLA_DK] + cum[:, GLA_DK:]
            b_edge = b[0:1] if reverse else b[C - 1:C]
            q = q_ref[rows, :].astype(F32)
            k = k_ref[rows, :].astype(F32)
            edge = jnp.exp(b_edge)
            kd = k * jnp.exp(-b)
            qf_scr[d, rows, :] = (q * jnp.exp(b + math.log(GLA_DK ** -0.5))).astype(BF16)
            kd_scr[d, rows, :] = kd.astype(BF16)
            ks_scr[d, rows, :] = (kd * edge).astype(BF16)
            dec = jnp.broadcast_to(edge, (GLA_DK, GLA_DK)).T
            dec_cols[d].append(jnp.concatenate([dec, dec], axis=1))

    atts = [[jnp.where((ci >= ri) if reverse else (ci <= ri),
                       _dot_nt(qf_scr[d, rows, :], kd_scr[d, rows, :]), 0.0).astype(BF16)
             for rows in chunks] for d, reverse, *_ in dirs]

    kvs = [[], []]
    for d, reverse, (q_ref, k_ref, v_ref, lr_ref, gg_ref), up_ref, gb_ref in dirs:
        for rows, att in zip(chunks, atts[d]):
            v = v_ref[rows, :]
            oin_scr[d, rows, :] = _dot(att, v)
            kvs[d].append(_dot_tn(ks_scr[d, rows, :], v))

    orders = [list(range(nchunk)), list(range(nchunk - 1, -1, -1))]
    for d, reverse, *_ in dirs:
        st = state[d]
        for c in orders[d]:
            st_scr[d, c] = st.astype(BF16)
            st = st * dec_cols[d][c] + kvs[d][c]
        state[d] = st

    for d, reverse, *_ in dirs:
        for c in orders[d]:
            rows = chunks[c]
            oin_scr[d, rows, :] = oin_scr[d, rows, :] + _dot(qf_scr[d, rows, :], st_scr[d, c])

    blocks = [i, nstep - 1 - i]

    @pl.when(i < nstep // 2)
    def _():
        for d, reverse, *_ in dirs:
            base = pl.multiple_of(blocks[d] * ts, ts)
            for rows in chunks:
                o_acc[pl.ds(base + rows.start, C), :] = oin_scr[d, rows, :]

    @pl.when(i >= nstep // 2)
    def _():
        for d, reverse, (q_ref, k_ref, v_ref, lr_ref, gg_ref), up_ref, gb_ref in dirs:
            base = pl.multiple_of(blocks[d] * ts, ts)
            for rows in chunks:
                dst = pl.ds(base + rows.start, C)
                tot = oin_scr[d, rows, :] + o_acc[dst, :]
                ms = jnp.mean(tot * tot, axis=-1, keepdims=True)
                g_o = tot * lax.rsqrt(ms + EPS) * gain_ref[...]
                out_ref[dst, :] = (g_o * _silu(gg_ref[rows, :].astype(F32))).astype(BF16)


def _gla(proj, lr, up_f, up_b, bias_f, bias_b, gain, batch, seq):
    ts = 1024
    nstep = seq // ts
    assert nstep % 2 == 0
    C = GLA_CHUNK
    p3 = proj.reshape(batch, seq, PROJ_WIDTH)
    lr3 = lr.reshape(batch, seq, LR_PAD)

    def direction_specs(step):
        def seq_block(width, col0):
            return pl.BlockSpec((None, ts, width), lambda b, h, i: (b, step(i), col0 // width + h))
        return [seq_block(GLA_DK, COL_GQ), seq_block(GLA_DK, COL_GK), seq_block(GLA_DV, COL_GV),
                pl.BlockSpec((None, ts, LR_PAD), lambda b, h, i: (b, step(i), 0)),
                seq_block(GLA_DV, COL_GG)]

    per_head = lambda rows, width: pl.BlockSpec((rows, width), lambda b, h, i: (0, h))
    in_specs = (direction_specs(lambda i: i) + direction_specs(lambda i: nstep - 1 - i)
                + [per_head(LR_PAD, GLA_DK), per_head(LR_PAD, GLA_DK),
                   per_head(1, GLA_DK), per_head(1, GLA_DK), per_head(1, GLA_DV)])
    dir_args = [p3, p3, p3, lr3, p3]
    return pl.pallas_call(
        functools.partial(_gla_kernel, ts=ts, nstep=nstep),
        grid=(batch, GLA_HEADS, nstep),
        in_specs=in_specs,
        out_specs=pl.BlockSpec((None, seq, GLA_DV), lambda b, h, i: (b, 0, h)),
        out_shape=jax.ShapeDtypeStruct((batch, seq, GLA_WIDTH), BF16),
        scratch_shapes=[pltpu.VMEM((2, GLA_DK, GLA_DV), F32),
                        pltpu.VMEM((seq, GLA_DV), F32),
                        pltpu.VMEM((2, ts, GLA_DK), BF16),
                        pltpu.VMEM((2, ts, GLA_DK), BF16),
                        pltpu.VMEM((2, ts, GLA_DK), BF16),
                        pltpu.VMEM((2, ts, GLA_DV), F32),
                        pltpu.VMEM((2, ts // C, GLA_DK, GLA_DV), BF16)],
        compiler_params=_params(("arbitrary", "arbitrary", "arbitrary")),
        name="gla",
    )(*dir_args, *dir_args, up_f, up_b, bias_f.reshape(1, GLA_KEY_WIDTH),
      bias_b.reshape(1, GLA_KEY_WIDTH), gain.reshape(1, GLA_WIDTH))


def _outproj_kernel(a_ref, g_ref, w_ref, x_ref, gate_ref, fg_ref, o_ref, *, final):
    y = _dot(jnp.concatenate([a_ref[...], g_ref[...]], axis=1), w_ref[...])
    xn = x_ref[...] + gate_ref[...] * y
    if final:
        ms = jnp.mean(xn * xn, axis=-1, keepdims=True)
        xn = xn * lax.rsqrt(ms + EPS) * fg_ref[...]
    o_ref[...] = xn


def _outproj(a_out, g_out, w_out_bf16, x2, gate, final_gain, seq, final):
    M, D = x2.shape
    tm = 512
    bpt = seq // tm
    return pl.pallas_call(
        functools.partial(_outproj_kernel, final=final),
        grid=(M // tm,),
        in_specs=[pl.BlockSpec((tm, ATT_WIDTH), lambda i: (i, 0)),
                  pl.BlockSpec((tm, GLA_WIDTH), lambda i: (i, 0)),
                  pl.BlockSpec((ATT_WIDTH + GLA_WIDTH, D), lambda i: (0, 0)),
                  pl.BlockSpec((tm, D), lambda i: (i, 0)),
                  pl.BlockSpec((None, 1, D), lambda i: (i // bpt, 0, 0)),
                  pl.BlockSpec((1, D), lambda i: (0, 0))],
        out_specs=pl.BlockSpec((tm, D), lambda i: (i, 0)),
        out_shape=jax.ShapeDtypeStruct((M, D), F32),
        compiler_params=_params(("arbitrary",)),
        name="outproj",
    )(a_out, g_out, w_out_bf16, x2, gate, final_gain.reshape(1, D))


def kernel(x, c, w_cond, b_cond, w_in, gla_gate_up_fwd, gla_gate_bias_fwd, gla_gate_up_bwd,
           gla_gate_bias_bwd, gla_norm_gain, rel_bias, w_out, final_gain):
    B, S, D = x.shape
    depth = w_cond.shape[0]
    R = GLA_GATE_RANK
    xs = x.reshape(B * S, D)
    for layer in range(depth):
        mod = _mod(c, w_cond[layer], b_cond[layer])
        shift, scale, gate = [m.reshape(B, 1, D) for m in jnp.split(mod, 3, axis=-1)]

        proj, lr, *res_qkv = _inproj(xs, scale, shift, _wprep(w_in, layer), S)

        def rows2d(t, d):
            return [a.reshape(B * S // d, -1) for a in t]

        nat = rows2d(_attn_pattern(proj.reshape(B, S, PROJ_WIDTH), _bias_tiles(rel_bias, 1), B, S, 1), 1)
        res = [rows2d(_attn_pattern(qkv.reshape(B, S // d, -1), _bias_tiles(rel_bias, d), B, S, d), d)
               for d, qkv in zip(RESIDUE_DILATIONS, res_qkv)]
        a_out = _merge(nat, res, proj)

        up_f = jnp.pad(gla_gate_up_fwd[layer], ((0, LR_PAD - R), (0, 0)))
        up_b = jnp.pad(gla_gate_up_bwd[layer], ((R, LR_PAD - 2 * R), (0, 0)))
        g_out = _gla(proj, lr, up_f, up_b, gla_gate_bias_fwd[layer], gla_gate_bias_bwd[layer],
                     gla_norm_gain[layer], B, S)

        xs = _outproj(a_out, g_out.reshape(B * S, GLA_WIDTH), w_out[layer].astype(BF16),
                      xs, gate, final_gain, S, final=layer == depth - 1)
    return xs.reshape(B, S, D)
```

```python
import functools
import math

import jax
import jax.numpy as jnp
import numpy as np
from jax import lax
from jax.experimental import pallas as pl
from jax.experimental.pallas import tpu as pltpu

D_MODEL = 2048
ATT_WIDTH = 1024
ATT_HEADS = 16
ATT_HEAD_DIM = 64
DILATED_PATTERNS = ((128, 1), (512, 4), (2048, 16))
ATT_STEPS = 64
GLA_WIDTH = 1024
GLA_HEADS = 4
GLA_KEY_WIDTH = 512
GLA_DK = 128
GLA_DV = 256
GLA_GATE_RANK = 16
GLA_GATE_NORM = 16.0
GLA_CHUNK = 64
REL_BUCKETS = 32
REL_MAX_DIST = 1024
EPS = 1e-6
NEG_INF = -1e30

PROJ_WIDTH = 4 * ATT_WIDTH + 2 * GLA_KEY_WIDTH + 2 * GLA_WIDTH
COL_AQ, COL_AK, COL_AV, COL_AG = 0, 1024, 2048, 3072
COL_GQ, COL_GK, COL_GV, COL_GG = 4096, 4608, 5120, 6144
LR_PAD = 128
ATT_QKV_TILES = 3
RESIDUE_DILATIONS = tuple(d for _, d in DILATED_PATTERNS if d > 1)
ATT_TQ = 128
ATT_TK = ATT_TQ + 2 * ATT_STEPS
ATT_GROUP_HEADS = 4
ATT_HEAD_SETS = 1

VMEM_LIMIT = 56 * 1024 * 1024

BF16 = jnp.bfloat16
F32 = jnp.float32


def _params(sem):
    return pltpu.CompilerParams(dimension_semantics=sem, vmem_limit_bytes=VMEM_LIMIT)


def _dot(a, b):
    return jnp.dot(a, b, preferred_element_type=F32)


def _dot_nt(a, b):
    return lax.dot_general(a, b, (((1,), (1,)), ((), ())), preferred_element_type=F32)


def _dot_tn(a, b):
    return lax.dot_general(a, b, (((0,), (0,)), ((), ())), preferred_element_type=F32)


def _split_bf16(x):
    hi = x.astype(BF16)
    lo = (x - hi.astype(F32)).astype(BF16)
    return hi, lo


def _silu(x):
    return x / (1.0 + jnp.exp(-x))


def _mod_kernel(c_ref, w_ref, b_ref, o_ref):
    s_hi, s_lo = _split_bf16(_silu(c_ref[...]))
    w_hi, w_lo = _split_bf16(w_ref[...])
    o_ref[...] = _dot(s_hi, w_hi) + _dot(s_lo, w_hi) + _dot(s_hi, w_lo) + b_ref[...]


def _mod(c, w_cond, b_cond):
    B, D = c.shape
    N = w_cond.shape[1]
    tn = 768
    cp = jnp.pad(c, ((0, 8 - B), (0, 0)))
    out = pl.pallas_call(
        _mod_kernel,
        grid=(N // tn,),
        in_specs=[pl.BlockSpec((8, D), lambda j: (0, 0)),
                  pl.BlockSpec((D, tn), lambda j: (0, j)),
                  pl.BlockSpec((1, tn), lambda j: (0, j))],
        out_specs=pl.BlockSpec((8, tn), lambda j: (0, j)),
        out_shape=jax.ShapeDtypeStruct((8, N), F32),
        compiler_params=_params(("arbitrary",)),
        name="mod",
    )(cp, w_cond, b_cond.reshape(1, N))
    return out[:B]


def _wprep_kernel(w_ref, o_ref, *, tn, ncols):
    col = pl.program_id(0) * tn + lax.broadcasted_iota(jnp.int32, (1, tn), 1)
    scale = jnp.where(col < ATT_WIDTH, ATT_HEAD_DIM ** -0.5, 1.0)
    o_ref[...] = jnp.where(col < ncols, w_ref[...] * scale, 0.0).astype(BF16)


def _wprep(w_all, layer):
    _, D, ncols = w_all.shape
    width = PROJ_WIDTH + LR_PAD
    tn = 384
    assert width % tn == 0
    return pl.pallas_call(
        functools.partial(_wprep_kernel, tn=tn, ncols=ncols),
        grid=(width // tn,),
        in_specs=[pl.BlockSpec((None, D, tn), lambda j: (layer, 0, j))],
        out_specs=pl.BlockSpec((D, tn), lambda j: (0, j)),
        out_shape=jax.ShapeDtypeStruct((D, width), BF16),
        compiler_params=_params(("arbitrary",)),
        name="wprep",
    )(w_all)


def _inproj_kernel(x_ref, scale_ref, shift_ref, w_ref, p_ref, lr_ref, *rest, tm, tn):
    nres = len(RESIDUE_DILATIONS)
    res_refs, h_scr, acc_scr = rest[:nres], rest[nres], rest[nres + 1:]
    j = pl.program_id(1)

    @pl.when(j == 0)
    def _():
        x = x_ref[...]
        ms = jnp.mean(x * x, axis=-1, keepdims=True)
        h = x * lax.rsqrt(ms + EPS) * (1.0 + scale_ref[...]) + shift_ref[...]
        hb = h.astype(BF16)
        h_scr[...] = hb
        lr_ref[...] = _dot(hb, w_ref[:, PROJ_WIDTH:PROJ_WIDTH + LR_PAD])

    @pl.when(j >= ATT_QKV_TILES)
    def _():
        w = w_ref[:, pl.ds(pl.multiple_of(j * tn, tn), tn)]
        p_ref[...] = _dot(h_scr[...], w).astype(BF16)

    @pl.when(j < ATT_QKV_TILES)
    def _():
        h = h_scr[...]
        chunk = 256
        for c0 in range(0, tn, chunk):
            acc = _dot(h, w_ref[:, pl.ds(pl.multiple_of(j * tn + c0, chunk), chunk)])
            p_ref[:, c0:c0 + chunk] = acc.astype(BF16)
            for c in range(c0 // 128, (c0 + chunk) // 128):
                lanes = slice(c * 128 - c0, (c + 1) * 128 - c0)
                src, prev_d = acc_scr[0], 1
                src[c] = acc[:, lanes]
                for lvl, (ref, d) in enumerate(zip(res_refs, RESIDUE_DILATIONS)):
                    ratio, n = d // prev_d, tm // d
                    dst = acc_scr[lvl + 1] if lvl + 1 < len(RESIDUE_DILATIONS) else None
                    for rp in range(prev_d):
                        for a in range(ratio):
                            r = rp + prev_d * a
                            rows = src[c, pl.ds(rp * (tm // prev_d) + a, n, stride=ratio), :]
                            ref[:, r * tn + c * 128:r * tn + (c + 1) * 128] = rows.astype(BF16)
                            if dst is not None:
                                dst[c, r * n:(r + 1) * n, :] = rows
                    src, prev_d = dst, d


def _inproj(x2, scale, shift, w_main, seq):
    M, D = x2.shape
    tm, tn = 512, ATT_WIDTH
    bpt = seq // tm
    last_qkv = ATT_QKV_TILES - 1
    res_specs = [pl.BlockSpec((tm // d, d * tn), lambda i, j: (i, jnp.minimum(j, last_qkv)))
                 for d in RESIDUE_DILATIONS]
    res_shapes = [jax.ShapeDtypeStruct((M // d, ATT_QKV_TILES * d * tn), BF16)
                  for d in RESIDUE_DILATIONS]
    return pl.pallas_call(
        functools.partial(_inproj_kernel, tm=tm, tn=tn),
        grid=(M // tm, PROJ_WIDTH // tn),
        in_specs=[pl.BlockSpec((tm, D), lambda i, j: (i, 0)),
                  pl.BlockSpec((None, 1, D), lambda i, j: (i // bpt, 0, 0)),
                  pl.BlockSpec((None, 1, D), lambda i, j: (i // bpt, 0, 0)),
                  pl.BlockSpec(w_main.shape, lambda i, j: (0, 0), pipeline_mode=pl.Buffered(1))],
        out_specs=[pl.BlockSpec((tm, tn), lambda i, j: (i, j)),
                   pl.BlockSpec((tm, LR_PAD), lambda i, j: (i, 0))] + res_specs,
        out_shape=[jax.ShapeDtypeStruct((M, PROJ_WIDTH), BF16),
                   jax.ShapeDtypeStruct((M, LR_PAD), F32)] + res_shapes,
        scratch_shapes=[pltpu.VMEM((tm, D), BF16)]
                       + [pltpu.VMEM((tn // 128, tm, 128), F32) for _ in RESIDUE_DILATIONS],
        compiler_params=_params(("arbitrary", "arbitrary")),
        name="inproj",
    )(x2, scale, shift, w_main)


def _t5_bucket_np(rel):
    nb = REL_BUCKETS // 2
    max_exact = nb // 2
    n = np.abs(rel)
    large = max_exact + (np.log(np.maximum(n, 1) / max_exact)
                         / np.log(REL_MAX_DIST / max_exact) * (nb - max_exact)).astype(np.int32)
    large = np.minimum(large, nb - 1)
    return (np.where(rel > 0, nb, 0) + np.where(n < max_exact, n, large)).astype(np.int32)


def _bias_kernel(rbt_ref, bucket_ref, mask_ref, o_ref):
    rbt = rbt_ref[...]
    bucket = bucket_ref[...]
    ids = lax.broadcasted_iota(jnp.int32, (REL_BUCKETS, bucket.shape[1]), 0)
    onehot = jnp.where(ids == bucket, 1.0, 0.0).astype(BF16)
    hi = rbt.astype(BF16)
    rest = rbt - hi.astype(F32)
    mid = rest.astype(BF16)
    lo = (rest - mid.astype(F32)).astype(BF16)
    tbl = _dot(hi, onehot) + _dot(mid, onehot) + _dot(lo, onehot)
    for v in range(3):
        o_ref[v] = jnp.where(mask_ref[v] > 0.5, tbl, NEG_INF)


def _bias_tiles(rel_bias, dilation):
    w, tq, tk = ATT_STEPS, ATT_TQ, ATT_TK
    qi = np.arange(tq)[:, None]
    kj = np.arange(tk)[None, :]
    step = kj - w - qi
    band = np.abs(step) <= w
    bucket = _t5_bucket_np(step * dilation).reshape(1, tq * tk)
    masks = np.stack([band & (kj >= w), band, band & (kj < tk - w)]).astype(np.float32)
    masks = masks.reshape(3, 1, tq * tk)
    out = pl.pallas_call(
        _bias_kernel,
        out_shape=jax.ShapeDtypeStruct((3, ATT_HEADS, tq * tk), F32),
        compiler_params=pltpu.CompilerParams(vmem_limit_bytes=VMEM_LIMIT),
        name=f"bias_d{dilation}",
    )(rel_bias.T, jnp.asarray(bucket), jnp.asarray(masks))
    return out.reshape(3, ATT_HEADS, tq, tk)


def _attn_kernel(q_ref, kp_ref, km_ref, kn_ref, vp_ref, vm_ref, vn_ref, bias_ref,
                 o_ref, m_ref, den_ref, kbuf, vbuf, s_scr, p_scr, inv_scr, *, tb):
    w, tq, tk = ATT_STEPS, ATT_TQ, ATT_TK
    nsub = tb // tq
    npair = ATT_HEADS // 2
    gh = ATT_GROUP_HEADS
    gw = gh * ATT_HEAD_DIM
    ngroup = ATT_HEADS // gh
    nk = tb + 2 * w
    i = pl.program_id(2)
    first = i == 0
    last = i == pl.num_programs(2) - 1

    row = 0
    for kpart, vpart in ((kp_ref, vp_ref), (km_ref, vm_ref), (kn_ref, vn_ref)):
        rows = slice(row, row + kpart.shape[0])
        kbuf[rows] = kpart[...]
        vbuf[rows] = vpart[...]
        row += kpart.shape[0]
    slot = lax.broadcasted_iota(jnp.int32, (1, gw), 1) // ATT_HEAD_DIM

    half = tq // 2
    lower_half = lax.broadcasted_iota(jnp.int32, (half, 128), 1) < ATT_HEAD_DIM
    lower_q = lax.broadcasted_iota(jnp.int32, (tq, 128), 1) < ATT_HEAD_DIM

    nsets = ATT_HEAD_SETS

    def head_pairs(hs):
        return range(hs * npair // nsets, (hs + 1) * npair // nsets)

    def tile(j):
        return pl.multiple_of(j * tq, tq)

    def logits(j, buf, hs):
        qs = tile(j)
        for hp in head_pairs(hs):
            cp = slice(hp * 128, (hp + 1) * 128)
            q = q_ref[pl.ds(qs, tq), cp]
            zero = jnp.zeros_like(q)
            q2 = jnp.concatenate([jnp.where(lower_q, q, zero), jnp.where(lower_q, zero, q)], axis=0)
            s2 = _dot_nt(q2, kbuf[pl.ds(qs, tk), cp])
            s_scr[buf, 2 * hp] = s2[:tq]
            s_scr[buf, 2 * hp + 1] = s2[tq:]

    def softmax(j, buf, hs):
        qs = tile(j)
        var = jnp.where(jnp.logical_and(first, j == 0), 0,
                        jnp.where(jnp.logical_and(last, j == nsub - 1), 2, 1))
        if hs == 0:
            m_ref[pl.ds(qs, tq), :] = jnp.zeros((tq, 128), F32)
            den_ref[pl.ds(qs, tq), :] = jnp.ones((tq, 128), F32)
        for hp in head_pairs(hs):
            for r0 in (0, half):
                dens = []
                for h in (2 * hp, 2 * hp + 1):
                    s = s_scr[buf, h, r0:r0 + half, :] + bias_ref[var, h, r0:r0 + half, :]
                    m = jnp.max(s, axis=-1, keepdims=True)
                    p = jnp.exp(s - m)
                    den = jnp.sum(p, axis=-1, keepdims=True)
                    p_scr[buf, h // gh, r0:r0 + half, (h % gh) * tk:(h % gh + 1) * tk] = p.astype(BF16)
                    m_ref[pl.ds(qs + r0, half), h:h + 1] = m
                    den_ref[pl.ds(qs + r0, half), h:h + 1] = den
                    dens.append(den)
                pair = hp % (gh // 2)
                inv_scr[buf, (2 * hp) // gh, r0:r0 + half, pair * 128:(pair + 1) * 128] = (
                    1.0 / jnp.where(lower_half, dens[0], dens[1]))

    def outputs(j, buf, hs):
        qs = tile(j)
        for g in range(hs * ngroup // nsets, (hs + 1) * ngroup // nsets):
            cg = slice(g * gw, (g + 1) * gw)
            vw = vbuf[pl.ds(qs, tk), cg]
            v_stack = jnp.concatenate([jnp.where(slot == s, vw, jnp.zeros_like(vw))
                                       for s in range(gh)], axis=0)
            o = _dot(p_scr[buf, g], v_stack)
            o_ref[pl.ds(qs, tq), cg] = (o * inv_scr[buf, g]).astype(BF16)

    for hs in range(nsets):
        def one_tile(j, carry, hs=hs):
            logits(j, 0, hs)
            softmax(j, 0, hs)
            outputs(j, 0, hs)
            return carry

        lax.fori_loop(0, nsub, one_tile, 0)


def _attn_pattern(qkv, bias, batch, seq, dilation):
    w = ATT_STEPS
    L = seq // dilation
    tb = min(1024, L)
    nblk = L // tb
    hb = tb // w
    nhalo = L // w

    def main(j):
        return pl.BlockSpec((None, tb, ATT_WIDTH), lambda b, r, i: (b, i, j * dilation + r))

    def prev(j):
        return pl.BlockSpec((None, w, ATT_WIDTH),
                            lambda b, r, i: (b, jnp.maximum(i * hb - 1, 0), j * dilation + r))

    def nxt(j):
        return pl.BlockSpec((None, w, ATT_WIDTH),
                            lambda b, r, i: (b, jnp.minimum((i + 1) * hb, nhalo - 1), j * dilation + r))

    nk, gh = tb + 2 * w, ATT_GROUP_HEADS
    return pl.pallas_call(
        functools.partial(_attn_kernel, tb=tb),
        grid=(batch, dilation, nblk),
        in_specs=[main(0), prev(1), main(1), nxt(1), prev(2), main(2), nxt(2),
                  pl.BlockSpec((3, ATT_HEADS, ATT_TQ, ATT_TK), lambda b, r, i: (0, 0, 0, 0),
                               pipeline_mode=pl.Buffered(1))],
        out_specs=[pl.BlockSpec((None, tb, ATT_WIDTH), lambda b, r, i: (b, i, r)),
                   pl.BlockSpec((None, tb, 128), lambda b, r, i: (b, i, r)),
                   pl.BlockSpec((None, tb, 128), lambda b, r, i: (b, i, r))],
        out_shape=[jax.ShapeDtypeStruct((batch, L, dilation * ATT_WIDTH), BF16),
                   jax.ShapeDtypeStruct((batch, L, dilation * 128), F32),
                   jax.ShapeDtypeStruct((batch, L, dilation * 128), F32)],
        scratch_shapes=[pltpu.VMEM((nk, ATT_WIDTH), BF16),
                        pltpu.VMEM((nk, ATT_WIDTH), BF16),
                        pltpu.VMEM((1, ATT_HEADS, ATT_TQ, ATT_TK), F32),
                        pltpu.VMEM((1, ATT_HEADS // gh, ATT_TQ, gh * ATT_TK), BF16),
                        pltpu.VMEM((1, ATT_HEADS // gh, ATT_TQ, gh * ATT_HEAD_DIM), F32)],
        compiler_params=_params(("arbitrary", "arbitrary", "arbitrary")),
        name=f"attn_d{dilation}",
    )(qkv, qkv, qkv, qkv, qkv, qkv, qkv, bias)


def _merge_kernel(*refs, tm):
    nres = len(RESIDUE_DILATIONS)
    o1_ref, m1_ref, d1_ref = refs[:3]
    res = [refs[3 + 3 * n:6 + 3 * n] for n in range(nres)]
    ag_ref, e_ref, out_ref = refs[3 + 3 * nres:6 + 3 * nres]
    scr = [refs[6 + 3 * nres + 3 * n:9 + 3 * nres + 3 * n] for n in range(nres)]
    tmp = refs[6 + 6 * nres]

    ncol = ATT_WIDTH // 128
    for d, (o_ref, m_ref, d_ref), (so, sm, sd) in zip(RESIDUE_DILATIONS, res, scr):
        prev = d // 4 if d > 4 else 1
        for r in range(d):
            rp, a = r % prev, r // prev
            for c in range(ncol):
                col = r * ATT_WIDTH + c * 128
                rows = o_ref[:, col:col + 128].astype(F32)
                if prev == 1:
                    so[c, pl.ds(r, tm // d, stride=d), :] = rows
                else:
                    tmp[c, pl.ds(rp * (tm // prev) + a, tm // d, stride=d // prev), :] = rows
            sm[pl.ds(r, tm // d, stride=d), :] = m_ref[:, r * 128:(r + 1) * 128]
            sd[pl.ds(r, tm // d, stride=d), :] = d_ref[:, r * 128:(r + 1) * 128]
        if prev > 1:
            n = tm // prev
            for rp in range(prev):
                for c in range(ncol):
                    so[c, pl.ds(rp, n, stride=prev), :] = tmp[c, rp * n:(rp + 1) * n, :]

    rc, gw = 128, 256
    head_lane = lax.broadcasted_iota(jnp.int32, (rc, 128), 1) < ATT_HEADS
    for r0 in range(0, tm, rc):
        rows = slice(r0, r0 + rc)
        lses = ([m1_ref[rows, :] + jnp.log(d1_ref[rows, :])]
                + [sm[rows, :] + jnp.log(sd[rows, :]) for _, sm, sd in scr])
        mx = functools.reduce(jnp.maximum, lses)
        es = [jnp.exp(l - mx) for l in lses]
        inv = 1.0 / functools.reduce(jnp.add, es)
        packed = []
        for ei in es:
            wgt = jnp.where(head_lane, ei * inv, 0.0)
            hi = wgt.astype(BF16).astype(F32)
            lo = (wgt - hi).astype(BF16).astype(F32)
            packed.append((hi + pltpu.roll(lo, ATT_HEADS, 1)).astype(BF16))
        for c0 in range(0, ATT_WIDTH, gw):
            cols = slice(c0, c0 + gw)
            e = e_ref[:, cols]
            outs = [o1_ref[rows, cols].astype(F32)] + [
                jnp.concatenate([so[c, rows, :] for c in range(c0 // 128, (c0 + gw) // 128)], axis=1)
                for so, _, _ in scr]
            att = functools.reduce(jnp.add, [_dot(w, e) * o for w, o in zip(packed, outs)])
            out_ref[rows, cols] = (att * _silu(ag_ref[rows, cols].astype(F32))).astype(BF16)


def _merge(nat, res, proj):
    M = proj.shape[0]
    tm = 512
    expand = np.zeros((128, ATT_WIDTH), np.float32)
    for h in range(ATT_HEADS):
        expand[h, h * ATT_HEAD_DIM:(h + 1) * ATT_HEAD_DIM] = 1.0
        expand[ATT_HEADS + h, h * ATT_HEAD_DIM:(h + 1) * ATT_HEAD_DIM] = 1.0
    row = lambda rows, width: pl.BlockSpec((rows, width), lambda i: (i, 0))
    triple = lambda d: [row(tm // d, d * ATT_WIDTH), row(tm // d, d * 128), row(tm // d, d * 128)]
    in_specs = (triple(1) + [s for d in RESIDUE_DILATIONS for s in triple(d)]
                + [pl.BlockSpec((tm, ATT_WIDTH), lambda i: (i, COL_AG // ATT_WIDTH)),
                   pl.BlockSpec((128, ATT_WIDTH), lambda i: (0, 0))])
    scratch = []
    for _ in RESIDUE_DILATIONS:
        scratch += [pltpu.VMEM((ATT_WIDTH // 128, tm, 128), F32),
                    pltpu.VMEM((tm, 128), F32), pltpu.VMEM((tm, 128), F32)]
    scratch.append(pltpu.VMEM((ATT_WIDTH // 128, tm, 128), F32))
    return pl.pallas_call(
        functools.partial(_merge_kernel, tm=tm),
        grid=(M // tm,),
        in_specs=in_specs,
        out_specs=row(tm, ATT_WIDTH),
        out_shape=jax.ShapeDtypeStruct((M, ATT_WIDTH), BF16),
        scratch_shapes=scratch,
        compiler_params=_params(("arbitrary",)),
        name="merge",
    )(*nat, *[a for t in res for a in t], proj, jnp.asarray(expand, BF16))


def _gla_kernel(*refs, ts, nstep):
    fwd_in, bwd_in = refs[:5], refs[5:10]
    (upf_ref, upb_ref, gbf_ref, gbb_ref, gain_ref, out_ref,
     state, o_acc, qf_scr, kd_scr, ks_scr, oin_scr, st_scr) = refs[10:]
    C = GLA_CHUNK
    nchunk = ts // C
    chunks = [slice(c * C, (c + 1) * C) for c in range(nchunk)]
    i = pl.program_id(2)
    dirs = [(0, False, fwd_in, upf_ref, gbf_ref), (1, True, bwd_in, upb_ref, gbb_ref)]

    @pl.when(i == 0)
    def _():
        state[...] = jnp.zeros_like(state)


    log_gs = []
    for d, reverse, (q_ref, k_ref, v_ref, lr_ref, gg_ref), up_ref, gb_ref in dirs:
        lr_hi, lr_lo = _split_bf16(lr_ref[...])
        up_hi, up_lo = _split_bf16(up_ref[...])
        z = _dot(lr_hi, up_hi) + _dot(lr_hi, up_lo) + _dot(lr_lo, up_hi) + gb_ref[...]
        log_gs.append((jnp.minimum(z, 0.0) - jnp.log(1.0 + jnp.exp(-jnp.abs(z))))
                      * (1.0 / GLA_GATE_NORM))

    ri = lax.broadcasted_iota(jnp.int32, (C, C), 0)
    ci = lax.broadcasted_iota(jnp.int32, (C, C), 1)
    dec_cols = [[], []]
    for d, reverse, (q_ref, k_ref, v_ref, lr_ref, gg_ref), up_ref, gb_ref in dirs:
        tri = jnp.where((ci >= ri) if reverse else (ci <= ri), 1.0, 0.0).astype(BF16)
        for rows in chunks:
            g_hi, g_lo = _split_bf16(log_gs[d][rows])
            cum = _dot(tri, jnp.concatenate([g_hi, g_lo], axis=1))
            b = cum[:, :GLA_DK] + cum[:, GLA_DK:]
            b_edge = b[0:1] if reverse else b[C - 1:C]
            q = q_ref[rows, :].astype(F32)
            k = k_ref[rows, :].astype(F32)
            edge = jnp.exp(b_edge)
            kd = k * jnp.exp(-b)
            qf_scr[d, rows, :] = (q * jnp.exp(b + math.log(GLA_DK ** -0.5))).astype(BF16)
            kd_scr[d, rows, :] = kd.astype(BF16)
            ks_scr[d, rows, :] = (kd * edge).astype(BF16)
            dec = jnp.broadcast_to(edge, (GLA_DK, GLA_DK)).T
            dec_cols[d].append(jnp.concatenate([dec, dec], axis=1))

    atts = [[jnp.where((ci >= ri) if reverse else (ci <= ri),
                       _dot_nt(qf_scr[d, rows, :], kd_scr[d, rows, :]), 0.0).astype(BF16)
             for rows in chunks] for d, reverse, *_ in dirs]

    kvs = [[], []]
    for d, reverse, (q_ref, k_ref, v_ref, lr_ref, gg_ref), up_ref, gb_ref in dirs:
        for rows, att in zip(chunks, atts[d]):
            v = v_ref[rows, :]
            oin_scr[d, rows, :] = _dot(att, v)
            kvs[d].append(_dot_tn(ks_scr[d, rows, :], v))

    orders = [list(range(nchunk)), list(range(nchunk - 1, -1, -1))]
    for d, reverse, *_ in dirs:
        st = state[d]
        for c in orders[d]:
            st_scr[d, c] = st.astype(BF16)
            st = st * dec_cols[d][c] + kvs[d][c]
        state[d] = st

    for d, reverse, *_ in dirs:
        for c in orders[d]:
            rows = chunks[c]
            oin_scr[d, rows, :] = oin_scr[d, rows, :] + _dot(qf_scr[d, rows, :], st_scr[d, c])

    blocks = [i, nstep - 1 - i]

    @pl.when(i < nstep // 2)
    def _():
        for d, reverse, *_ in dirs:
            base = pl.multiple_of(blocks[d] * ts, ts)
            for rows in chunks:
                o_acc[pl.ds(base + rows.start, C), :] = oin_scr[d, rows, :]

    @pl.when(i >= nstep // 2)
    def _():
        for d, reverse, (q_ref, k_ref, v_ref, lr_ref, gg_ref), up_ref, gb_ref in dirs:
            base = pl.multiple_of(blocks[d] * ts, ts)
            for rows in chunks:
                dst = pl.ds(base + rows.start, C)
                tot = oin_scr[d, rows, :] + o_acc[dst, :]
                ms = jnp.mean(tot * tot, axis=-1, keepdims=True)
                g_o = tot * lax.rsqrt(ms + EPS) * gain_ref[...]
                out_ref[dst, :] = (g_o * _silu(gg_ref[rows, :].astype(F32))).astype(BF16)


def _gla(proj, lr, up_f, up_b, bias_f, bias_b, gain, batch, seq):
    ts = 2048
    nstep = seq // ts
    assert nstep % 2 == 0
    C = GLA_CHUNK
    p3 = proj.reshape(batch, seq, PROJ_WIDTH)
    lr3 = lr.reshape(batch, seq, LR_PAD)

    def direction_specs(step):
        def seq_block(width, col0):
            return pl.BlockSpec((None, ts, width), lambda b, h, i: (b, step(i), col0 // width + h))
        return [seq_block(GLA_DK, COL_GQ), seq_block(GLA_DK, COL_GK), seq_block(GLA_DV, COL_GV),
                pl.BlockSpec((None, ts, LR_PAD), lambda b, h, i: (b, step(i), 0)),
                seq_block(GLA_DV, COL_GG)]

    per_head = lambda rows, width: pl.BlockSpec((rows, width), lambda b, h, i: (0, h))
    in_specs = (direction_specs(lambda i: i) + direction_specs(lambda i: nstep - 1 - i)
                + [per_head(LR_PAD, GLA_DK), per_head(LR_PAD, GLA_DK),
                   per_head(1, GLA_DK), per_head(1, GLA_DK), per_head(1, GLA_DV)])
    dir_args = [p3, p3, p3, lr3, p3]
    return pl.pallas_call(
        functools.partial(_gla_kernel, ts=ts, nstep=nstep),
        grid=(batch, GLA_HEADS, nstep),
        in_specs=in_specs,
        out_specs=pl.BlockSpec((None, seq, GLA_DV), lambda b, h, i: (b, 0, h)),
        out_shape=jax.ShapeDtypeStruct((batch, seq, GLA_WIDTH), BF16),
        scratch_shapes=[pltpu.VMEM((2, GLA_DK, GLA_DV), F32),
                        pltpu.VMEM((seq, GLA_DV), F32),
                        pltpu.VMEM((2, ts, GLA_DK), BF16),
                        pltpu.VMEM((2, ts, GLA_DK), BF16),
                        pltpu.VMEM((2, ts, GLA_DK), BF16),
                        pltpu.VMEM((2, ts, GLA_DV), F32),
                        pltpu.VMEM((2, ts // C, GLA_DK, GLA_DV), BF16)],
        compiler_params=_params(("arbitrary", "arbitrary", "arbitrary")),
        name="gla",
    )(*dir_args, *dir_args, up_f, up_b, bias_f.reshape(1, GLA_KEY_WIDTH),
      bias_b.reshape(1, GLA_KEY_WIDTH), gain.reshape(1, GLA_WIDTH))


def _outproj_kernel(a_ref, g_ref, w_ref, x_ref, gate_ref, fg_ref, o_ref, *, final):
    y = _dot(jnp.concatenate([a_ref[...], g_ref[...]], axis=1), w_ref[...])
    xn = x_ref[...] + gate_ref[...] * y
    if final:
        ms = jnp.mean(xn * xn, axis=-1, keepdims=True)
        xn = xn * lax.rsqrt(ms + EPS) * fg_ref[...]
    o_ref[...] = xn


def _outproj(a_out, g_out, w_out_bf16, x2, gate, final_gain, seq, final):
    M, D = x2.shape
    tm = 512
    bpt = seq // tm
    return pl.pallas_call(
        functools.partial(_outproj_kernel, final=final),
        grid=(M // tm,),
        in_specs=[pl.BlockSpec((tm, ATT_WIDTH), lambda i: (i, 0)),
                  pl.BlockSpec((tm, GLA_WIDTH), lambda i: (i, 0)),
                  pl.BlockSpec((ATT_WIDTH + GLA_WIDTH, D), lambda i: (0, 0)),
                  pl.BlockSpec((tm, D), lambda i: (i, 0)),
                  pl.BlockSpec((None, 1, D), lambda i: (i // bpt, 0, 0)),
                  pl.BlockSpec((1, D), lambda i: (0, 0))],
        out_specs=pl.BlockSpec((tm, D), lambda i: (i, 0)),
        out_shape=jax.ShapeDtypeStruct((M, D), F32),
        compiler_params=_params(("arbitrary",)),
        name="outproj",
    )(a_out, g_out, w_out_bf16, x2, gate, final_gain.reshape(1, D))


def kernel(x, c, w_cond, b_cond, w_in, gla_gate_up_fwd, gla_gate_bias_fwd, gla_gate_up_bwd,
           gla_gate_bias_bwd, gla_norm_gain, rel_bias, w_out, final_gain):
    B, S, D = x.shape
    depth = w_cond.shape[0]
    R = GLA_GATE_RANK
    xs = x.reshape(B * S, D)
    for layer in range(depth):
        mod = _mod(c, w_cond[layer], b_cond[layer])
        shift, scale, gate = [m.reshape(B, 1, D) for m in jnp.split(mod, 3, axis=-1)]

        proj, lr, *res_qkv = _inproj(xs, scale, shift, _wprep(w_in, layer), S)

        def rows2d(t, d):
            return [a.reshape(B * S // d, -1) for a in t]

        nat = rows2d(_attn_pattern(proj.reshape(B, S, PROJ_WIDTH), _bias_tiles(rel_bias, 1), B, S, 1), 1)
        res = [rows2d(_attn_pattern(qkv.reshape(B, S // d, -1), _bias_tiles(rel_bias, d), B, S, d), d)
               for d, qkv in zip(RESIDUE_DILATIONS, res_qkv)]
        a_out = _merge(nat, res, proj)

        up_f = jnp.pad(gla_gate_up_fwd[layer], ((0, LR_PAD - R), (0, 0)))
        up_b = jnp.pad(gla_gate_up_bwd[layer], ((R, LR_PAD - 2 * R), (0, 0)))
        g_out = _gla(proj, lr, up_f, up_b, gla_gate_bias_fwd[layer], gla_gate_bias_bwd[layer],
                     gla_norm_gain[layer], B, S)

        xs = _outproj(a_out, g_out.reshape(B * S, GLA_WIDTH), w_out[layer].astype(BF16),
                      xs, gate, final_gain, S, final=layer == depth - 1)
    return xs.reshape(B, S, D)
```

```python
import functools
import math

import jax
import jax.numpy as jnp
import numpy as np
from jax import lax
from jax.experimental import pallas as pl
from jax.experimental.pallas import tpu as pltpu

D_MODEL = 2048
ATT_WIDTH = 1024
ATT_HEADS = 16
ATT_HEAD_DIM = 64
DILATED_PATTERNS = ((128, 1), (512, 4), (2048, 16))
ATT_STEPS = 64
GLA_WIDTH = 1024
GLA_HEADS = 4
GLA_KEY_WIDTH = 512
GLA_DK = 128
GLA_DV = 256
GLA_GATE_RANK = 16
GLA_GATE_NORM = 16.0
GLA_CHUNK = 64
REL_BUCKETS = 32
REL_MAX_DIST = 1024
EPS = 1e-6
NEG_INF = -1e30

PROJ_WIDTH = 4 * ATT_WIDTH + 2 * GLA_KEY_WIDTH + 2 * GLA_WIDTH
COL_AQ, COL_AK, COL_AV, COL_AG = 0, 1024, 2048, 3072
COL_GQ, COL_GK, COL_GV, COL_GG = 4096, 4608, 5120, 6144
LR_PAD = 128
ATT_QKV_TILES = 3
RESIDUE_DILATIONS = tuple(d for _, d in DILATED_PATTERNS if d > 1)
ATT_TQ = 128
ATT_TK = ATT_TQ + 2 * ATT_STEPS
ATT_GROUP_HEADS = 4
ATT_HEAD_SETS = 1

VMEM_LIMIT = 56 * 1024 * 1024

BF16 = jnp.bfloat16
F32 = jnp.float32


def _params(sem):
    return pltpu.CompilerParams(dimension_semantics=sem, vmem_limit_bytes=VMEM_LIMIT)


def _dot(a, b):
    return jnp.dot(a, b, preferred_element_type=F32)


def _dot_nt(a, b):
    return lax.dot_general(a, b, (((1,), (1,)), ((), ())), preferred_element_type=F32)


def _dot_tn(a, b):
    return lax.dot_general(a, b, (((0,), (0,)), ((), ())), preferred_element_type=F32)


def _split_bf16(x):
    hi = x.astype(BF16)
    lo = (x - hi.astype(F32)).astype(BF16)
    return hi, lo


def _silu(x):
    return x / (1.0 + jnp.exp(-x))


def _mod_kernel(c_ref, w_ref, b_ref, o_ref):
    s_hi, s_lo = _split_bf16(_silu(c_ref[...]))
    w_hi, w_lo = _split_bf16(w_ref[...])
    o_ref[...] = _dot(s_hi, w_hi) + _dot(s_lo, w_hi) + _dot(s_hi, w_lo) + b_ref[...]


def _mod(c, w_cond, b_cond):
    B, D = c.shape
    N = w_cond.shape[1]
    tn = 768
    cp = jnp.pad(c, ((0, 8 - B), (0, 0)))
    out = pl.pallas_call(
        _mod_kernel,
        grid=(N // tn,),
        in_specs=[pl.BlockSpec((8, D), lambda j: (0, 0)),
                  pl.BlockSpec((D, tn), lambda j: (0, j)),
                  pl.BlockSpec((1, tn), lambda j: (0, j))],
        out_specs=pl.BlockSpec((8, tn), lambda j: (0, j)),
        out_shape=jax.ShapeDtypeStruct((8, N), F32),
        compiler_params=_params(("arbitrary",)),
        name="mod",
    )(cp, w_cond, b_cond.reshape(1, N))
    return out[:B]


def _wprep_kernel(w_ref, lr_ref, o_ref, *, tn, ntile):
    j = pl.program_id(0)

    @pl.when(j < ntile)
    def _():
        col = j * tn + lax.broadcasted_iota(jnp.int32, (1, tn), 1)
        scale = jnp.where(col < ATT_WIDTH, ATT_HEAD_DIM ** -0.5, 1.0)
        o_ref[...] = (w_ref[...] * scale).astype(BF16)

    @pl.when(j == ntile)
    def _():
        o_ref[:, :LR_PAD] = lr_ref[...].astype(BF16)


def _wprep(w):
    D, ncols = w.shape
    tn = ATT_WIDTH
    ntile = PROJ_WIDTH // tn
    main = w[:, :PROJ_WIDTH]
    lr = jnp.pad(w[:, PROJ_WIDTH:], ((0, 0), (0, PROJ_WIDTH + LR_PAD - ncols)))
    return pl.pallas_call(
        functools.partial(_wprep_kernel, tn=tn, ntile=ntile),
        grid=(ntile + 1,),
        in_specs=[pl.BlockSpec((D, tn), lambda j: (0, jnp.minimum(j, ntile - 1))),
                  pl.BlockSpec((D, LR_PAD), lambda j: (0, 0))],
        out_specs=pl.BlockSpec((D, tn), lambda j: (0, j)),
        out_shape=jax.ShapeDtypeStruct((D, PROJ_WIDTH + LR_PAD), BF16),
        compiler_params=_params(("arbitrary",)),
        name="wprep",
    )(main, lr)


def _inproj_kernel(x_ref, scale_ref, shift_ref, w_ref, p_ref, lr_ref, *rest, tm, tn):
    nres = len(RESIDUE_DILATIONS)
    res_refs, h_scr, acc_scr = rest[:nres], rest[nres], rest[nres + 1:]
    j = pl.program_id(1)

    @pl.when(j == 0)
    def _():
        x = x_ref[...]
        ms = jnp.mean(x * x, axis=-1, keepdims=True)
        h = x * lax.rsqrt(ms + EPS) * (1.0 + scale_ref[...]) + shift_ref[...]
        hb = h.astype(BF16)
        h_scr[...] = hb
        lr_ref[...] = _dot(hb, w_ref[:, PROJ_WIDTH:PROJ_WIDTH + LR_PAD])

    @pl.when(j >= ATT_QKV_TILES)
    def _():
        w = w_ref[:, pl.ds(pl.multiple_of(j * tn, tn), tn)]
        p_ref[...] = _dot(h_scr[...], w).astype(BF16)

    @pl.when(j < ATT_QKV_TILES)
    def _():
        h = h_scr[...]
        chunk = 256
        for c0 in range(0, tn, chunk):
            acc = _dot(h, w_ref[:, pl.ds(pl.multiple_of(j * tn + c0, chunk), chunk)])
            p_ref[:, c0:c0 + chunk] = acc.astype(BF16)
            for c in range(c0 // 128, (c0 + chunk) // 128):
                lanes = slice(c * 128 - c0, (c + 1) * 128 - c0)
                src, prev_d = acc_scr[0], 1
                src[c] = acc[:, lanes]
                for lvl, (ref, d) in enumerate(zip(res_refs, RESIDUE_DILATIONS)):
                    ratio, n = d // prev_d, tm // d
                    dst = acc_scr[lvl + 1] if lvl + 1 < len(RESIDUE_DILATIONS) else None
                    for rp in range(prev_d):
                        for a in range(ratio):
                            r = rp + prev_d * a
                            rows = src[c, pl.ds(rp * (tm // prev_d) + a, n, stride=ratio), :]
                            ref[:, r * tn + c * 128:r * tn + (c + 1) * 128] = rows.astype(BF16)
                            if dst is not None:
                                dst[c, r * n:(r + 1) * n, :] = rows
                    src, prev_d = dst, d


def _inproj(x2, scale, shift, w_main, seq):
    M, D = x2.shape
    tm, tn = 512, ATT_WIDTH
    bpt = seq // tm
    last_qkv = ATT_QKV_TILES - 1
    res_specs = [pl.BlockSpec((tm // d, d * tn), lambda i, j: (i, jnp.minimum(j, last_qkv)))
                 for d in RESIDUE_DILATIONS]
    res_shapes = [jax.ShapeDtypeStruct((M // d, ATT_QKV_TILES * d * tn), BF16)
                  for d in RESIDUE_DILATIONS]
    return pl.pallas_call(
        functools.partial(_inproj_kernel, tm=tm, tn=tn),
        grid=(M // tm, PROJ_WIDTH // tn),
        in_specs=[pl.BlockSpec((tm, D), lambda i, j: (i, 0)),
                  pl.BlockSpec((None, 1, D), lambda i, j: (i // bpt, 0, 0)),
                  pl.BlockSpec((None, 1, D), lambda i, j: (i // bpt, 0, 0)),
                  pl.BlockSpec(w_main.shape, lambda i, j: (0, 0), pipeline_mode=pl.Buffered(1))],
        out_specs=[pl.BlockSpec((tm, tn), lambda i, j: (i, j)),
                   pl.BlockSpec((tm, LR_PAD), lambda i, j: (i, 0))] + res_specs,
        out_shape=[jax.ShapeDtypeStruct((M, PROJ_WIDTH), BF16),
                   jax.ShapeDtypeStruct((M, LR_PAD), F32)] + res_shapes,
        scratch_shapes=[pltpu.VMEM((tm, D), BF16)]
                       + [pltpu.VMEM((tn // 128, tm, 128), F32) for _ in RESIDUE_DILATIONS],
        compiler_params=_params(("arbitrary", "arbitrary")),
        name="inproj",
    )(x2, scale, shift, w_main)


def _t5_bucket_np(rel):
    nb = REL_BUCKETS // 2
    max_exact = nb // 2
    n = np.abs(rel)
    large = max_exact + (np.log(np.maximum(n, 1) / max_exact)
                         / np.log(REL_MAX_DIST / max_exact) * (nb - max_exact)).astype(np.int32)
    large = np.minimum(large, nb - 1)
    return (np.where(rel > 0, nb, 0) + np.where(n < max_exact, n, large)).astype(np.int32)


def _bias_kernel(rbt_ref, bucket_ref, mask_ref, o_ref):
    rbt = rbt_ref[...]
    bucket = bucket_ref[...]
    ids = lax.broadcasted_iota(jnp.int32, (REL_BUCKETS, bucket.shape[1]), 0)
    onehot = jnp.where(ids == bucket, 1.0, 0.0).astype(BF16)
    hi = rbt.astype(BF16)
    rest = rbt - hi.astype(F32)
    mid = rest.astype(BF16)
    lo = (rest - mid.astype(F32)).astype(BF16)
    tbl = _dot(hi, onehot) + _dot(mid, onehot) + _dot(lo, onehot)
    for v in range(3):
        o_ref[v] = jnp.where(mask_ref[v] > 0.5, tbl, NEG_INF)


def _bias_tiles(rel_bias, dilation):
    w, tq, tk = ATT_STEPS, ATT_TQ, ATT_TK
    qi = np.arange(tq)[:, None]
    kj = np.arange(tk)[None, :]
    step = kj - w - qi
    band = np.abs(step) <= w
    bucket = _t5_bucket_np(step * dilation).reshape(1, tq * tk)
    masks = np.stack([band & (kj >= w), band, band & (kj < tk - w)]).astype(np.float32)
    masks = masks.reshape(3, 1, tq * tk)
    out = pl.pallas_call(
        _bias_kernel,
        out_shape=jax.ShapeDtypeStruct((3, ATT_HEADS, tq * tk), F32),
        compiler_params=pltpu.CompilerParams(vmem_limit_bytes=VMEM_LIMIT),
        name=f"bias_d{dilation}",
    )(rel_bias.T, jnp.asarray(bucket), jnp.asarray(masks))
    return out.reshape(3, ATT_HEADS, tq, tk)


def _attn_kernel(q_ref, kp_ref, km_ref, kn_ref, vp_ref, vm_ref, vn_ref, bias_ref,
                 o_ref, m_ref, den_ref, kbuf, vbuf, s_scr, p_scr, inv_scr, *, tb):
    w, tq, tk = ATT_STEPS, ATT_TQ, ATT_TK
    nsub = tb // tq
    npair = ATT_HEADS // 2
    gh = ATT_GROUP_HEADS
    gw = gh * ATT_HEAD_DIM
    ngroup = ATT_HEADS // gh
    nk = tb + 2 * w
    i = pl.program_id(2)
    first = i == 0
    last = i == pl.num_programs(2) - 1

    row = 0
    for kpart, vpart in ((kp_ref, vp_ref), (km_ref, vm_ref), (kn_ref, vn_ref)):
        rows = slice(row, row + kpart.shape[0])
        kbuf[rows] = kpart[...]
        vbuf[rows] = vpart[...]
        row += kpart.shape[0]
    slot = lax.broadcasted_iota(jnp.int32, (1, gw), 1) // ATT_HEAD_DIM

    half = tq // 2
    lower_half = lax.broadcasted_iota(jnp.int32, (half, 128), 1) < ATT_HEAD_DIM
    lower_q = lax.broadcasted_iota(jnp.int32, (tq, 128), 1) < ATT_HEAD_DIM

    nsets = ATT_HEAD_SETS

    def head_pairs(hs):
        return range(hs * npair // nsets, (hs + 1) * npair // nsets)

    def tile(j):
        return pl.multiple_of(j * tq, tq)

    def logits(j, buf, hs):
        qs = tile(j)
        for hp in head_pairs(hs):
            cp = slice(hp * 128, (hp + 1) * 128)
            q = q_ref[pl.ds(qs, tq), cp]
            zero = jnp.zeros_like(q)
            q2 = jnp.concatenate([jnp.where(lower_q, q, zero), jnp.where(lower_q, zero, q)], axis=0)
            s2 = _dot_nt(q2, kbuf[pl.ds(qs, tk), cp])
            s_scr[buf, 2 * hp] = s2[:tq]
            s_scr[buf, 2 * hp + 1] = s2[tq:]

    def softmax(j, buf, hs):
        qs = tile(j)
        var = jnp.where(jnp.logical_and(first, j == 0), 0,
                        jnp.where(jnp.logical_and(last, j == nsub - 1), 2, 1))
        if hs == 0:
            m_ref[pl.ds(qs, tq), :] = jnp.zeros((tq, 128), F32)
            den_ref[pl.ds(qs, tq), :] = jnp.ones((tq, 128), F32)
        for hp in head_pairs(hs):
            for r0 in (0, half):
                dens = []
                for h in (2 * hp, 2 * hp + 1):
                    s = s_scr[buf, h, r0:r0 + half, :] + bias_ref[var, h, r0:r0 + half, :]
                    m = jnp.max(s, axis=-1, keepdims=True)
                    p = jnp.exp(s - m)
                    den = jnp.sum(p, axis=-1, keepdims=True)
                    p_scr[buf, h // gh, r0:r0 + half, (h % gh) * tk:(h % gh + 1) * tk] = p.astype(BF16)
                    m_ref[pl.ds(qs + r0, half), h:h + 1] = m
                    den_ref[pl.ds(qs + r0, half), h:h + 1] = den
                    dens.append(den)
                pair = hp % (gh // 2)
                inv_scr[buf, (2 * hp) // gh, r0:r0 + half, pair * 128:(pair + 1) * 128] = (
                    1.0 / jnp.where(lower_half, dens[0], dens[1]))

    def outputs(j, buf, hs):
        qs = tile(j)
        for g in range(hs * ngroup // nsets, (hs + 1) * ngroup // nsets):
            cg = slice(g * gw, (g + 1) * gw)
            vw = vbuf[pl.ds(qs, tk), cg]
            v_stack = jnp.concatenate([jnp.where(slot == s, vw, jnp.zeros_like(vw))
                                       for s in range(gh)], axis=0)
            o = _dot(p_scr[buf, g], v_stack)
            o_ref[pl.ds(qs, tq), cg] = (o * inv_scr[buf, g]).astype(BF16)

    for hs in range(nsets):
        def one_tile(j, carry, hs=hs):
            logits(j, 0, hs)
            softmax(j, 0, hs)
            outputs(j, 0, hs)
            return carry

        lax.fori_loop(0, nsub, one_tile, 0)


def _attn_pattern(qkv, bias, batch, seq, dilation):
    w = ATT_STEPS
    L = seq // dilation
    tb = min(1024, L)
    nblk = L // tb
    hb = tb // w
    nhalo = L // w

    def main(j):
        return pl.BlockSpec((None, tb, ATT_WIDTH), lambda b, r, i: (b, i, j * dilation + r))

    def prev(j):
        return pl.BlockSpec((None, w, ATT_WIDTH),
                            lambda b, r, i: (b, jnp.maximum(i * hb - 1, 0), j * dilation + r))

    def nxt(j):
        return pl.BlockSpec((None, w, ATT_WIDTH),
                            lambda b, r, i: (b, jnp.minimum((i + 1) * hb, nhalo - 1), j * dilation + r))

    nk, gh = tb + 2 * w, ATT_GROUP_HEADS
    return pl.pallas_call(
        functools.partial(_attn_kernel, tb=tb),
        grid=(batch, dilation, nblk),
        in_specs=[main(0), prev(1), main(1), nxt(1), prev(2), main(2), nxt(2),
                  pl.BlockSpec((3, ATT_HEADS, ATT_TQ, ATT_TK), lambda b, r, i: (0, 0, 0, 0),
                               pipeline_mode=pl.Buffered(1))],
        out_specs=[pl.BlockSpec((None, tb, ATT_WIDTH), lambda b, r, i: (b, i, r)),
                   pl.BlockSpec((None, tb, 128), lambda b, r, i: (b, i, r)),
                   pl.BlockSpec((None, tb, 128), lambda b, r, i: (b, i, r))],
        out_shape=[jax.ShapeDtypeStruct((batch, L, dilation * ATT_WIDTH), BF16),
                   jax.ShapeDtypeStruct((batch, L, dilation * 128), F32),
                   jax.ShapeDtypeStruct((batch, L, dilation * 128), F32)],
        scratch_shapes=[pltpu.VMEM((nk, ATT_WIDTH), BF16),
                        pltpu.VMEM((nk, ATT_WIDTH), BF16),
                        pltpu.VMEM((1, ATT_HEADS, ATT_TQ, ATT_TK), F32),
                        pltpu.VMEM((1, ATT_HEADS // gh, ATT_TQ, gh * ATT_TK), BF16),
                        pltpu.VMEM((1, ATT_HEADS // gh, ATT_TQ, gh * ATT_HEAD_DIM), F32)],
        compiler_params=_params(("arbitrary", "arbitrary", "arbitrary")),
        name=f"attn_d{dilation}",
    )(qkv, qkv, qkv, qkv, qkv, qkv, qkv, bias)


def _merge_kernel(*refs, tm):
    nres = len(RESIDUE_DILATIONS)
    o1_ref, m1_ref, d1_ref = refs[:3]
    res = [refs[3 + 3 * n:6 + 3 * n] for n in range(nres)]
    ag_ref, e_ref, out_ref = refs[3 + 3 * nres:6 + 3 * nres]
    scr = [refs[6 + 3 * nres + 3 * n:9 + 3 * nres + 3 * n] for n in range(nres)]
    tmp = refs[6 + 6 * nres]

    ncol = ATT_WIDTH // 128
    for d, (o_ref, m_ref, d_ref), (so, sm, sd) in zip(RESIDUE_DILATIONS, res, scr):
        prev = d // 4 if d > 4 else 1
        for r in range(d):
            rp, a = r % prev, r // prev
            for c in range(ncol):
                col = r * ATT_WIDTH + c * 128
                rows = o_ref[:, col:col + 128].astype(F32)
                if prev == 1:
                    so[c, pl.ds(r, tm // d, stride=d), :] = rows
                else:
                    tmp[c, pl.ds(rp * (tm // prev) + a, tm // d, stride=d // prev), :] = rows
            sm[pl.ds(r, tm // d, stride=d), :] = m_ref[:, r * 128:(r + 1) * 128]
            sd[pl.ds(r, tm // d, stride=d), :] = d_ref[:, r * 128:(r + 1) * 128]
        if prev > 1:
            n = tm // prev
            for rp in range(prev):
                for c in range(ncol):
                    so[c, pl.ds(rp, n, stride=prev), :] = tmp[c, rp * n:(rp + 1) * n, :]

    rc, gw = 128, 256
    head_lane = lax.broadcasted_iota(jnp.int32, (rc, 128), 1) < ATT_HEADS
    for r0 in range(0, tm, rc):
        rows = slice(r0, r0 + rc)
        lses = ([m1_ref[rows, :] + jnp.log(d1_ref[rows, :])]
                + [sm[rows, :] + jnp.log(sd[rows, :]) for _, sm, sd in scr])
        mx = functools.reduce(jnp.maximum, lses)
        es = [jnp.exp(l - mx) for l in lses]
        inv = 1.0 / functools.reduce(jnp.add, es)
        packed = []
        for ei in es:
            wgt = jnp.where(head_lane, ei * inv, 0.0)
            hi = wgt.astype(BF16).astype(F32)
            lo = (wgt - hi).astype(BF16).astype(F32)
            packed.append((hi + pltpu.roll(lo, ATT_HEADS, 1)).astype(BF16))
        for c0 in range(0, ATT_WIDTH, gw):
            cols = slice(c0, c0 + gw)
            e = e_ref[:, cols]
            outs = [o1_ref[rows, cols].astype(F32)] + [
                jnp.concatenate([so[c, rows, :] for c in range(c0 // 128, (c0 + gw) // 128)], axis=1)
                for so, _, _ in scr]
            att = functools.reduce(jnp.add, [_dot(w, e) * o for w, o in zip(packed, outs)])
            out_ref[rows, cols] = (att * _silu(ag_ref[rows, cols].astype(F32))).astype(BF16)


def _merge(nat, res, proj):
    M = proj.shape[0]
    tm = 512
    expand = np.zeros((128, ATT_WIDTH), np.float32)
    for h in range(ATT_HEADS):
        expand[h, h * ATT_HEAD_DIM:(h + 1) * ATT_HEAD_DIM] = 1.0
        expand[ATT_HEADS + h, h * ATT_HEAD_DIM:(h + 1) * ATT_HEAD_DIM] = 1.0
    row = lambda rows, width: pl.BlockSpec((rows, width), lambda i: (i, 0))
    triple = lambda d: [row(tm // d, d * ATT_WIDTH), row(tm // d, d * 128), row(tm // d, d * 128)]
    in_specs = (triple(1) + [s for d in RESIDUE_DILATIONS for s in triple(d)]
                + [pl.BlockSpec((tm, ATT_WIDTH), lambda i: (i, COL_AG // ATT_WIDTH)),
                   pl.BlockSpec((128, ATT_WIDTH), lambda i: (0, 0))])
    scratch = []
    for _ in RESIDUE_DILATIONS:
        scratch += [pltpu.VMEM((ATT_WIDTH // 128, tm, 128), F32),
                    pltpu.VMEM((tm, 128), F32), pltpu.VMEM((tm, 128), F32)]
    scratch.append(pltpu.VMEM((ATT_WIDTH // 128, tm, 128), F32))
    return pl.pallas_call(
        functools.partial(_merge_kernel, tm=tm),
        grid=(M // tm,),
        in_specs=in_specs,
        out_specs=row(tm, ATT_WIDTH),
        out_shape=jax.ShapeDtypeStruct((M, ATT_WIDTH), BF16),
        scratch_shapes=scratch,
        compiler_params=_params(("arbitrary",)),
        name="merge",
    )(*nat, *[a for t in res for a in t], proj, jnp.asarray(expand, BF16))


def _gla_kernel(*refs, ts, nstep):
    fwd_in, bwd_in = refs[:5], refs[5:10]
    (upf_ref, upb_ref, gbf_ref, gbb_ref, gain_ref, out_ref,
     state, o_acc, qf_scr, kd_scr, ks_scr, oin_scr, st_scr) = refs[10:]
    C = GLA_CHUNK
    nchunk = ts // C
    chunks = [slice(c * C, (c + 1) * C) for c in range(nchunk)]
    i = pl.program_id(2)
    dirs = [(0, False, fwd_in, upf_ref, gbf_ref), (1, True, bwd_in, upb_ref, gbb_ref)]

    @pl.when(i == 0)
    def _():
        state[...] = jnp.zeros_like(state)


    log_gs = []
    for d, reverse, (q_ref, k_ref, v_ref, lr_ref, gg_ref), up_ref, gb_ref in dirs:
        lr_hi, lr_lo = _split_bf16(lr_ref[...])
        up_hi, up_lo = _split_bf16(up_ref[...])
        z = _dot(lr_hi, up_hi) + _dot(lr_hi, up_lo) + _dot(lr_lo, up_hi) + gb_ref[...]
        log_gs.append((jnp.minimum(z, 0.0) - jnp.log(1.0 + jnp.exp(-jnp.abs(z))))
                      * (1.0 / GLA_GATE_NORM))

    ri = lax.broadcasted_iota(jnp.int32, (C, C), 0)
    ci = lax.broadcasted_iota(jnp.int32, (C, C), 1)
    dec_cols = [[], []]
    for d, reverse, (q_ref, k_ref, v_ref, lr_ref, gg_ref), up_ref, gb_ref in dirs:
        tri = jnp.where((ci >= ri) if reverse else (ci <= ri), 1.0, 0.0).astype(BF16)
        for rows in chunks:
            g_hi, g_lo = _split_bf16(log_gs[d][rows])
            cum = _dot(tri, jnp.concatenate([g_hi, g_lo], axis=1))
            b = cum[:, :GLA_DK] + cum[:, GLA_DK:]
            b_edge = b[0:1] if reverse else b[C - 1:C]
            q = q_ref[rows, :].astype(F32)
            k = k_ref[rows, :].astype(F32)
            edge = jnp.exp(b_edge)
            kd = k * jnp.exp(-b)
            qf_scr[d, rows, :] = (q * jnp.exp(b + math.log(GLA_DK ** -0.5))).astype(BF16)
            kd_scr[d, rows, :] = kd.astype(BF16)
            ks_scr[d, rows, :] = (kd * edge).astype(BF16)
            dec = jnp.broadcast_to(edge, (GLA_DK, GLA_DK)).T
            dec_cols[d].append(jnp.concatenate([dec, dec], axis=1))

    atts = [[jnp.where((ci >= ri) if reverse else (ci <= ri),
                       _dot_nt(qf_scr[d, rows, :], kd_scr[d, rows, :]), 0.0).astype(BF16)
             for rows in chunks] for d, reverse, *_ in dirs]

    kvs = [[], []]
    for d, reverse, (q_ref, k_ref, v_ref, lr_ref, gg_ref), up_ref, gb_ref in dirs:
        for rows, att in zip(chunks, atts[d]):
            v = v_ref[rows, :]
            oin_scr[d, rows, :] = _dot(att, v)
            kvs[d].append(_dot_tn(ks_scr[d, rows, :], v))

    orders = [list(range(nchunk)), list(range(nchunk - 1, -1, -1))]
    for d, reverse, *_ in dirs:
        st = state[d]
        for c in orders[d]:
            st_scr[d, c] = st.astype(BF16)
            st = st * dec_cols[d][c] + kvs[d][c]
        state[d] = st

    for d, reverse, *_ in dirs:
        for c in orders[d]:
            rows = chunks[c]
            oin_scr[d, rows, :] = oin_scr[d, rows, :] + _dot(qf_scr[d, rows, :], st_scr[d, c])

    blocks = [i, nstep - 1 - i]

    @pl.when(i < nstep // 2)
    def _():
        for d, reverse, *_ in dirs:
            base = pl.multiple_of(blocks[d] * ts, ts)
            for rows in chunks:
                o_acc[pl.ds(base + rows.start, C), :] = oin_scr[d, rows, :]

    @pl.when(i >= nstep // 2)
    def _():
        for d, reverse, (q_ref, k_ref, v_ref, lr_ref, gg_ref), up_ref, gb_ref in dirs:
            base = pl.multiple_of(blocks[d] * ts, ts)
            for rows in chunks:
                dst = pl.ds(base + rows.start, C)
                tot = oin_scr[d, rows, :] + o_acc[dst, :]
                ms = jnp.mean(tot * tot, axis=-1, keepdims=True)
                g_o = tot * lax.rsqrt(ms + EPS) * gain_ref[...]
                out_ref[dst, :] = (g_o * _silu(gg_ref[rows, :].astype(F32))).astype(BF16)


def _gla(proj, lr, up_f, up_b, bias_f, bias_b, gain, batch, seq):
    ts = 2048
    nstep = seq // ts
    assert nstep % 2 == 0
    C = GLA_CHUNK
    p3 = proj.reshape(batch, seq, PROJ_WIDTH)
    lr3 = lr.reshape(batch, seq, LR_PAD)

    def direction_specs(step):
        def seq_block(width, col0):
            return pl.BlockSpec((None, ts, width), lambda b, h, i: (b, step(i), col0 // width + h))
        return [seq_block(GLA_DK, COL_GQ), seq_block(GLA_DK, COL_GK), seq_block(GLA_DV, COL_GV),
                pl.BlockSpec((None, ts, LR_PAD), lambda b, h, i: (b, step(i), 0)),
                seq_block(GLA_DV, COL_GG)]

    per_head = lambda rows, width: pl.BlockSpec((rows, width), lambda b, h, i: (0, h))
    in_specs = (direction_specs(lambda i: i) + direction_specs(lambda i: nstep - 1 - i)
                + [per_head(LR_PAD, GLA_DK), per_head(LR_PAD, GLA_DK),
                   per_head(1, GLA_DK), per_head(1, GLA_DK), per_head(1, GLA_DV)])
    dir_args = [p3, p3, p3, lr3, p3]
    return pl.pallas_call(
        functools.partial(_gla_kernel, ts=ts, nstep=nstep),
        grid=(batch, GLA_HEADS, nstep),
        in_specs=in_specs,
        out_specs=pl.BlockSpec((None, seq, GLA_DV), lambda b, h, i: (b, 0, h)),
        out_shape=jax.ShapeDtypeStruct((batch, seq, GLA_WIDTH), BF16),
        scratch_shapes=[pltpu.VMEM((2, GLA_DK, GLA_DV), F32),
                        pltpu.VMEM((seq, GLA_DV), F32),
                        pltpu.VMEM((2, ts, GLA_DK), BF16),
                        pltpu.VMEM((2, ts, GLA_DK), BF16),
                        pltpu.VMEM((2, ts, GLA_DK), BF16),
                        pltpu.VMEM((2, ts, GLA_DV), F32),
                        pltpu.VMEM((2, ts // C, GLA_DK, GLA_DV), BF16)],
        compiler_params=_params(("arbitrary", "arbitrary", "arbitrary")),
        name="gla",
    )(*dir_args, *dir_args, up_f, up_b, bias_f.reshape(1, GLA_KEY_WIDTH),
      bias_b.reshape(1, GLA_KEY_WIDTH), gain.reshape(1, GLA_WIDTH))


def _outproj_kernel(a_ref, g_ref, w_ref, x_ref, gate_ref, fg_ref, o_ref, *, final):
    y = _dot(jnp.concatenate([a_ref[...], g_ref[...]], axis=1), w_ref[...])
    xn = x_ref[...] + gate_ref[...] * y
    if final:
        ms = jnp.mean(xn * xn, axis=-1, keepdims=True)
        xn = xn * lax.rsqrt(ms + EPS) * fg_ref[...]
    o_ref[...] = xn


def _outproj(a_out, g_out, w_out_bf16, x2, gate, final_gain, seq, final):
    M, D = x2.shape
    tm = 512
    bpt = seq // tm
    return pl.pallas_call(
        functools.partial(_outproj_kernel, final=final),
        grid=(M // tm,),
        in_specs=[pl.BlockSpec((tm, ATT_WIDTH), lambda i: (i, 0)),
                  pl.BlockSpec((tm, GLA_WIDTH), lambda i: (i, 0)),
                  pl.BlockSpec((ATT_WIDTH + GLA_WIDTH, D), lambda i: (0, 0)),
                  pl.BlockSpec((tm, D), lambda i: (i, 0)),
                  pl.BlockSpec((None, 1, D), lambda i: (i // bpt, 0, 0)),
                  pl.BlockSpec((1, D), lambda i: (0, 0))],
        out_specs=pl.BlockSpec((tm, D), lambda i: (i, 0)),
        out_shape=jax.ShapeDtypeStruct((M, D), F32),
        compiler_params=_params(("arbitrary",)),
        name="outproj",
    )(a_out, g_out, w_out_bf16, x2, gate, final_gain.reshape(1, D))


def kernel(x, c, w_cond, b_cond, w_in, gla_gate_up_fwd, gla_gate_bias_fwd, gla_gate_up_bwd,
           gla_gate_bias_bwd, gla_norm_gain, rel_bias, w_out, final_gain):
    B, S, D = x.shape
    depth = w_cond.shape[0]
    R = GLA_GATE_RANK
    xs = x.reshape(B * S, D)
    for layer in range(depth):
        mod = _mod(c, w_cond[layer], b_cond[layer])
        shift, scale, gate = [m.reshape(B, 1, D) for m in jnp.split(mod, 3, axis=-1)]

        proj, lr, *res_qkv = _inproj(xs, scale, shift, _wprep(w_in[layer]), S)

        def rows2d(t, d):
            return [a.reshape(B * S // d, -1) for a in t]

        nat = rows2d(_attn_pattern(proj.reshape(B, S, PROJ_WIDTH), _bias_tiles(rel_bias, 1), B, S, 1), 1)
        res = [rows2d(_attn_pattern(qkv.reshape(B, S // d, -1), _bias_tiles(rel_bias, d), B, S, d), d)
               for d, qkv in zip(RESIDUE_DILATIONS, res_qkv)]
        a_out = _merge(nat, res, proj)

        up_f = jnp.pad(gla_gate_up_fwd[layer], ((0, LR_PAD - R), (0, 0)))
        up_b = jnp.pad(gla_gate_up_bwd[layer], ((R, LR_PAD - 2 * R), (0, 0)))
        g_out = _gla(proj, lr, up_f, up_b, gla_gate_bias_fwd[layer], gla_gate_bias_bwd[layer],
                     gla_norm_gain[layer], B, S)

        xs = _outproj(a_out, g_out.reshape(B * S, GLA_WIDTH), w_out[layer].astype(BF16),
                      xs, gate, final_gain, S, final=layer == depth - 1)
    return xs.reshape(B, S, D)
```

```python
import functools
import math

import jax
import jax.numpy as jnp
import numpy as np
from jax import lax
from jax.experimental import pallas as pl
from jax.experimental.pallas import tpu as pltpu

D_MODEL = 2048
ATT_WIDTH = 1024
ATT_HEADS = 16
ATT_HEAD_DIM = 64
DILATED_PATTERNS = ((128, 1), (512, 4), (2048, 16))
ATT_STEPS = 64
GLA_WIDTH = 1024
GLA_HEADS = 4
GLA_KEY_WIDTH = 512
GLA_DK = 128
GLA_DV = 256
GLA_GATE_RANK = 16
GLA_GATE_NORM = 16.0
GLA_CHUNK = 64
REL_BUCKETS = 32
REL_MAX_DIST = 1024
EPS = 1e-6
NEG_INF = -1e30

PROJ_WIDTH = 4 * ATT_WIDTH + 2 * GLA_KEY_WIDTH + 2 * GLA_WIDTH
COL_AQ, COL_AK, COL_AV, COL_AG = 0, 1024, 2048, 3072
COL_GQ, COL_GK, COL_GV, COL_GG = 4096, 4608, 5120, 6144
LR_PAD = 128
ATT_QKV_TILES = 3
RESIDUE_DILATIONS = tuple(d for _, d in DILATED_PATTERNS if d > 1)
ATT_TQ = 128
ATT_TK = ATT_TQ + 2 * ATT_STEPS
ATT_GROUP_HEADS = 4
ATT_HEAD_SETS = 1

VMEM_LIMIT = 56 * 1024 * 1024

BF16 = jnp.bfloat16
F32 = jnp.float32


def _params(sem):
    return pltpu.CompilerParams(dimension_semantics=sem, vmem_limit_bytes=VMEM_LIMIT)


def _dot(a, b):
    return jnp.dot(a, b, preferred_element_type=F32)


def _dot_nt(a, b):
    return lax.dot_general(a, b, (((1,), (1,)), ((), ())), preferred_element_type=F32)


def _dot_tn(a, b):
    return lax.dot_general(a, b, (((0,), (0,)), ((), ())), preferred_element_type=F32)


def _split_bf16(x):
    hi = x.astype(BF16)
    lo = (x - hi.astype(F32)).astype(BF16)
    return hi, lo


def _silu(x):
    return x / (1.0 + jnp.exp(-x))


def _mod_kernel(c_ref, w_ref, b_ref, o_ref):
    s_hi, s_lo = _split_bf16(_silu(c_ref[...]))
    w_hi, w_lo = _split_bf16(w_ref[...])
    o_ref[...] = _dot(s_hi, w_hi) + _dot(s_lo, w_hi) + _dot(s_hi, w_lo) + b_ref[...]


def _mod(c, w_cond, b_cond):
    B, D = c.shape
    N = w_cond.shape[1]
    tn = 768
    cp = jnp.pad(c, ((0, 8 - B), (0, 0)))
    out = pl.pallas_call(
        _mod_kernel,
        grid=(N // tn,),
        in_specs=[pl.BlockSpec((8, D), lambda j: (0, 0)),
                  pl.BlockSpec((D, tn), lambda j: (0, j)),
                  pl.BlockSpec((1, tn), lambda j: (0, j))],
        out_specs=pl.BlockSpec((8, tn), lambda j: (0, j)),
        out_shape=jax.ShapeDtypeStruct((8, N), F32),
        compiler_params=_params(("arbitrary",)),
        name="mod",
    )(cp, w_cond, b_cond.reshape(1, N))
    return out[:B]


def _wprep_kernel(w_ref, lr_ref, o_ref, *, tn, ntile):
    j = pl.program_id(0)

    @pl.when(j < ntile)
    def _():
        col = j * tn + lax.broadcasted_iota(jnp.int32, (1, tn), 1)
        scale = jnp.where(col < ATT_WIDTH, ATT_HEAD_DIM ** -0.5, 1.0)
        o_ref[...] = (w_ref[...] * scale).astype(BF16)

    @pl.when(j == ntile)
    def _():
        o_ref[:, :LR_PAD] = lr_ref[...].astype(BF16)


def _wprep(w_all, layer):
    _, D, ncols = w_all.shape
    tn = ATT_WIDTH
    ntile = PROJ_WIDTH // tn
    main = w_all[layer, :, :PROJ_WIDTH]
    lr = jnp.pad(w_all[layer, :, PROJ_WIDTH:], ((0, 0), (0, PROJ_WIDTH + LR_PAD - ncols)))
    return pl.pallas_call(
        functools.partial(_wprep_kernel, tn=tn, ntile=ntile),
        grid=(ntile + 1,),
        in_specs=[pl.BlockSpec((D, tn), lambda j: (0, jnp.minimum(j, ntile - 1))),
                  pl.BlockSpec((D, LR_PAD), lambda j: (0, 0))],
        out_specs=pl.BlockSpec((D, tn), lambda j: (0, j)),
        out_shape=jax.ShapeDtypeStruct((D, PROJ_WIDTH + LR_PAD), BF16),
        compiler_params=_params(("arbitrary",)),
        name="wprep",
    )(main, lr)


def _inproj_kernel(x_ref, scale_ref, shift_ref, w_ref, p_ref, lr_ref, *rest, tm, tn):
    nres = len(RESIDUE_DILATIONS)
    res_refs, h_scr, acc_scr = rest[:nres], rest[nres], rest[nres + 1:]
    j = pl.program_id(1)

    @pl.when(j == 0)
    def _():
        x = x_ref[...]
        ms = jnp.mean(x * x, axis=-1, keepdims=True)
        h = x * lax.rsqrt(ms + EPS) * (1.0 + scale_ref[...]) + shift_ref[...]
        hb = h.astype(BF16)
        h_scr[...] = hb
        lr_ref[...] = _dot(hb, w_ref[:, PROJ_WIDTH:PROJ_WIDTH + LR_PAD])

    @pl.when(j >= ATT_QKV_TILES)
    def _():
        w = w_ref[:, pl.ds(pl.multiple_of(j * tn, tn), tn)]
        p_ref[...] = _dot(h_scr[...], w).astype(BF16)

    @pl.when(j < ATT_QKV_TILES)
    def _():
        h = h_scr[...]
        chunk = 256
        for c0 in range(0, tn, chunk):
            acc = _dot(h, w_ref[:, pl.ds(pl.multiple_of(j * tn + c0, chunk), chunk)])
            p_ref[:, c0:c0 + chunk] = acc.astype(BF16)
            for c in range(c0 // 128, (c0 + chunk) // 128):
                lanes = slice(c * 128 - c0, (c + 1) * 128 - c0)
                src, prev_d = acc_scr[0], 1
                src[c] = acc[:, lanes]
                for lvl, (ref, d) in enumerate(zip(res_refs, RESIDUE_DILATIONS)):
                    ratio, n = d // prev_d, tm // d
                    dst = acc_scr[lvl + 1] if lvl + 1 < len(RESIDUE_DILATIONS) else None
                    for rp in range(prev_d):
                        for a in range(ratio):
                            r = rp + prev_d * a
                            rows = src[c, pl.ds(rp * (tm // prev_d) + a, n, stride=ratio), :]
                            ref[:, r * tn + c * 128:r * tn + (c + 1) * 128] = rows.astype(BF16)
                            if dst is not None:
                                dst[c, r * n:(r + 1) * n, :] = rows
                    src, prev_d = dst, d


def _inproj(x2, scale, shift, w_main, seq):
    M, D = x2.shape
    tm, tn = 512, ATT_WIDTH
    bpt = seq // tm
    last_qkv = ATT_QKV_TILES - 1
    res_specs = [pl.BlockSpec((tm // d, d * tn), lambda i, j: (i, jnp.minimum(j, last_qkv)))
                 for d in RESIDUE_DILATIONS]
    res_shapes = [jax.ShapeDtypeStruct((M // d, ATT_QKV_TILES * d * tn), BF16)
                  for d in RESIDUE_DILATIONS]
    return pl.pallas_call(
        functools.partial(_inproj_kernel, tm=tm, tn=tn),
        grid=(M // tm, PROJ_WIDTH // tn),
        in_specs=[pl.BlockSpec((tm, D), lambda i, j: (i, 0)),
                  pl.BlockSpec((None, 1, D), lambda i, j: (i // bpt, 0, 0)),
                  pl.BlockSpec((None, 1, D), lambda i, j: (i // bpt, 0, 0)),
                  pl.BlockSpec(w_main.shape, lambda i, j: (0, 0), pipeline_mode=pl.Buffered(1))],
        out_specs=[pl.BlockSpec((tm, tn), lambda i, j: (i, j)),
                   pl.BlockSpec((tm, LR_PAD), lambda i, j: (i, 0))] + res_specs,
        out_shape=[jax.ShapeDtypeStruct((M, PROJ_WIDTH), BF16),
                   jax.ShapeDtypeStruct((M, LR_PAD), F32)] + res_shapes,
        scratch_shapes=[pltpu.VMEM((tm, D), BF16)]
                       + [pltpu.VMEM((tn // 128, tm, 128), F32) for _ in RESIDUE_DILATIONS],
        compiler_params=_params(("arbitrary", "arbitrary")),
        name="inproj",
    )(x2, scale, shift, w_main)


def _t5_bucket_np(rel):
    nb = REL_BUCKETS // 2
    max_exact = nb // 2
    n = np.abs(rel)
    large = max_exact + (np.log(np.maximum(n, 1) / max_exact)
                         / np.log(REL_MAX_DIST / max_exact) * (nb - max_exact)).astype(np.int32)
    large = np.minimum(large, nb - 1)
    return (np.where(rel > 0, nb, 0) + np.where(n < max_exact, n, large)).astype(np.int32)


def _bias_kernel(rbt_ref, bucket_ref, mask_ref, o_ref):
    rbt = rbt_ref[...]
    bucket = bucket_ref[...]
    ids = lax.broadcasted_iota(jnp.int32, (REL_BUCKETS, bucket.shape[1]), 0)
    onehot = jnp.where(ids == bucket, 1.0, 0.0).astype(BF16)
    hi = rbt.astype(BF16)
    rest = rbt - hi.astype(F32)
    mid = rest.astype(BF16)
    lo = (rest - mid.astype(F32)).astype(BF16)
    tbl = _dot(hi, onehot) + _dot(mid, onehot) + _dot(lo, onehot)
    for v in range(3):
        o_ref[v] = jnp.where(mask_ref[v] > 0.5, tbl, NEG_INF)


def _bias_tiles(rel_bias, dilation):
    w, tq, tk = ATT_STEPS, ATT_TQ, ATT_TK
    qi = np.arange(tq)[:, None]
    kj = np.arange(tk)[None, :]
    step = kj - w - qi
    band = np.abs(step) <= w
    bucket = _t5_bucket_np(step * dilation).reshape(1, tq * tk)
    masks = np.stack([band & (kj >= w), band, band & (kj < tk - w)]).astype(np.float32)
    masks = masks.reshape(3, 1, tq * tk)
    out = pl.pallas_call(
        _bias_kernel,
        out_shape=jax.ShapeDtypeStruct((3, ATT_HEADS, tq * tk), F32),
        compiler_params=pltpu.CompilerParams(vmem_limit_bytes=VMEM_LIMIT),
        name=f"bias_d{dilation}",
    )(rel_bias.T, jnp.asarray(bucket), jnp.asarray(masks))
    return out.reshape(3, ATT_HEADS, tq, tk)


def _attn_kernel(q_ref, kp_ref, km_ref, kn_ref, vp_ref, vm_ref, vn_ref, bias_ref,
                 o_ref, m_ref, den_ref, kbuf, vbuf, s_scr, p_scr, inv_scr, *, tb):
    w, tq, tk = ATT_STEPS, ATT_TQ, ATT_TK
    nsub = tb // tq
    npair = ATT_HEADS // 2
    gh = ATT_GROUP_HEADS
    gw = gh * ATT_HEAD_DIM
    ngroup = ATT_HEADS // gh
    nk = tb + 2 * w
    i = pl.program_id(2)
    first = i == 0
    last = i == pl.num_programs(2) - 1

    row = 0
    for kpart, vpart in ((kp_ref, vp_ref), (km_ref, vm_ref), (kn_ref, vn_ref)):
        rows = slice(row, row + kpart.shape[0])
        kbuf[rows] = kpart[...]
        vbuf[rows] = vpart[...]
        row += kpart.shape[0]
    slot = lax.broadcasted_iota(jnp.int32, (1, gw), 1) // ATT_HEAD_DIM

    half = tq // 2
    lower_half = lax.broadcasted_iota(jnp.int32, (half, 128), 1) < ATT_HEAD_DIM
    lower_q = lax.broadcasted_iota(jnp.int32, (tq, 128), 1) < ATT_HEAD_DIM

    nsets = ATT_HEAD_SETS

    def head_pairs(hs):
        return range(hs * npair // nsets, (hs + 1) * npair // nsets)

    def tile(j):
        return pl.multiple_of(j * tq, tq)

    def logits(j, buf, hs):
        qs = tile(j)
        for hp in head_pairs(hs):
            cp = slice(hp * 128, (hp + 1) * 128)
            q = q_ref[pl.ds(qs, tq), cp]
            zero = jnp.zeros_like(q)
            q2 = jnp.concatenate([jnp.where(lower_q, q, zero), jnp.where(lower_q, zero, q)], axis=0)
            s2 = _dot_nt(q2, kbuf[pl.ds(qs, tk), cp])
            s_scr[buf, 2 * hp] = s2[:tq]
            s_scr[buf, 2 * hp + 1] = s2[tq:]

    def softmax(j, buf, hs):
        qs = tile(j)
        var = jnp.where(jnp.logical_and(first, j == 0), 0,
                        jnp.where(jnp.logical_and(last, j == nsub - 1), 2, 1))
        if hs == 0:
            m_ref[pl.ds(qs, tq), :] = jnp.zeros((tq, 128), F32)
            den_ref[pl.ds(qs, tq), :] = jnp.ones((tq, 128), F32)
        for hp in head_pairs(hs):
            for r0 in (0, half):
                dens = []
                for h in (2 * hp, 2 * hp + 1):
                    s = s_scr[buf, h, r0:r0 + half, :] + bias_ref[var, h, r0:r0 + half, :]
                    m = jnp.max(s, axis=-1, keepdims=True)
                    p = jnp.exp(s - m)
                    den = jnp.sum(p, axis=-1, keepdims=True)
                    p_scr[buf, h // gh, r0:r0 + half, (h % gh) * tk:(h % gh + 1) * tk] = p.astype(BF16)
                    m_ref[pl.ds(qs + r0, half), h:h + 1] = m
                    den_ref[pl.ds(qs + r0, half), h:h + 1] = den
                    dens.append(den)
                pair = hp % (gh // 2)
                inv_scr[buf, (2 * hp) // gh, r0:r0 + half, pair * 128:(pair + 1) * 128] = (
                    1.0 / jnp.where(lower_half, dens[0], dens[1]))

    def outputs(j, buf, hs):
        qs = tile(j)
        for g in range(hs * ngroup // nsets, (hs + 1) * ngroup // nsets):
            cg = slice(g * gw, (g + 1) * gw)
            vw = vbuf[pl.ds(qs, tk), cg]
            v_stack = jnp.concatenate([jnp.where(slot == s, vw, jnp.zeros_like(vw))
                                       for s in range(gh)], axis=0)
            o = _dot(p_scr[buf, g], v_stack)
            o_ref[pl.ds(qs, tq), cg] = (o * inv_scr[buf, g]).astype(BF16)

    for hs in range(nsets):
        def one_tile(j, carry, hs=hs):
            logits(j, 0, hs)
            softmax(j, 0, hs)
            outputs(j, 0, hs)
            return carry

        lax.fori_loop(0, nsub, one_tile, 0)


def _attn_pattern(qkv, bias, batch, seq, dilation):
    w = ATT_STEPS
    L = seq // dilation
    tb = min(1024, L)
    nblk = L // tb
    hb = tb // w
    nhalo = L // w

    def main(j):
        return pl.BlockSpec((None, tb, ATT_WIDTH), lambda b, r, i: (b, i, j * dilation + r))

    def prev(j):
        return pl.BlockSpec((None, w, ATT_WIDTH),
                            lambda b, r, i: (b, jnp.maximum(i * hb - 1, 0), j * dilation + r))

    def nxt(j):
        return pl.BlockSpec((None, w, ATT_WIDTH),
                            lambda b, r, i: (b, jnp.minimum((i + 1) * hb, nhalo - 1), j * dilation + r))

    nk, gh = tb + 2 * w, ATT_GROUP_HEADS
    return pl.pallas_call(
        functools.partial(_attn_kernel, tb=tb),
        grid=(batch, dilation, nblk),
        in_specs=[main(0), prev(1), main(1), nxt(1), prev(2), main(2), nxt(2),
                  pl.BlockSpec((3, ATT_HEADS, ATT_TQ, ATT_TK), lambda b, r, i: (0, 0, 0, 0),
                               pipeline_mode=pl.Buffered(1))],
        out_specs=[pl.BlockSpec((None, tb, ATT_WIDTH), lambda b, r, i: (b, i, r)),
                   pl.BlockSpec((None, tb, 128), lambda b, r, i: (b, i, r)),
                   pl.BlockSpec((None, tb, 128), lambda b, r, i: (b, i, r))],
        out_shape=[jax.ShapeDtypeStruct((batch, L, dilation * ATT_WIDTH), BF16),
                   jax.ShapeDtypeStruct((batch, L, dilation * 128), F32),
                   jax.ShapeDtypeStruct((batch, L, dilation * 128), F32)],
        scratch_shapes=[pltpu.VMEM((nk, ATT_WIDTH), BF16),
                        pltpu.VMEM((nk, ATT_WIDTH), BF16),
                        pltpu.VMEM((1, ATT_HEADS, ATT_TQ, ATT_TK), F32),
                        pltpu.VMEM((1, ATT_HEADS // gh, ATT_TQ, gh * ATT_TK), BF16),
                        pltpu.VMEM((1, ATT_HEADS // gh, ATT_TQ, gh * ATT_HEAD_DIM), F32)],
        compiler_params=_params(("arbitrary", "arbitrary", "arbitrary")),
        name=f"attn_d{dilation}",
    )(qkv, qkv, qkv, qkv, qkv, qkv, qkv, bias)


def _merge_kernel(*refs, tm):
    nres = len(RESIDUE_DILATIONS)
    o1_ref, m1_ref, d1_ref = refs[:3]
    res = [refs[3 + 3 * n:6 + 3 * n] for n in range(nres)]
    ag_ref, e_ref, out_ref = refs[3 + 3 * nres:6 + 3 * nres]
    scr = [refs[6 + 3 * nres + 3 * n:9 + 3 * nres + 3 * n] for n in range(nres)]
    tmp = refs[6 + 6 * nres]

    ncol = ATT_WIDTH // 128
    for d, (o_ref, m_ref, d_ref), (so, sm, sd) in zip(RESIDUE_DILATIONS, res, scr):
        prev = d // 4 if d > 4 else 1
        for r in range(d):
            rp, a = r % prev, r // prev
            for c in range(ncol):
                col = r * ATT_WIDTH + c * 128
                rows = o_ref[:, col:col + 128].astype(F32)
                if prev == 1:
                    so[c, pl.ds(r, tm // d, stride=d), :] = rows
                else:
                    tmp[c, pl.ds(rp * (tm // prev) + a, tm // d, stride=d // prev), :] = rows
            sm[pl.ds(r, tm // d, stride=d), :] = m_ref[:, r * 128:(r + 1) * 128]
            sd[pl.ds(r, tm // d, stride=d), :] = d_ref[:, r * 128:(r + 1) * 128]
        if prev > 1:
            n = tm // prev
            for rp in range(prev):
                for c in range(ncol):
                    so[c, pl.ds(rp, n, stride=prev), :] = tmp[c, rp * n:(rp + 1) * n, :]

    rc, gw = 128, 256
    head_lane = lax.broadcasted_iota(jnp.int32, (rc, 128), 1) < ATT_HEADS
    for r0 in range(0, tm, rc):
        rows = slice(r0, r0 + rc)
        lses = ([m1_ref[rows, :] + jnp.log(d1_ref[rows, :])]
                + [sm[rows, :] + jnp.log(sd[rows, :]) for _, sm, sd in scr])
        mx = functools.reduce(jnp.maximum, lses)
        es = [jnp.exp(l - mx) for l in lses]
        inv = 1.0 / functools.reduce(jnp.add, es)
        packed = []
        for ei in es:
            wgt = jnp.where(head_lane, ei * inv, 0.0)
            hi = wgt.astype(BF16).astype(F32)
            lo = (wgt - hi).astype(BF16).astype(F32)
            packed.append((hi + pltpu.roll(lo, ATT_HEADS, 1)).astype(BF16))
        for c0 in range(0, ATT_WIDTH, gw):
            cols = slice(c0, c0 + gw)
            e = e_ref[:, cols]
            outs = [o1_ref[rows, cols].astype(F32)] + [
                jnp.concatenate([so[c, rows, :] for c in range(c0 // 128, (c0 + gw) // 128)], axis=1)
                for so, _, _ in scr]
            att = functools.reduce(jnp.add, [_dot(w, e) * o for w, o in zip(packed, outs)])
            out_ref[rows, cols] = (att * _silu(ag_ref[rows, cols].astype(F32))).astype(BF16)


def _merge(nat, res, proj):
    M = proj.shape[0]
    tm = 512
    expand = np.zeros((128, ATT_WIDTH), np.float32)
    for h in range(ATT_HEADS):
        expand[h, h * ATT_HEAD_DIM:(h + 1) * ATT_HEAD_DIM] = 1.0
        expand[ATT_HEADS + h, h * ATT_HEAD_DIM:(h + 1) * ATT_HEAD_DIM] = 1.0
    row = lambda rows, width: pl.BlockSpec((rows, width), lambda i: (i, 0))
    triple = lambda d: [row(tm // d, d * ATT_WIDTH), row(tm // d, d * 128), row(tm // d, d * 128)]
    in_specs = (triple(1) + [s for d in RESIDUE_DILATIONS for s in triple(d)]
                + [pl.BlockSpec((tm, ATT_WIDTH), lambda i: (i, COL_AG // ATT_WIDTH)),
                   pl.BlockSpec((128, ATT_WIDTH), lambda i: (0, 0))])
    scratch = []
    for _ in RESIDUE_DILATIONS:
        scratch += [pltpu.VMEM((ATT_WIDTH // 128, tm, 128), F32),
                    pltpu.VMEM((tm, 128), F32), pltpu.VMEM((tm, 128), F32)]
    scratch.append(pltpu.VMEM((ATT_WIDTH // 128, tm, 128), F32))
    return pl.pallas_call(
        functools.partial(_merge_kernel, tm=tm),
        grid=(M // tm,),
        in_specs=in_specs,
        out_specs=row(tm, ATT_WIDTH),
        out_shape=jax.ShapeDtypeStruct((M, ATT_WIDTH), BF16),
        scratch_shapes=scratch,
        compiler_params=_params(("arbitrary",)),
        name="merge",
    )(*nat, *[a for t in res for a in t], proj, jnp.asarray(expand, BF16))


def _gla_kernel(*refs, ts, nstep):
    fwd_in, bwd_in = refs[:5], refs[5:10]
    (upf_ref, upb_ref, gbf_ref, gbb_ref, gain_ref, out_ref,
     state, o_acc, qf_scr, kd_scr, ks_scr, oin_scr, st_scr) = refs[10:]
    C = GLA_CHUNK
    nchunk = ts // C
    chunks = [slice(c * C, (c + 1) * C) for c in range(nchunk)]
    i = pl.program_id(2)
    dirs = [(0, False, fwd_in, upf_ref, gbf_ref), (1, True, bwd_in, upb_ref, gbb_ref)]

    @pl.when(i == 0)
    def _():
        state[...] = jnp.zeros_like(state)


    log_gs = []
    for d, reverse, (q_ref, k_ref, v_ref, lr_ref, gg_ref), up_ref, gb_ref in dirs:
        lr_hi, lr_lo = _split_bf16(lr_ref[...])
        up_hi, up_lo = _split_bf16(up_ref[...])
        z = _dot(lr_hi, up_hi) + _dot(lr_hi, up_lo) + _dot(lr_lo, up_hi) + gb_ref[...]
        log_gs.append((jnp.minimum(z, 0.0) - jnp.log(1.0 + jnp.exp(-jnp.abs(z))))
                      * (1.0 / GLA_GATE_NORM))

    ri = lax.broadcasted_iota(jnp.int32, (C, C), 0)
    ci = lax.broadcasted_iota(jnp.int32, (C, C), 1)
    dec_cols = [[], []]
    for d, reverse, (q_ref, k_ref, v_ref, lr_ref, gg_ref), up_ref, gb_ref in dirs:
        tri = jnp.where((ci >= ri) if reverse else (ci <= ri), 1.0, 0.0).astype(BF16)
        for rows in chunks:
            g_hi, g_lo = _split_bf16(log_gs[d][rows])
            cum = _dot(tri, jnp.concatenate([g_hi, g_lo], axis=1))
            b = cum[:, :GLA_DK] + cum[:, GLA_DK:]
            b_edge = b[0:1] if reverse else b[C - 1:C]
            q = q_ref[rows, :].astype(F32)
            k = k_ref[rows, :].astype(F32)
            edge = jnp.exp(b_edge)
            kd = k * jnp.exp(-b)
            qf_scr[d, rows, :] = (q * jnp.exp(b + math.log(GLA_DK ** -0.5))).astype(BF16)
            kd_scr[d, rows, :] = kd.astype(BF16)
            ks_scr[d, rows, :] = (kd * edge).astype(BF16)
            dec = jnp.broadcast_to(edge, (GLA_DK, GLA_DK)).T
            dec_cols[d].append(jnp.concatenate([dec, dec], axis=1))

    atts = [[jnp.where((ci >= ri) if reverse else (ci <= ri),
                       _dot_nt(qf_scr[d, rows, :], kd_scr[d, rows, :]), 0.0).astype(BF16)
             for rows in chunks] for d, reverse, *_ in dirs]

    kvs = [[], []]
    for d, reverse, (q_ref, k_ref, v_ref, lr_ref, gg_ref), up_ref, gb_ref in dirs:
        for rows, att in zip(chunks, atts[d]):
            v = v_ref[rows, :]
            oin_scr[d, rows, :] = _dot(att, v)
            kvs[d].append(_dot_tn(ks_scr[d, rows, :], v))

    orders = [list(range(nchunk)), list(range(nchunk - 1, -1, -1))]
    for d, reverse, *_ in dirs:
        st = state[d]
        for c in orders[d]:
            st_scr[d, c] = st.astype(BF16)
            st = st * dec_cols[d][c] + kvs[d][c]
        state[d] = st

    for d, reverse, *_ in dirs:
        for c in orders[d]:
            rows = chunks[c]
            oin_scr[d, rows, :] = oin_scr[d, rows, :] + _dot(qf_scr[d, rows, :], st_scr[d, c])

    blocks = [i, nstep - 1 - i]

    @pl.when(i < nstep // 2)
    def _():
        for d, reverse, *_ in dirs:
            base = pl.multiple_of(blocks[d] * ts, ts)
            for rows in chunks:
                o_acc[pl.ds(base + rows.start, C), :] = oin_scr[d, rows, :]

    @pl.when(i >= nstep // 2)
    def _():
        for d, reverse, (q_ref, k_ref, v_ref, lr_ref, gg_ref), up_ref, gb_ref in dirs:
            base = pl.multiple_of(blocks[d] * ts, ts)
            for rows in chunks:
                dst = pl.ds(base + rows.start, C)
                tot = oin_scr[d, rows, :] + o_acc[dst, :]
                ms = jnp.mean(tot * tot, axis=-1, keepdims=True)
                g_o = tot * lax.rsqrt(ms + EPS) * gain_ref[...]
                out_ref[dst, :] = (g_o * _silu(gg_ref[rows, :].astype(F32))).astype(BF16)


def _gla(proj, lr, up_f, up_b, bias_f, bias_b, gain, batch, seq):
    ts = 2048
    nstep = seq // ts
    assert nstep % 2 == 0
    C = GLA_CHUNK
    p3 = proj.reshape(batch, seq, PROJ_WIDTH)
    lr3 = lr.reshape(batch, seq, LR_PAD)

    def direction_specs(step):
        def seq_block(width, col0):
            return pl.BlockSpec((None, ts, width), lambda b, h, i: (b, step(i), col0 // width + h))
        return [seq_block(GLA_DK, COL_GQ), seq_block(GLA_DK, COL_GK), seq_block(GLA_DV, COL_GV),
                pl.BlockSpec((None, ts, LR_PAD), lambda b, h, i: (b, step(i), 0)),
                seq_block(GLA_DV, COL_GG)]

    per_head = lambda rows, width: pl.BlockSpec((rows, width), lambda b, h, i: (0, h))
    in_specs = (direction_specs(lambda i: i) + direction_specs(lambda i: nstep - 1 - i)
                + [per_head(LR_PAD, GLA_DK), per_head(LR_PAD, GLA_DK),
                   per_head(1, GLA_DK), per_head(1, GLA_DK), per_head(1, GLA_DV)])
    dir_args = [p3, p3, p3, lr3, p3]
    return pl.pallas_call(
        functools.partial(_gla_kernel, ts=ts, nstep=nstep),
        grid=(batch, GLA_HEADS, nstep),
        in_specs=in_specs,
        out_specs=pl.BlockSpec((None, seq, GLA_DV), lambda b, h, i: (b, 0, h)),
        out_shape=jax.ShapeDtypeStruct((batch, seq, GLA_WIDTH), BF16),
        scratch_shapes=[pltpu.VMEM((2, GLA_DK, GLA_DV), F32),
                        pltpu.VMEM((seq, GLA_DV), F32),
                        pltpu.VMEM((2, ts, GLA_DK), BF16),
                        pltpu.VMEM((2, ts, GLA_DK), BF16),
                        pltpu.VMEM((2, ts, GLA_DK), BF16),
                        pltpu.VMEM((2, ts, GLA_DV), F32),
                        pltpu.VMEM((2, ts // C, GLA_DK, GLA_DV), BF16)],
        compiler_params=_params(("arbitrary", "arbitrary", "arbitrary")),
        name="gla",
    )(*dir_args, *dir_args, up_f, up_b, bias_f.reshape(1, GLA_KEY_WIDTH),
      bias_b.reshape(1, GLA_KEY_WIDTH), gain.reshape(1, GLA_WIDTH))


def _outproj_kernel(a_ref, g_ref, w_ref, x_ref, gate_ref, fg_ref, o_ref, *, final):
    y = _dot(jnp.concatenate([a_ref[...], g_ref[...]], axis=1), w_ref[...])
    xn = x_ref[...] + gate_ref[...] * y
    if final:
        ms = jnp.mean(xn * xn, axis=-1, keepdims=True)
        xn = xn * lax.rsqrt(ms + EPS) * fg_ref[...]
    o_ref[...] = xn


def _outproj(a_out, g_out, w_out_bf16, x2, gate, final_gain, seq, final):
    M, D = x2.shape
    tm = 512
    bpt = seq // tm
    return pl.pallas_call(
        functools.partial(_outproj_kernel, final=final),
        grid=(M // tm,),
        in_specs=[pl.BlockSpec((tm, ATT_WIDTH), lambda i: (i, 0)),
                  pl.BlockSpec((tm, GLA_WIDTH), lambda i: (i, 0)),
                  pl.BlockSpec((ATT_WIDTH + GLA_WIDTH, D), lambda i: (0, 0)),
                  pl.BlockSpec((tm, D), lambda i: (i, 0)),
                  pl.BlockSpec((None, 1, D), lambda i: (i // bpt, 0, 0)),
                  pl.BlockSpec((1, D), lambda i: (0, 0))],
        out_specs=pl.BlockSpec((tm, D), lambda i: (i, 0)),
        out_shape=jax.ShapeDtypeStruct((M, D), F32),
        compiler_params=_params(("arbitrary",)),
        name="outproj",
    )(a_out, g_out, w_out_bf16, x2, gate, final_gain.reshape(1, D))


def kernel(x, c, w_cond, b_cond, w_in, gla_gate_up_fwd, gla_gate_bias_fwd, gla_gate_up_bwd,
           gla_gate_bias_bwd, gla_norm_gain, rel_bias, w_out, final_gain):
    B, S, D = x.shape
    depth = w_cond.shape[0]
    R = GLA_GATE_RANK
    xs = x.reshape(B * S, D)
    for layer in range(depth):
        mod = _mod(c, w_cond[layer], b_cond[layer])
        shift, scale, gate = [m.reshape(B, 1, D) for m in jnp.split(mod, 3, axis=-1)]

        proj, lr, *res_qkv = _inproj(xs, scale, shift, _wprep(w_in, layer), S)

        def rows2d(t, d):
            return [a.reshape(B * S // d, -1) for a in t]

        nat = rows2d(_attn_pattern(proj.reshape(B, S, PROJ_WIDTH), _bias_tiles(rel_bias, 1), B, S, 1), 1)
        res = [rows2d(_attn_pattern(qkv.reshape(B, S // d, -1), _bias_tiles(rel_bias, d), B, S, d), d)
               for d, qkv in zip(RESIDUE_DILATIONS, res_qkv)]
        a_out = _merge(nat, res, proj)

        up_f = jnp.pad(gla_gate_up_fwd[layer], ((0, LR_PAD - R), (0, 0)))
        up_b = jnp.pad(gla_gate_up_bwd[layer], ((R, LR_PAD - 2 * R), (0, 0)))
        g_out = _gla(proj, lr, up_f, up_b, gla_gate_bias_fwd[layer], gla_gate_bias_bwd[layer],
                     gla_norm_gain[layer], B, S)

        xs = _outproj(a_out, g_out.reshape(B * S, GLA_WIDTH), w_out[layer].astype(BF16),
                      xs, gate, final_gain, S, final=layer == depth - 1)
    return xs.reshape(B, S, D)
```

```python
import functools
import math

import jax
import jax.numpy as jnp
import numpy as np
from jax import lax
from jax.experimental import pallas as pl
from jax.experimental.pallas import tpu as pltpu

D_MODEL = 2048
ATT_WIDTH = 1024
ATT_HEADS = 16
ATT_HEAD_DIM = 64
DILATED_PATTERNS = ((128, 1), (512, 4), (2048, 16))
ATT_STEPS = 64
GLA_WIDTH = 1024
GLA_HEADS = 4
GLA_KEY_WIDTH = 512
GLA_DK = 128
GLA_DV = 256
GLA_GATE_RANK = 16
GLA_GATE_NORM = 16.0
GLA_CHUNK = 64
REL_BUCKETS = 32
REL_MAX_DIST = 1024
EPS = 1e-6
NEG_INF = -1e30

PROJ_WIDTH = 4 * ATT_WIDTH + 2 * GLA_KEY_WIDTH + 2 * GLA_WIDTH
COL_AQ, COL_AK, COL_AV, COL_AG = 0, 1024, 2048, 3072
COL_GQ, COL_GK, COL_GV, COL_GG = 4096, 4608, 5120, 6144
LR_PAD = 128
ATT_QKV_TILES = 3
RESIDUE_DILATIONS = tuple(d for _, d in DILATED_PATTERNS if d > 1)
ATT_TQ = 128
ATT_TK = ATT_TQ + 2 * ATT_STEPS
ATT_GROUP_HEADS = 4
ATT_HEAD_SETS = 1

VMEM_LIMIT = 56 * 1024 * 1024

BF16 = jnp.bfloat16
F32 = jnp.float32


def _params(sem):
    return pltpu.CompilerParams(dimension_semantics=sem, vmem_limit_bytes=VMEM_LIMIT)


def _dot(a, b):
    return jnp.dot(a, b, preferred_element_type=F32)


def _dot_nt(a, b):
    return lax.dot_general(a, b, (((1,), (1,)), ((), ())), preferred_element_type=F32)


def _dot_tn(a, b):
    return lax.dot_general(a, b, (((0,), (0,)), ((), ())), preferred_element_type=F32)


def _split_bf16(x):
    hi = x.astype(BF16)
    lo = (x - hi.astype(F32)).astype(BF16)
    return hi, lo


def _silu(x):
    return x / (1.0 + jnp.exp(-x))


def _mod_kernel(c_ref, w_ref, b_ref, o_ref):
    s_hi, s_lo = _split_bf16(_silu(c_ref[...]))
    w_hi, w_lo = _split_bf16(w_ref[...])
    o_ref[...] = _dot(s_hi, w_hi) + _dot(s_lo, w_hi) + _dot(s_hi, w_lo) + b_ref[...]


def _mod(c, w_cond, b_cond):
    B, D = c.shape
    N = w_cond.shape[1]
    tn = 768
    cp = jnp.pad(c, ((0, 8 - B), (0, 0)))
    out = pl.pallas_call(
        _mod_kernel,
        grid=(N // tn,),
        in_specs=[pl.BlockSpec((8, D), lambda j: (0, 0)),
                  pl.BlockSpec((D, tn), lambda j: (0, j)),
                  pl.BlockSpec((1, tn), lambda j: (0, j))],
        out_specs=pl.BlockSpec((8, tn), lambda j: (0, j)),
        out_shape=jax.ShapeDtypeStruct((8, N), F32),
        compiler_params=_params(("arbitrary",)),
        name="mod",
    )(cp, w_cond, b_cond.reshape(1, N))
    return out[:B]


def _wprep_kernel(w_ref, o_ref, *, tn, ncols):
    col = pl.program_id(0) * tn + lax.broadcasted_iota(jnp.int32, (1, tn), 1)
    scale = jnp.where(col < ATT_WIDTH, ATT_HEAD_DIM ** -0.5, 1.0)
    o_ref[...] = jnp.where(col < ncols, w_ref[...] * scale, 0.0).astype(BF16)


def _wprep(w_all, layer):
    _, D, ncols = w_all.shape
    width = PROJ_WIDTH + LR_PAD
    tn = 384
    assert width % tn == 0
    return pl.pallas_call(
        functools.partial(_wprep_kernel, tn=tn, ncols=ncols),
        grid=(width // tn,),
        in_specs=[pl.BlockSpec((None, D, tn), lambda j: (layer, 0, j))],
        out_specs=pl.BlockSpec((D, tn), lambda j: (0, j)),
        out_shape=jax.ShapeDtypeStruct((D, width), BF16),
        compiler_params=_params(("arbitrary",)),
        name="wprep",
    )(w_all)


def _inproj_kernel(x_ref, scale_ref, shift_ref, w_ref, p_ref, lr_ref, *rest, tm, tn):
    nres = len(RESIDUE_DILATIONS)
    res_refs, h_scr, acc_scr = rest[:nres], rest[nres], rest[nres + 1:]
    j = pl.program_id(1)

    @pl.when(j == 0)
    def _():
        x = x_ref[...]
        ms = jnp.mean(x * x, axis=-1, keepdims=True)
        h = x * lax.rsqrt(ms + EPS) * (1.0 + scale_ref[...]) + shift_ref[...]
        hb = h.astype(BF16)
        h_scr[...] = hb
        lr_ref[...] = _dot(hb, w_ref[:, PROJ_WIDTH:PROJ_WIDTH + LR_PAD])

    @pl.when(j >= ATT_QKV_TILES)
    def _():
        w = w_ref[:, pl.ds(pl.multiple_of(j * tn, tn), tn)]
        p_ref[...] = _dot(h_scr[...], w).astype(BF16)

    @pl.when(j < ATT_QKV_TILES)
    def _():
        h = h_scr[...]
        chunk = 256
        for c0 in range(0, tn, chunk):
            acc = _dot(h, w_ref[:, pl.ds(pl.multiple_of(j * tn + c0, chunk), chunk)])
            p_ref[:, c0:c0 + chunk] = acc.astype(BF16)
            for c in range(c0 // 128, (c0 + chunk) // 128):
                lanes = slice(c * 128 - c0, (c + 1) * 128 - c0)
                src, prev_d = acc_scr[0], 1
                src[c] = acc[:, lanes]
                for lvl, (ref, d) in enumerate(zip(res_refs, RESIDUE_DILATIONS)):
                    ratio, n = d // prev_d, tm // d
                    dst = acc_scr[lvl + 1] if lvl + 1 < len(RESIDUE_DILATIONS) else None
                    for rp in range(prev_d):
                        for a in range(ratio):
                            r = rp + prev_d * a
                            rows = src[c, pl.ds(rp * (tm // prev_d) + a, n, stride=ratio), :]
                            ref[:, r * tn + c * 128:r * tn + (c + 1) * 128] = rows.astype(BF16)
                            if dst is not None:
                                dst[c, r * n:(r + 1) * n, :] = rows
                    src, prev_d = dst, d


def _inproj(x2, scale, shift, w_main, seq):
    M, D = x2.shape
    tm, tn = 512, ATT_WIDTH
    bpt = seq // tm
    last_qkv = ATT_QKV_TILES - 1
    res_specs = [pl.BlockSpec((tm // d, d * tn), lambda i, j: (i, jnp.minimum(j, last_qkv)))
                 for d in RESIDUE_DILATIONS]
    res_shapes = [jax.ShapeDtypeStruct((M // d, ATT_QKV_TILES * d * tn), BF16)
                  for d in RESIDUE_DILATIONS]
    return pl.pallas_call(
        functools.partial(_inproj_kernel, tm=tm, tn=tn),
        grid=(M // tm, PROJ_WIDTH // tn),
        in_specs=[pl.BlockSpec((tm, D), lambda i, j: (i, 0)),
                  pl.BlockSpec((None, 1, D), lambda i, j: (i // bpt, 0, 0)),
                  pl.BlockSpec((None, 1, D), lambda i, j: (i // bpt, 0, 0)),
                  pl.BlockSpec(w_main.shape, lambda i, j: (0, 0), pipeline_mode=pl.Buffered(1))],
        out_specs=[pl.BlockSpec((tm, tn), lambda i, j: (i, j)),
                   pl.BlockSpec((tm, LR_PAD), lambda i, j: (i, 0))] + res_specs,
        out_shape=[jax.ShapeDtypeStruct((M, PROJ_WIDTH), BF16),
                   jax.ShapeDtypeStruct((M, LR_PAD), F32)] + res_shapes,
        scratch_shapes=[pltpu.VMEM((tm, D), BF16)]
                       + [pltpu.VMEM((tn // 128, tm, 128), F32) for _ in RESIDUE_DILATIONS],
        compiler_params=_params(("arbitrary", "arbitrary")),
        name="inproj",
    )(x2, scale, shift, w_main)


def _t5_bucket_np(rel):
    nb = REL_BUCKETS // 2
    max_exact = nb // 2
    n = np.abs(rel)
    large = max_exact + (np.log(np.maximum(n, 1) / max_exact)
                         / np.log(REL_MAX_DIST / max_exact) * (nb - max_exact)).astype(np.int32)
    large = np.minimum(large, nb - 1)
    return (np.where(rel > 0, nb, 0) + np.where(n < max_exact, n, large)).astype(np.int32)


def _bias_kernel(rbt_ref, bucket_ref, mask_ref, o_ref):
    rbt = rbt_ref[...]
    bucket = bucket_ref[...]
    ids = lax.broadcasted_iota(jnp.int32, (REL_BUCKETS, bucket.shape[1]), 0)
    onehot = jnp.where(ids == bucket, 1.0, 0.0).astype(BF16)
    hi = rbt.astype(BF16)
    rest = rbt - hi.astype(F32)
    mid = rest.astype(BF16)
    lo = (rest - mid.astype(F32)).astype(BF16)
    tbl = _dot(hi, onehot) + _dot(mid, onehot) + _dot(lo, onehot)
    for v in range(3):
        o_ref[v] = jnp.where(mask_ref[v] > 0.5, tbl, NEG_INF)


def _bias_tiles(rel_bias):
    w, tq, tk = ATT_STEPS, ATT_TQ, ATT_TK
    npat = len(DILATED_PATTERNS)
    qi = np.arange(tq)[:, None]
    kj = np.arange(tk)[None, :]
    step = kj - w - qi
    band = np.abs(step) <= w
    buckets = np.stack([_t5_bucket_np(step * d).reshape(1, tq * tk) for _, d in DILATED_PATTERNS])
    masks = np.stack([band & (kj >= w), band, band & (kj < tk - w)]).astype(np.float32)
    masks = masks.reshape(3, 1, tq * tk)
    out = pl.pallas_call(
        _bias_kernel,
        grid=(npat,),
        in_specs=[pl.BlockSpec((ATT_HEADS, REL_BUCKETS), lambda p: (0, 0)),
                  pl.BlockSpec((None, 1, tq * tk), lambda p: (p, 0, 0)),
                  pl.BlockSpec((3, 1, tq * tk), lambda p: (0, 0, 0))],
        out_specs=pl.BlockSpec((None, 3, ATT_HEADS, tq * tk), lambda p: (p, 0, 0, 0)),
        out_shape=jax.ShapeDtypeStruct((npat, 3, ATT_HEADS, tq * tk), F32),
        compiler_params=_params(("arbitrary",)),
        name="bias",
    )(rel_bias.T, jnp.asarray(buckets), jnp.asarray(masks))
    return {d: out[p].reshape(3, ATT_HEADS, tq, tk) for p, (_, d) in enumerate(DILATED_PATTERNS)}


def _attn_kernel(q_ref, kp_ref, km_ref, kn_ref, vp_ref, vm_ref, vn_ref, bias_ref,
                 o_ref, m_ref, den_ref, kbuf, vbuf, s_scr, p_scr, inv_scr, *, tb):
    w, tq, tk = ATT_STEPS, ATT_TQ, ATT_TK
    nsub = tb // tq
    npair = ATT_HEADS // 2
    gh = ATT_GROUP_HEADS
    gw = gh * ATT_HEAD_DIM
    ngroup = ATT_HEADS // gh
    nk = tb + 2 * w
    i = pl.program_id(2)
    first = i == 0
    last = i == pl.num_programs(2) - 1

    row = 0
    for kpart, vpart in ((kp_ref, vp_ref), (km_ref, vm_ref), (kn_ref, vn_ref)):
        rows = slice(row, row + kpart.shape[0])
        kbuf[rows] = kpart[...]
        vbuf[rows] = vpart[...]
        row += kpart.shape[0]
    slot = lax.broadcasted_iota(jnp.int32, (1, gw), 1) // ATT_HEAD_DIM

    half = tq // 2
    lower_half = lax.broadcasted_iota(jnp.int32, (half, 128), 1) < ATT_HEAD_DIM
    lower_q = lax.broadcasted_iota(jnp.int32, (tq, 128), 1) < ATT_HEAD_DIM

    nsets = ATT_HEAD_SETS

    def head_pairs(hs):
        return range(hs * npair // nsets, (hs + 1) * npair // nsets)

    def tile(j):
        return pl.multiple_of(j * tq, tq)

    def logits(j, buf, hs):
        qs = tile(j)
        for hp in head_pairs(hs):
            cp = slice(hp * 128, (hp + 1) * 128)
            q = q_ref[pl.ds(qs, tq), cp]
            zero = jnp.zeros_like(q)
            q2 = jnp.concatenate([jnp.where(lower_q, q, zero), jnp.where(lower_q, zero, q)], axis=0)
            s2 = _dot_nt(q2, kbuf[pl.ds(qs, tk), cp])
            s_scr[buf, 2 * hp] = s2[:tq]
            s_scr[buf, 2 * hp + 1] = s2[tq:]

    def softmax(j, buf, hs):
        qs = tile(j)
        var = jnp.where(jnp.logical_and(first, j == 0), 0,
                        jnp.where(jnp.logical_and(last, j == nsub - 1), 2, 1))
        if hs == 0:
            m_ref[pl.ds(qs, tq), :] = jnp.zeros((tq, 128), F32)
            den_ref[pl.ds(qs, tq), :] = jnp.ones((tq, 128), F32)
        for hp in head_pairs(hs):
            for r0 in (0, half):
                dens = []
                for h in (2 * hp, 2 * hp + 1):
                    s = s_scr[buf, h, r0:r0 + half, :] + bias_ref[var, h, r0:r0 + half, :]
                    m = jnp.max(s, axis=-1, keepdims=True)
                    p = jnp.exp(s - m)
                    den = jnp.sum(p, axis=-1, keepdims=True)
                    p_scr[buf, h // gh, r0:r0 + half, (h % gh) * tk:(h % gh + 1) * tk] = p.astype(BF16)
                    m_ref[pl.ds(qs + r0, half), h:h + 1] = m
                    den_ref[pl.ds(qs + r0, half), h:h + 1] = den
                    dens.append(den)
                pair = hp % (gh // 2)
                inv_scr[buf, (2 * hp) // gh, r0:r0 + half, pair * 128:(pair + 1) * 128] = (
                    1.0 / jnp.where(lower_half, dens[0], dens[1]))

    def outputs(j, buf, hs):
        qs = tile(j)
        for g in range(hs * ngroup // nsets, (hs + 1) * ngroup // nsets):
            cg = slice(g * gw, (g + 1) * gw)
            vw = vbuf[pl.ds(qs, tk), cg]
            v_stack = jnp.concatenate([jnp.where(slot == s, vw, jnp.zeros_like(vw))
                                       for s in range(gh)], axis=0)
            o = _dot(p_scr[buf, g], v_stack)
            o_ref[pl.ds(qs, tq), cg] = (o * inv_scr[buf, g]).astype(BF16)

    for hs in range(nsets):
        def one_tile(j, carry, hs=hs):
            logits(j, 0, hs)
            softmax(j, 0, hs)
            outputs(j, 0, hs)
            return carry

        lax.fori_loop(0, nsub, one_tile, 0)


def _attn_pattern(qkv, bias, batch, seq, dilation):
    w = ATT_STEPS
    L = seq // dilation
    tb = min(1024, L)
    nblk = L // tb
    hb = tb // w
    nhalo = L // w

    def main(j):
        return pl.BlockSpec((None, tb, ATT_WIDTH), lambda b, r, i: (b, i, j * dilation + r))

    def prev(j):
        return pl.BlockSpec((None, w, ATT_WIDTH),
                            lambda b, r, i: (b, jnp.maximum(i * hb - 1, 0), j * dilation + r))

    def nxt(j):
        return pl.BlockSpec((None, w, ATT_WIDTH),
                            lambda b, r, i: (b, jnp.minimum((i + 1) * hb, nhalo - 1), j * dilation + r))

    nk, gh = tb + 2 * w, ATT_GROUP_HEADS
    return pl.pallas_call(
        functools.partial(_attn_kernel, tb=tb),
        grid=(batch, dilation, nblk),
        in_specs=[main(0), prev(1), main(1), nxt(1), prev(2), main(2), nxt(2),
                  pl.BlockSpec((3, ATT_HEADS, ATT_TQ, ATT_TK), lambda b, r, i: (0, 0, 0, 0),
                               pipeline_mode=pl.Buffered(1))],
        out_specs=[pl.BlockSpec((None, tb, ATT_WIDTH), lambda b, r, i: (b, i, r)),
                   pl.BlockSpec((None, tb, 128), lambda b, r, i: (b, i, r)),
                   pl.BlockSpec((None, tb, 128), lambda b, r, i: (b, i, r))],
        out_shape=[jax.ShapeDtypeStruct((batch, L, dilation * ATT_WIDTH), BF16),
                   jax.ShapeDtypeStruct((batch, L, dilation * 128), F32),
                   jax.ShapeDtypeStruct((batch, L, dilation * 128), F32)],
        scratch_shapes=[pltpu.VMEM((nk, ATT_WIDTH), BF16),
                        pltpu.VMEM((nk, ATT_WIDTH), BF16),
                        pltpu.VMEM((1, ATT_HEADS, ATT_TQ, ATT_TK), F32),
                        pltpu.VMEM((1, ATT_HEADS // gh, ATT_TQ, gh * ATT_TK), BF16),
                        pltpu.VMEM((1, ATT_HEADS // gh, ATT_TQ, gh * ATT_HEAD_DIM), F32)],
        compiler_params=_params(("arbitrary", "arbitrary", "arbitrary")),
        name=f"attn_d{dilation}",
    )(qkv, qkv, qkv, qkv, qkv, qkv, qkv, bias)


def _merge_kernel(*refs, tm):
    nres = len(RESIDUE_DILATIONS)
    o1_ref, m1_ref, d1_ref = refs[:3]
    res = [refs[3 + 3 * n:6 + 3 * n] for n in range(nres)]
    ag_ref, e_ref, out_ref = refs[3 + 3 * nres:6 + 3 * nres]
    scr = [refs[6 + 3 * nres + 3 * n:9 + 3 * nres + 3 * n] for n in range(nres)]
    tmp = refs[6 + 6 * nres]

    ncol = ATT_WIDTH // 128
    for d, (o_ref, m_ref, d_ref), (so, sm, sd) in zip(RESIDUE_DILATIONS, res, scr):
        prev = d // 4 if d > 4 else 1
        for r in range(d):
            rp, a = r % prev, r // prev
            for c in range(ncol):
                col = r * ATT_WIDTH + c * 128
                rows = o_ref[:, col:col + 128].astype(F32)
                if prev == 1:
                    so[c, pl.ds(r, tm // d, stride=d), :] = rows
                else:
                    tmp[c, pl.ds(rp * (tm // prev) + a, tm // d, stride=d // prev), :] = rows
            sm[pl.ds(r, tm // d, stride=d), :] = m_ref[:, r * 128:(r + 1) * 128]
            sd[pl.ds(r, tm // d, stride=d), :] = d_ref[:, r * 128:(r + 1) * 128]
        if prev > 1:
            n = tm // prev
            for rp in range(prev):
                for c in range(ncol):
                    so[c, pl.ds(rp, n, stride=prev), :] = tmp[c, rp * n:(rp + 1) * n, :]

    rc, gw = 128, 256
    head_lane = lax.broadcasted_iota(jnp.int32, (rc, 128), 1) < ATT_HEADS
    for r0 in range(0, tm, rc):
        rows = slice(r0, r0 + rc)
        lses = ([m1_ref[rows, :] + jnp.log(d1_ref[rows, :])]
                + [sm[rows, :] + jnp.log(sd[rows, :]) for _, sm, sd in scr])
        mx = functools.reduce(jnp.maximum, lses)
        es = [jnp.exp(l - mx) for l in lses]
        inv = 1.0 / functools.reduce(jnp.add, es)
        packed = []
        for ei in es:
            wgt = jnp.where(head_lane, ei * inv, 0.0)
            hi = wgt.astype(BF16).astype(F32)
            lo = (wgt - hi).astype(BF16).astype(F32)
            packed.append((hi + pltpu.roll(lo, ATT_HEADS, 1)).astype(BF16))
        for c0 in range(0, ATT_WIDTH, gw):
            cols = slice(c0, c0 + gw)
            e = e_ref[:, cols]
            outs = [o1_ref[rows, cols].astype(F32)] + [
                jnp.concatenate([so[c, rows, :] for c in range(c0 // 128, (c0 + gw) // 128)], axis=1)
                for so, _, _ in scr]
            att = functools.reduce(jnp.add, [_dot(w, e) * o for w, o in zip(packed, outs)])
            out_ref[rows, cols] = (att * _silu(ag_ref[rows, cols].astype(F32))).astype(BF16)


def _merge(nat, res, proj):
    M = proj.shape[0]
    tm = 512
    expand = np.zeros((128, ATT_WIDTH), np.float32)
    for h in range(ATT_HEADS):
        expand[h, h * ATT_HEAD_DIM:(h + 1) * ATT_HEAD_DIM] = 1.0
        expand[ATT_HEADS + h, h * ATT_HEAD_DIM:(h + 1) * ATT_HEAD_DIM] = 1.0
    row = lambda rows, width: pl.BlockSpec((rows, width), lambda i: (i, 0))
    triple = lambda d: [row(tm // d, d * ATT_WIDTH), row(tm // d, d * 128), row(tm // d, d * 128)]
    in_specs = (triple(1) + [s for d in RESIDUE_DILATIONS for s in triple(d)]
                + [pl.BlockSpec((tm, ATT_WIDTH), lambda i: (i, COL_AG // ATT_WIDTH)),
                   pl.BlockSpec((128, ATT_WIDTH), lambda i: (0, 0))])
    scratch = []
    for _ in RESIDUE_DILATIONS:
        scratch += [pltpu.VMEM((ATT_WIDTH // 128, tm, 128), F32),
                    pltpu.VMEM((tm, 128), F32), pltpu.VMEM((tm, 128), F32)]
    scratch.append(pltpu.VMEM((ATT_WIDTH // 128, tm, 128), F32))
    return pl.pallas_call(
        functools.partial(_merge_kernel, tm=tm),
        grid=(M // tm,),
        in_specs=in_specs,
        out_specs=row(tm, ATT_WIDTH),
        out_shape=jax.ShapeDtypeStruct((M, ATT_WIDTH), BF16),
        scratch_shapes=scratch,
        compiler_params=_params(("arbitrary",)),
        name="merge",
    )(*nat, *[a for t in res for a in t], proj, jnp.asarray(expand, BF16))


def _gla_kernel(*refs, ts, nstep):
    fwd_in, bwd_in = refs[:5], refs[5:10]
    (upf_ref, upb_ref, gbf_ref, gbb_ref, gain_ref, out_ref,
     state, o_acc, qf_scr, kd_scr, ks_scr, oin_scr, st_scr) = refs[10:]
    C = GLA_CHUNK
    nchunk = ts // C
    chunks = [slice(c * C, (c + 1) * C) for c in range(nchunk)]
    i = pl.program_id(2)
    dirs = [(0, False, fwd_in, upf_ref, gbf_ref), (1, True, bwd_in, upb_ref, gbb_ref)]

    @pl.when(i == 0)
    def _():
        state[...] = jnp.zeros_like(state)


    log_gs = []
    for d, reverse, (q_ref, k_ref, v_ref, lr_ref, gg_ref), up_ref, gb_ref in dirs:
        lr_hi, lr_lo = _split_bf16(lr_ref[...])
        up_hi, up_lo = _split_bf16(up_ref[...])
        z = _dot(lr_hi, up_hi) + _dot(lr_hi, up_lo) + _dot(lr_lo, up_hi) + gb_ref[...]
        log_gs.append((jnp.minimum(z, 0.0) - jnp.log(1.0 + jnp.exp(-jnp.abs(z))))
                      * (1.0 / GLA_GATE_NORM))

    ri = lax.broadcasted_iota(jnp.int32, (C, C), 0)
    ci = lax.broadcasted_iota(jnp.int32, (C, C), 1)
    dec_cols = [[], []]
    for d, reverse, (q_ref, k_ref, v_ref, lr_ref, gg_ref), up_ref, gb_ref in dirs:
        tri = jnp.where((ci >= ri) if reverse else (ci <= ri), 1.0, 0.0).astype(BF16)
        for rows in chunks:
            g_hi, g_lo = _split_bf16(log_gs[d][rows])
            cum = _dot(tri, jnp.concatenate([g_hi, g_lo], axis=1))
            b = cum[:, :GLA_DK] + cum[:, GLA_DK:]
            b_edge = b[0:1] if reverse else b[C - 1:C]
            q = q_ref[rows, :].astype(F32)
            k = k_ref[rows, :].astype(F32)
            edge = jnp.exp(b_edge)
            kd = k * jnp.exp(-b)
            qf_scr[d, rows, :] = (q * jnp.exp(b + math.log(GLA_DK ** -0.5))).astype(BF16)
            kd_scr[d, rows, :] = kd.astype(BF16)
            ks_scr[d, rows, :] = (kd * edge).astype(BF16)
            dec = jnp.broadcast_to(edge, (GLA_DK, GLA_DK)).T
            dec_cols[d].append(jnp.concatenate([dec, dec], axis=1))

    atts = [[jnp.where((ci >= ri) if reverse else (ci <= ri),
                       _dot_nt(qf_scr[d, rows, :], kd_scr[d, rows, :]), 0.0).astype(BF16)
             for rows in chunks] for d, reverse, *_ in dirs]

    kvs = [[], []]
    for d, reverse, (q_ref, k_ref, v_ref, lr_ref, gg_ref), up_ref, gb_ref in dirs:
        for rows, att in zip(chunks, atts[d]):
            v = v_ref[rows, :]
            oin_scr[d, rows, :] = _dot(att, v)
            kvs[d].append(_dot_tn(ks_scr[d, rows, :], v))

    orders = [list(range(nchunk)), list(range(nchunk - 1, -1, -1))]
    for d, reverse, *_ in dirs:
        st = state[d]
        for c in orders[d]:
            st_scr[d, c] = st.astype(BF16)
            st = st * dec_cols[d][c] + kvs[d][c]
        state[d] = st

    for d, reverse, *_ in dirs:
        for c in orders[d]:
            rows = chunks[c]
            oin_scr[d, rows, :] = oin_scr[d, rows, :] + _dot(qf_scr[d, rows, :], st_scr[d, c])

    blocks = [i, nstep - 1 - i]

    @pl.when(i < nstep // 2)
    def _():
        for d, reverse, *_ in dirs:
            base = pl.multiple_of(blocks[d] * ts, ts)
            for rows in chunks:
                o_acc[pl.ds(base + rows.start, C), :] = oin_scr[d, rows, :]

    @pl.when(i >= nstep // 2)
    def _():
        for d, reverse, (q_ref, k_ref, v_ref, lr_ref, gg_ref), up_ref, gb_ref in dirs:
            base = pl.multiple_of(blocks[d] * ts, ts)
            for rows in chunks:
                dst = pl.ds(base + rows.start, C)
                tot = oin_scr[d, rows, :] + o_acc[dst, :]
                ms = jnp.mean(tot * tot, axis=-1, keepdims=True)
                g_o = tot * lax.rsqrt(ms + EPS) * gain_ref[...]
                out_ref[dst, :] = (g_o * _silu(gg_ref[rows, :].astype(F32))).astype(BF16)


def _gla(proj, lr, up_f, up_b, bias_f, bias_b, gain, batch, seq):
    ts = 2048
    nstep = seq // ts
    assert nstep % 2 == 0
    C = GLA_CHUNK
    p3 = proj.reshape(batch, seq, PROJ_WIDTH)
    lr3 = lr.reshape(batch, seq, LR_PAD)

    def direction_specs(step):
        def seq_block(width, col0):
            return pl.BlockSpec((None, ts, width), lambda b, h, i: (b, step(i), col0 // width + h))
        return [seq_block(GLA_DK, COL_GQ), seq_block(GLA_DK, COL_GK), seq_block(GLA_DV, COL_GV),
                pl.BlockSpec((None, ts, LR_PAD), lambda b, h, i: (b, step(i), 0)),
                seq_block(GLA_DV, COL_GG)]

    per_head = lambda rows, width: pl.BlockSpec((rows, width), lambda b, h, i: (0, h))
    in_specs = (direction_specs(lambda i: i) + direction_specs(lambda i: nstep - 1 - i)
                + [per_head(LR_PAD, GLA_DK), per_head(LR_PAD, GLA_DK),
                   per_head(1, GLA_DK), per_head(1, GLA_DK), per_head(1, GLA_DV)])
    dir_args = [p3, p3, p3, lr3, p3]
    return pl.pallas_call(
        functools.partial(_gla_kernel, ts=ts, nstep=nstep),
        grid=(batch, GLA_HEADS, nstep),
        in_specs=in_specs,
        out_specs=pl.BlockSpec((None, seq, GLA_DV), lambda b, h, i: (b, 0, h)),
        out_shape=jax.ShapeDtypeStruct((batch, seq, GLA_WIDTH), BF16),
        scratch_shapes=[pltpu.VMEM((2, GLA_DK, GLA_DV), F32),
                        pltpu.VMEM((seq, GLA_DV), F32),
                        pltpu.VMEM((2, ts, GLA_DK), BF16),
                        pltpu.VMEM((2, ts, GLA_DK), BF16),
                        pltpu.VMEM((2, ts, GLA_DK), BF16),
                        pltpu.VMEM((2, ts, GLA_DV), F32),
                        pltpu.VMEM((2, ts // C, GLA_DK, GLA_DV), BF16)],
        compiler_params=_params(("arbitrary", "arbitrary", "arbitrary")),
        name="gla",
    )(*dir_args, *dir_args, up_f, up_b, bias_f.reshape(1, GLA_KEY_WIDTH),
      bias_b.reshape(1, GLA_KEY_WIDTH), gain.reshape(1, GLA_WIDTH))


def _outproj_kernel(a_ref, g_ref, w_ref, x_ref, gate_ref, fg_ref, o_ref, *, final):
    y = _dot(jnp.concatenate([a_ref[...], g_ref[...]], axis=1), w_ref[...])
    xn = x_ref[...] + gate_ref[...] * y
    if final:
        ms = jnp.mean(xn * xn, axis=-1, keepdims=True)
        xn = xn * lax.rsqrt(ms + EPS) * fg_ref[...]
    o_ref[...] = xn


def _outproj(a_out, g_out, w_out_bf16, x2, gate, final_gain, seq, final):
    M, D = x2.shape
    tm = 512
    bpt = seq // tm
    return pl.pallas_call(
        functools.partial(_outproj_kernel, final=final),
        grid=(M // tm,),
        in_specs=[pl.BlockSpec((tm, ATT_WIDTH), lambda i: (i, 0)),
                  pl.BlockSpec((tm, GLA_WIDTH), lambda i: (i, 0)),
                  pl.BlockSpec((ATT_WIDTH + GLA_WIDTH, D), lambda i: (0, 0)),
                  pl.BlockSpec((tm, D), lambda i: (i, 0)),
                  pl.BlockSpec((None, 1, D), lambda i: (i // bpt, 0, 0)),
                  pl.BlockSpec((1, D), lambda i: (0, 0))],
        out_specs=pl.BlockSpec((tm, D), lambda i: (i, 0)),
        out_shape=jax.ShapeDtypeStruct((M, D), F32),
        compiler_params=_params(("arbitrary",)),
        name="outproj",
    )(a_out, g_out, w_out_bf16, x2, gate, final_gain.reshape(1, D))


def kernel(x, c, w_cond, b_cond, w_in, gla_gate_up_fwd, gla_gate_bias_fwd, gla_gate_up_bwd,
           gla_gate_bias_bwd, gla_norm_gain, rel_bias, w_out, final_gain):
    B, S, D = x.shape
    depth = w_cond.shape[0]
    R = GLA_GATE_RANK
    xs = x.reshape(B * S, D)
    for layer in range(depth):
        mod = _mod(c, w_cond[layer], b_cond[layer])
        shift, scale, gate = [m.reshape(B, 1, D) for m in jnp.split(mod, 3, axis=-1)]

        proj, lr, *res_qkv = _inproj(xs, scale, shift, _wprep(w_in, layer), S)

        def rows2d(t, d):
            return [a.reshape(B * S // d, -1) for a in t]

        bias = _bias_tiles(rel_bias)
        nat = rows2d(_attn_pattern(proj.reshape(B, S, PROJ_WIDTH), bias[1], B, S, 1), 1)
        res = [rows2d(_attn_pattern(qkv.reshape(B, S // d, -1), bias[d], B, S, d), d)
               for d, qkv in zip(RESIDUE_DILATIONS, res_qkv)]
        a_out = _merge(nat, res, proj)

        up_f = jnp.pad(gla_gate_up_fwd[layer], ((0, LR_PAD - R), (0, 0)))
        up_b = jnp.pad(gla_gate_up_bwd[layer], ((R, LR_PAD - 2 * R), (0, 0)))
        g_out = _gla(proj, lr, up_f, up_b, gla_gate_bias_fwd[layer], gla_gate_bias_bwd[layer],
                     gla_norm_gain[layer], B, S)

        xs = _outproj(a_out, g_out.reshape(B * S, GLA_WIDTH), w_out[layer].astype(BF16),
                      xs, gate, final_gain, S, final=layer == depth - 1)
    return xs.reshape(B, S, D)
```
